```python
import math
import jax, jax.numpy as jnp
from jax import lax
import numpy as np

D_MODEL = 1024
BATCH = 2
SEQ = 16384
DEPTH = 1
DEC_BATCH = 16
DEC_SEQ = 4096
PAST_LEN = 128

HEAD_DIM = 64
A_PATTERNS = ((128, 1), (512, 4), (2048, 16))
A_HEADS_PER_GROUP = 4
A_HEADS = 12
A_WIDTH = A_HEADS * HEAD_DIM
A_OUT_WIDTH = A_HEADS_PER_GROUP * HEAD_DIM
A_ROT_DIM = HEAD_DIM // 4
ROPE_THETA = 500000.0
B_Q_HEADS = 8
B_KV_HEADS = 2
B_Q_WIDTH = B_Q_HEADS * HEAD_DIM
B_KV_WIDTH = B_KV_HEADS * HEAD_DIM
AXIAL_THETA = 10000.0
GRID_W = 64
Q_BLOCK = 128
N_BRANCHES = 2
GATE_WIDTH = N_BRANCHES * D_MODEL
IN_WIDTH = 3 * A_WIDTH + B_Q_WIDTH + 2 * B_KV_WIDTH + GATE_WIDTH
N_EXPERTS = 16
CAPACITY_FACTOR = 2
D_EXPERT = 1024
PLE_DIM = 256
EPS = 1e-6
NEG_INF = -1e30

kernel_name = "hybrid_dilated_gqa_ec_encoder"


def rms_norm(x, g):
    xf = x.astype(jnp.float32)
    y = xf * lax.rsqrt(jnp.mean(xf * xf, axis=-1, keepdims=True) + EPS)
    return (y * g.astype(jnp.float32)).astype(x.dtype)


def rope(x, pos, theta):
    n = x.shape[-1]
    half = n // 2
    inv = jnp.power(theta, -jnp.arange(0, n, 2, dtype=jnp.float32) / n)
    ang = pos.astype(jnp.float32)[:, None] * inv[None, :]
    cos = jnp.cos(ang)[:, None, :]
    sin = jnp.sin(ang)[:, None, :]
    xf = x.astype(jnp.float32)
    x1, x2 = xf[..., :half], xf[..., half:]
    return jnp.concatenate([x1 * cos - x2 * sin, x2 * cos + x1 * sin], axis=-1).astype(x.dtype)


def partial_rope(x, pos):
    return jnp.concatenate([rope(x[..., :A_ROT_DIM], pos, ROPE_THETA), x[..., A_ROT_DIM:]], axis=-1)


def axial_rope(x, row, col):
    h = HEAD_DIM // 2
    return jnp.concatenate([rope(x[..., :h], row, AXIAL_THETA), rope(x[..., h:], col, AXIAL_THETA)], axis=-1)


def dilated_attention(q, k, v, dilation, radius):
    b, s, h, e = q.shape
    L = s // dilation
    blk = radius
    nb = -(-L // blk)
    lp = nb * blk

    def to_classes(t):
        return t.reshape(b, L, dilation, h, e).transpose(0, 2, 1, 3, 4)

    qc = jnp.pad(to_classes(q), ((0, 0), (0, 0), (0, lp - L), (0, 0), (0, 0))).reshape(b, dilation, nb, blk, h, e)

    def neighbours(t):
        tp = jnp.pad(to_classes(t), ((0, 0), (0, 0), (blk, lp - L + blk), (0, 0), (0, 0)))
        tp = tp.reshape(b, dilation, nb + 2, blk, h, e)
        return jnp.concatenate([tp[:, :, :-2], tp[:, :, 1:-1], tp[:, :, 2:]], axis=3)

    kn, vn = neighbours(k), neighbours(v)
    qpos = jnp.arange(lp).reshape(nb, blk)
    kpos = (jnp.arange(nb)[:, None] - 1) * blk + jnp.arange(3 * blk)[None, :]
    rel = kpos[:, None, :] - qpos[:, :, None]
    valid = (jnp.abs(rel) <= radius) & (kpos[:, None, :] >= 0) & (kpos[:, None, :] < L)
    scores = jnp.einsum('bdnqhe,bdnkhe->bdnhqk', qc, kn, preferred_element_type=jnp.float32) * (e ** -0.5)
    scores = jnp.where(valid[None, None, :, None], scores, NEG_INF)
    m = jnp.max(scores, axis=-1, keepdims=True)
    p = jnp.exp(scores - m)
    den = jnp.sum(p, axis=-1, keepdims=True)
    o = jnp.einsum('bdnhqk,bdnkhe->bdnqhe', (p / den).astype(v.dtype), vn)
    lse = (m + jnp.log(den))[..., 0]
    o = o.reshape(b, dilation, lp, h, e)[:, :, :L].transpose(0, 2, 1, 3, 4).reshape(b, s, h, e)
    lse = lse.transpose(0, 1, 2, 4, 3).reshape(b, dilation, lp, h)[:, :, :L]
    lse = lse.transpose(0, 2, 1, 3).reshape(b, s, h)
    return o, lse


def gqa_block_attention(q, k, v):
    b, s, hq, e = q.shape
    hkv = k.shape[2]
    g = hq // hkv
    nq = s // Q_BLOCK
    qb = q.reshape(b, nq, Q_BLOCK, hkv, g, e).transpose(1, 0, 2, 3, 4, 5)

    def one_block(qi):
        sc = jnp.einsum('bqhge,bkhe->bhgqk', qi, k, preferred_element_type=jnp.float32) * (e ** -0.5)
        pr = jax.nn.softmax(sc, axis=-1)
        return jnp.einsum('bhgqk,bkhe->bqhge', pr.astype(v.dtype), v)

    o = lax.map(one_block, qb)
    return o.transpose(1, 0, 2, 3, 4, 5).reshape(b, s, hq * e)


def expert_choice(h, w_router, w1, w3, w2):
    b, s, dm = h.shape
    n = b * s
    hf = h.reshape(n, dm)
    aff = jax.nn.softmax(jnp.einsum('nd,de->ne', hf, w_router, preferred_element_type=jnp.float32), axis=-1)
    cap = CAPACITY_FACTOR * n // N_EXPERTS
    gate, idx = lax.top_k(aff.T, cap)
    xe = hf[idx]
    hid = jax.nn.silu(jnp.einsum('ecd,edf->ecf', xe, w1)) * jnp.einsum('ecd,edf->ecf', xe, w3)
    ye = jnp.einsum('ecf,efd->ecd', hid, w2) * gate[..., None].astype(h.dtype)
    y = jnp.zeros_like(hf).at[idx.reshape(-1)].add(ye.reshape(-1, dm))
    return y.reshape(b, s, dm)


def encoder_layer(x, p, g_mix, w_in, g_qn, g_kn, w_a_proj, w_b_proj, w_out,
                  g_ffn, w_router, w_exp_gate, w_exp_up, w_exp_down, g_ple, w_ple, w_ple_gate):
    b, s, _ = x.shape
    pos = jnp.arange(s)
    n_rows = s // GRID_W
    row = jnp.repeat(jnp.arange(n_rows), GRID_W)
    col = jnp.tile(jnp.arange(GRID_W), n_rows)

    h = rms_norm(x, g_mix)
    proj = h @ w_in
    c1 = A_WIDTH
    c2 = 2 * A_WIDTH
    c3 = 3 * A_WIDTH
    c4 = c3 + B_Q_WIDTH
    c5 = c4 + B_KV_WIDTH
    c6 = c5 + B_KV_WIDTH
    qa, ka, va, qb, kb, vb, gates = jnp.split(proj, [c1, c2, c3, c4, c5, c6], axis=-1)

    qa = partial_rope(qa.reshape(b, s, A_HEADS, HEAD_DIM), pos)
    ka = partial_rope(ka.reshape(b, s, A_HEADS, HEAD_DIM), pos)
    va = va.reshape(b, s, A_HEADS, HEAD_DIM)
    outs, lses = [], []
    for gi, (window, dil) in enumerate(A_PATTERNS):
        sl = slice(gi * A_HEADS_PER_GROUP, (gi + 1) * A_HEADS_PER_GROUP)
        o, l = dilated_attention(qa[:, :, sl], ka[:, :, sl], va[:, :, sl], dil, window // (2 * dil))
        outs.append(o)
        lses.append(l)
    wts = jax.nn.softmax(jnp.stack(lses), axis=0)
    oa = jnp.sum(wts[..., None] * jnp.stack(outs).astype(jnp.float32), axis=0)
    ya = oa.astype(x.dtype).reshape(b, s, A_OUT_WIDTH) @ w_a_proj

    qb = axial_rope(rms_norm(qb.reshape(b, s, B_Q_HEADS, HEAD_DIM), g_qn), row, col)
    kb = axial_rope(rms_norm(kb.reshape(b, s, B_KV_HEADS, HEAD_DIM), g_kn), row, col)
    vb = vb.reshape(b, s, B_KV_HEADS, HEAD_DIM)
    yb = gqa_block_attention(qb, kb, vb) @ w_b_proj

    gate_a, gate_b = jnp.split(jax.nn.sigmoid(gates), N_BRANCHES, axis=-1)
    x = x + (gate_a * ya + gate_b * yb) @ w_out

    x = x + expert_choice(rms_norm(x, g_ffn), w_router, w_exp_gate, w_exp_up, w_exp_down)

    x = x + (p @ w_ple) * jax.nn.sigmoid(rms_norm(x, g_ple) @ w_ple_gate)
    return x


def setup_inputs(seed: int = 0) -> dict:
    key = jax.random.key(seed)
    ks = jax.random.split(key, 24)
    f32 = jnp.float32

    def nrm(k, shape, scale):
        return jax.random.normal(k, shape, f32) * scale

    def gain(k, shape):
        return 1.0 + 0.01 * jax.random.normal(k, shape, f32)

    return {
        "x_prompt": nrm(ks[0], (BATCH, SEQ, D_MODEL), 1.0),
        "x_sample": nrm(ks[1], (DEC_BATCH, DEC_SEQ, D_MODEL), 1.0),
        "p_prompt": nrm(ks[2], (DEPTH, BATCH, SEQ, PLE_DIM), 1.0),
        "p_sample": nrm(ks[3], (DEPTH, DEC_BATCH, DEC_SEQ, PLE_DIM), 1.0),
        "g_mix": gain(ks[4], (DEPTH, D_MODEL)),
        "w_in": nrm(ks[5], (DEPTH, D_MODEL, IN_WIDTH), D_MODEL ** -0.5),
        "g_qn": gain(ks[6], (DEPTH, HEAD_DIM)),
        "g_kn": gain(ks[7], (DEPTH, HEAD_DIM)),
        "w_a_proj": nrm(ks[8], (DEPTH, A_OUT_WIDTH, D_MODEL), A_OUT_WIDTH ** -0.5),
        "w_b_proj": nrm(ks[9], (DEPTH, B_Q_WIDTH, D_MODEL), B_Q_WIDTH ** -0.5),
        "w_out": nrm(ks[10], (DEPTH, D_MODEL, D_MODEL), D_MODEL ** -0.5),
        "g_ffn": gain(ks[11], (DEPTH, D_MODEL)),
        "w_router": nrm(ks[12], (DEPTH, D_MODEL, N_EXPERTS), D_MODEL ** -0.5),
        "w_exp_gate": nrm(ks[13], (DEPTH, N_EXPERTS, D_MODEL, D_EXPERT), D_MODEL ** -0.5),
        "w_exp_up": nrm(ks[14], (DEPTH, N_EXPERTS, D_MODEL, D_EXPERT), D_MODEL ** -0.5),
        "w_exp_down": nrm(ks[15], (DEPTH, N_EXPERTS, D_EXPERT, D_MODEL), D_EXPERT ** -0.5),
        "g_ple": gain(ks[16], (DEPTH, D_MODEL)),
        "w_ple": nrm(ks[17], (DEPTH, PLE_DIM, D_MODEL), PLE_DIM ** -0.5),
        "w_ple_gate": nrm(ks[18], (DEPTH, D_MODEL, D_MODEL), D_MODEL ** -0.5),
        "g_final": gain(ks[19], (D_MODEL,)),
    }


def reference(x_prompt, x_sample, p_prompt, p_sample, g_mix, w_in, g_qn, g_kn, w_a_proj, w_b_proj,
              w_out, g_ffn, w_router, w_exp_gate, w_exp_up, w_exp_down, g_ple, w_ple, w_ple_gate, g_final):
    def run(x, p):
        for i in range(DEPTH):
            x = encoder_layer(x, p[i], g_mix[i], w_in[i], g_qn[i], g_kn[i], w_a_proj[i], w_b_proj[i],
                              w_out[i], g_ffn[i], w_router[i], w_exp_gate[i], w_exp_up[i],
                              w_exp_down[i], g_ple[i], w_ple[i], w_ple_gate[i])
        return rms_norm(x, g_final)

    y_prompt = run(x_prompt, p_prompt)
    y_sample = run(x_sample, p_sample)
    return (y_prompt, y_sample)
```

```python
import functools
import math

import jax
import jax.numpy as jnp
from jax import lax
from jax.experimental import pallas as pl
from jax.experimental.pallas import tpu as pltpu

F32 = jnp.float32
BF16 = jnp.bfloat16

D_MODEL = 1024
HEAD_DIM = 64
A_PATTERNS = ((128, 1), (512, 4), (2048, 16))
A_HEADS_PER_GROUP = 4
A_WIDTH = 768
A_GROUP_WIDTH = A_HEADS_PER_GROUP * HEAD_DIM
A_ROT_DIM = 16
A_RADIUS = 64
ROPE_THETA = 500000.0
B_Q_WIDTH = 512
B_KV_HEADS = 2
B_KV_WIDTH = 128
B_GROUP_WIDTH = B_Q_WIDTH // B_KV_HEADS
AXIAL_THETA = 10000.0
GRID_W = 64
GATE_WIDTH = 2048
N_EXPERTS = 16
CAPACITY_FACTOR = 2
PLE_DIM = 256
EPS = 1e-6
NEG_INF = -1e30
LN2 = math.log(2.0)
Q_SCALE = (HEAD_DIM ** -0.5) / LN2

LANES = 128
VMEM_LIMIT = 48 * 1024 * 1024

ROW_TILE = 512
A_TQ = 256
B_TQ = 512
B_TK = 512
FFN_TC = 512


def _cparams(sem):
    return pltpu.CompilerParams(dimension_semantics=sem, vmem_limit_bytes=VMEM_LIMIT)


def _rope_tables(s):
    pos = jnp.arange(s)
    posf = pos.astype(F32)
    inv_a = jnp.power(ROPE_THETA, -jnp.arange(0, A_ROT_DIM, 2, dtype=F32) / A_ROT_DIM)
    ang = posf[:, None] * inv_a[None, :]
    ca, sa = jnp.cos(ang), jnp.sin(ang)
    z8 = jnp.zeros_like(sa)
    rest = HEAD_DIM - A_ROT_DIM
    cos_a = jnp.concatenate([ca, ca, jnp.ones((s, rest), F32)], axis=1)
    sp_a = jnp.concatenate([z8, sa, jnp.zeros((s, rest), F32)], axis=1)
    sm_a = jnp.concatenate([-sa, z8, jnp.zeros((s, rest), F32)], axis=1)

    hb = HEAD_DIM // 2
    inv_b = jnp.power(AXIAL_THETA, -jnp.arange(0, hb, 2, dtype=F32) / hb)
    row = (pos // GRID_W).astype(F32)
    col = (pos % GRID_W).astype(F32)
    ar = row[:, None] * inv_b[None, :]
    ac = col[:, None] * inv_b[None, :]
    cr, sr, cc, sc = jnp.cos(ar), jnp.sin(ar), jnp.cos(ac), jnp.sin(ac)
    z16 = jnp.zeros_like(sr)
    cos_b = jnp.concatenate([cr, cr, cc, cc], axis=1)
    sp_b = jnp.concatenate([z16, sr, z16, sc], axis=1)
    sm_b = jnp.concatenate([-sr, z16, -sc, z16], axis=1)
    heads_per_tile = LANES // HEAD_DIM
    parts = [jnp.tile(t, (1, heads_per_tile)) for t in (cos_a, sp_a, sm_a, cos_b, sp_b, sm_b)]
    return jnp.concatenate(parts, axis=1)


def _rope_chunk(x, tab_ref, base, half):
    cos = tab_ref[:, base:base + LANES]
    sp = tab_ref[:, base + LANES:base + 2 * LANES]
    sm = tab_ref[:, base + 2 * LANES:base + 3 * LANES]
    return x * cos + pltpu.roll(x, half, 1) * sp + pltpu.roll(x, LANES - half, 1) * sm


def _head_mean_sq(acc, blk_ref):
    sq = acc * acc
    hi = sq.astype(BF16)
    lo = (sq - hi.astype(F32)).astype(BF16)
    blk = blk_ref[...]
    return (jnp.dot(hi, blk, preferred_element_type=F32)
            + jnp.dot(lo, blk, preferred_element_type=F32))


def _in_proj_kernel(x_ref, tab_ref, gmix_ref, wqa_ref, wka_ref, wva_ref, wqb_ref, wkb_ref, wvb_ref,
                    wg_ref, gq_ref, gk_ref, blkq_ref, blkk_ref,
                    qa_ref, ka_ref, va_ref, qb_ref, kb_ref, vb_ref, gates_ref):
    x = x_ref[...]
    ms = jnp.mean(x * x, axis=-1, keepdims=True)
    h = (x * lax.rsqrt(ms + EPS) * gmix_ref[...]).astype(BF16)

    acc = jnp.dot(h, wqa_ref[...], preferred_element_type=F32)
    for c in range(A_WIDTH // LANES):
        sl = slice(c * LANES, (c + 1) * LANES)
        qa_ref[:, sl] = (_rope_chunk(acc[:, sl], tab_ref, 0, A_ROT_DIM // 2) * Q_SCALE).astype(BF16)

    acc = jnp.dot(h, wka_ref[...], preferred_element_type=F32)
    for c in range(A_WIDTH // LANES):
        sl = slice(c * LANES, (c + 1) * LANES)
        ka_ref[:, sl] = _rope_chunk(acc[:, sl], tab_ref, 0, A_ROT_DIM // 2).astype(BF16)

    va_ref[...] = jnp.dot(h, wva_ref[...], preferred_element_type=F32).astype(BF16)

    acc = jnp.dot(h, wqb_ref[...], preferred_element_type=F32)
    acc = acc * lax.rsqrt(_head_mean_sq(acc, blkq_ref) + EPS) * gq_ref[...]
    for c in range(B_Q_WIDTH // LANES):
        sl = slice(c * LANES, (c + 1) * LANES)
        qb_ref[:, sl] = (_rope_chunk(acc[:, sl], tab_ref, 3 * LANES, HEAD_DIM // 4) * Q_SCALE).astype(BF16)

    acc = jnp.dot(h, wkb_ref[...], preferred_element_type=F32)
    acc = acc * lax.rsqrt(_head_mean_sq(acc, blkk_ref) + EPS) * gk_ref[...]
    kb_ref[...] = _rope_chunk(acc, tab_ref, 3 * LANES, HEAD_DIM // 4).astype(BF16)

    vb_ref[...] = jnp.dot(h, wvb_ref[...], preferred_element_type=F32).astype(BF16)

    gates_ref[...] = jax.nn.sigmoid(jnp.dot(h, wg_ref[...], preferred_element_type=F32)).astype(BF16)


def _in_proj(xf, tab, s, gmix, w_in, gq, gk):
    n = xf.shape[0]
    tm = ROW_TILE
    nts = s // tm
    c1, c2, c3 = A_WIDTH, 2 * A_WIDTH, 3 * A_WIDTH
    c4 = c3 + B_Q_WIDTH
    c5 = c4 + B_KV_WIDTH
    c6 = c5 + B_KV_WIDTH
    wb = w_in.astype(BF16)
    ws = [wb[:, :c1], wb[:, c1:c2], wb[:, c2:c3], wb[:, c3:c4], wb[:, c4:c5], wb[:, c5:c6], wb[:, c6:]]

    def head_blockdiag(width):
        hid = jnp.arange(width) // HEAD_DIM
        return jnp.where(hid[:, None] == hid[None, :], 1.0 / HEAD_DIM, 0.0).astype(BF16)

    consts = [gmix.reshape(1, D_MODEL)] + ws + [
        jnp.tile(gq, B_Q_WIDTH // HEAD_DIM).reshape(1, B_Q_WIDTH),
        jnp.tile(gk, B_KV_WIDTH // HEAD_DIM).reshape(1, B_KV_WIDTH),
        head_blockdiag(B_Q_WIDTH), head_blockdiag(B_KV_WIDTH)]
    widths = [A_WIDTH, A_WIDTH, A_WIDTH, B_Q_WIDTH, B_KV_WIDTH, B_KV_WIDTH, GATE_WIDTH]

    def full(a):
        return pl.BlockSpec(a.shape, lambda i: (0,) * a.ndim)

    return pl.pallas_call(
        _in_proj_kernel,
        out_shape=[jax.ShapeDtypeStruct((n, w), BF16) for w in widths],
        grid=(n // tm,),
        in_specs=[pl.BlockSpec((tm, D_MODEL), lambda i: (i, 0)),
                  pl.BlockSpec((tm, tab.shape[1]), lambda i: (i % nts, 0))] + [full(a) for a in consts],
        out_specs=[pl.BlockSpec((tm, w), lambda i: (i, 0)) for w in widths],
        compiler_params=_cparams(("parallel",)),
        name="in_proj",
    )(xf, tab, *consts)


def _attn_a_kernel(q_ref, kp_ref, kc_ref, kn_ref, vp_ref, vc_ref, vn_ref, o_ref, lse_ref, *, class_len):
    tq = q_ref.shape[0]
    tk = tq + 2 * A_RADIUS
    i = pl.program_id(2)
    q = q_ref[...]
    k = jnp.concatenate([kp_ref[...], kc_ref[...], kn_ref[...]], axis=0)
    v = jnp.concatenate([vp_ref[...], vc_ref[...], vn_ref[...]], axis=0)
    qpos = i * tq + lax.broadcasted_iota(jnp.int32, (tq, tk), 0)
    kpos = i * tq - A_RADIUS + lax.broadcasted_iota(jnp.int32, (tq, tk), 1)
    valid = (jnp.abs(kpos - qpos) <= A_RADIUS) & (kpos >= 0) & (kpos < class_len)
    for h in range(A_HEADS_PER_GROUP):
        sl = slice(h * HEAD_DIM, (h + 1) * HEAD_DIM)
        sc = lax.dot_general(q[:, sl], k[:, sl], (((1,), (1,)), ((), ())), preferred_element_type=F32)
        sc = jnp.where(valid, sc, NEG_INF)
        m = jnp.max(sc, axis=1, keepdims=True)
        p = jnp.exp2(sc - m)
        den = jnp.sum(p, axis=1, keepdims=True)
        o = jnp.dot(p.astype(BF16), v[:, sl], preferred_element_type=F32) / den
        o_ref[:, sl] = o
        lse_ref[:, sl] = jnp.broadcast_to(m * LN2 + jnp.log(den), (tq, HEAD_DIM))


def _attn_a(qa, ka, va, gi, dil, b, s):
    cl = s // dil
    tq = min(A_TQ, cl)
    hb = tq // A_RADIUS
    n_halo = cl // A_RADIUS
    groups = A_WIDTH // A_GROUP_WIDTH
    view = (b, cl, dil * A_WIDTH)
    q3, k3, v3 = qa.reshape(view), ka.reshape(view), va.reshape(view)

    def col(r):
        return r * groups + gi

    main = pl.BlockSpec((None, tq, A_GROUP_WIDTH), lambda bi, r, i: (bi, i, col(r)))
    prev = pl.BlockSpec((None, A_RADIUS, A_GROUP_WIDTH),
                        lambda bi, r, i: (bi, jnp.maximum(i * hb - 1, 0), col(r)))
    nxt = pl.BlockSpec((None, A_RADIUS, A_GROUP_WIDTH),
                       lambda bi, r, i: (bi, jnp.minimum((i + 1) * hb, n_halo - 1), col(r)))
    out = pl.BlockSpec((None, tq, A_GROUP_WIDTH), lambda bi, r, i: (bi, i, r))
    o, lse = pl.pallas_call(
        functools.partial(_attn_a_kernel, class_len=cl),
        out_shape=[jax.ShapeDtypeStruct((b, cl, dil * A_GROUP_WIDTH), F32)] * 2,
        grid=(b, dil, cl // tq),
        in_specs=[main, prev, main, nxt, prev, main, nxt],
        out_specs=[out, out],
        compiler_params=_cparams(("parallel", "parallel", "parallel")),
        name=f"attn_a{gi}",
    )(q3, k3, k3, k3, v3, v3, v3)
    return o.reshape(b * s, A_GROUP_WIDTH), lse.reshape(b * s, A_GROUP_WIDTH)


def _flash_b_kernel(qt_ref, k_ref, vt_ref, o_ref, m_sc, l_sc, acc_sc):
    kv = pl.program_id(3)
    heads = B_GROUP_WIDTH // HEAD_DIM

    @pl.when(kv == 0)
    def _():
        m_sc[...] = jnp.full(m_sc.shape, NEG_INF, F32)
        l_sc[...] = jnp.zeros(l_sc.shape, F32)
        acc_sc[...] = jnp.zeros(acc_sc.shape, F32)

    k = k_ref[...]
    vt = vt_ref[...]
    for h in range(heads):
        rows = slice(h * HEAD_DIM, (h + 1) * HEAD_DIM)
        sc = jnp.dot(k, qt_ref[rows, :], preferred_element_type=F32)
        m_old = m_sc[h:h + 1, :]
        m_new = jnp.maximum(m_old, jnp.max(sc, axis=0, keepdims=True))
        alpha = jnp.exp2(m_old - m_new)
        p = jnp.exp2(sc - m_new)
        l_sc[h:h + 1, :] = alpha * l_sc[h:h + 1, :] + jnp.sum(p, axis=0, keepdims=True)
        acc_sc[rows, :] = alpha * acc_sc[rows, :] + jnp.dot(vt, p.astype(BF16), preferred_element_type=F32)
        m_sc[h:h + 1, :] = m_new

    @pl.when(kv == pl.num_programs(3) - 1)
    def _():
        for h in range(heads):
            rows = slice(h * HEAD_DIM, (h + 1) * HEAD_DIM)
            acc_sc[rows, :] = acc_sc[rows, :] / l_sc[h:h + 1, :]
        o_ref[...] = acc_sc[...].T.astype(o_ref.dtype)


def _flash_b(qt, k4, vt, b, s):
    tq, tk = min(B_TQ, s), min(B_TK, s)
    return pl.pallas_call(
        _flash_b_kernel,
        out_shape=jax.ShapeDtypeStruct((b, B_KV_HEADS, s, B_GROUP_WIDTH), BF16),
        grid=(b, B_KV_HEADS, s // tq, s // tk),
        in_specs=[pl.BlockSpec((None, None, B_GROUP_WIDTH, tq), lambda bi, g, i, j: (bi, g, 0, i)),
                  pl.BlockSpec((None, None, tk, HEAD_DIM), lambda bi, g, i, j: (bi, g, j, 0)),
                  pl.BlockSpec((None, None, HEAD_DIM, tk), lambda bi, g, i, j: (bi, g, 0, j))],
        out_specs=pl.BlockSpec((None, None, tq, B_GROUP_WIDTH), lambda bi, g, i, j: (bi, g, i, 0)),
        scratch_shapes=[pltpu.VMEM((8, tq), F32), pltpu.VMEM((8, tq), F32),
                        pltpu.VMEM((B_GROUP_WIDTH, tq), F32)],
        compiler_params=_cparams(("parallel", "parallel", "parallel", "arbitrary")),
        name="flash_b",
    )(qt, k4, vt)


def _post_attn_kernel(x_ref, o0_ref, o1_ref, o2_ref, l0_ref, l1_ref, l2_ref, ob0_ref, ob1_ref, gates_ref,
                      wa_ref, wb0_ref, wb1_ref, wout_ref, gffn_ref, wrh_ref, wrl_ref,
                      x1_ref, h2_ref, aff_ref):
    l0, l1, l2 = l0_ref[...], l1_ref[...], l2_ref[...]
    m = jnp.maximum(jnp.maximum(l0, l1), l2)
    e0, e1, e2 = jnp.exp(l0 - m), jnp.exp(l1 - m), jnp.exp(l2 - m)
    oa = (e0 * o0_ref[...] + e1 * o1_ref[...] + e2 * o2_ref[...]) / (e0 + e1 + e2)
    ya = jnp.dot(oa.astype(BF16), wa_ref[...], preferred_element_type=F32)
    yb = (jnp.dot(ob0_ref[...], wb0_ref[...], preferred_element_type=F32)
          + jnp.dot(ob1_ref[...], wb1_ref[...], preferred_element_type=F32))
    ga = gates_ref[:, :D_MODEL].astype(F32)
    gb = gates_ref[:, D_MODEL:].astype(F32)
    z = (ga * ya + gb * yb).astype(BF16)
    x1 = x_ref[...] + jnp.dot(z, wout_ref[...], preferred_element_type=F32)
    x1_ref[...] = x1
    ms = jnp.mean(x1 * x1, axis=-1, keepdims=True)
    h2 = x1 * lax.rsqrt(ms + EPS) * gffn_ref[...]
    hi = h2.astype(BF16)
    lo = (h2 - hi.astype(F32)).astype(BF16)
    h2_ref[...] = hi
    nt = (((1,), (1,)), ((), ()))
    wrh = wrh_ref[...]
    logits = (lax.dot_general(wrh, hi, nt, preferred_element_type=F32)
              + lax.dot_general(wrh, lo, nt, preferred_element_type=F32)
              + lax.dot_general(wrl_ref[...], hi, nt, preferred_element_type=F32))
    logits = logits - jnp.max(logits, axis=0, keepdims=True)
    e = jnp.exp(logits)
    aff_ref[...] = e / jnp.sum(e, axis=0, keepdims=True)


def _post_attn(xf, oas, lses, ob, gates, b, s, w_a, w_b, w_out, g_ffn, w_router):
    n = xf.shape[0]
    tm = ROW_TILE
    nts = s // tm
    wa = w_a.astype(BF16)
    wb = w_b.astype(BF16)
    wb0, wb1 = wb[:B_GROUP_WIDTH], wb[B_GROUP_WIDTH:]
    wout = w_out.astype(BF16)
    wrt = w_router.T
    wrh = wrt.astype(BF16)
    wrl = (wrt - wrh.astype(F32)).astype(BF16)
    consts = [wa, wb0, wb1, wout, g_ffn.reshape(1, D_MODEL), wrh, wrl]

    def full(a):
        return pl.BlockSpec(a.shape, lambda i: (0,) * a.ndim)

    def rows(w):
        return pl.BlockSpec((tm, w), lambda i: (i, 0))

    def ob_spec(g):
        return pl.BlockSpec((None, None, tm, B_GROUP_WIDTH), lambda i: (i // nts, g, i % nts, 0))

    return pl.pallas_call(
        _post_attn_kernel,
        out_shape=[jax.ShapeDtypeStruct((n, D_MODEL), F32),
                   jax.ShapeDtypeStruct((n, D_MODEL), BF16),
                   jax.ShapeDtypeStruct((N_EXPERTS, n), F32)],
        grid=(n // tm,),
        in_specs=[rows(D_MODEL)] + [rows(A_GROUP_WIDTH)] * 6 + [ob_spec(0), ob_spec(1), rows(GATE_WIDTH)]
                 + [full(a) for a in consts],
        out_specs=[rows(D_MODEL), rows(D_MODEL), pl.BlockSpec((N_EXPERTS, tm), lambda i: (0, i))],
        compiler_params=_cparams(("parallel",)),
        name="post_attn",
    )(xf, *oas, *lses, ob, ob, gates, *consts)


def _moe_ffn_kernel(xe_ref, gate_ref, w1_ref, w3_ref, w2_ref, ye_ref):
    xe = xe_ref[...]
    a = jnp.dot(xe, w1_ref[...], preferred_element_type=F32)
    u = jnp.dot(xe, w3_ref[...], preferred_element_type=F32)
    hid = (a * jax.nn.sigmoid(a) * u).astype(BF16)
    ye = jnp.dot(hid, w2_ref[...], preferred_element_type=F32) * gate_ref[...]
    ye_ref[...] = ye.astype(ye_ref.dtype)


def _moe_ffn(xe, gate, w1, w3, w2):
    e, cap, _ = xe.shape
    tc = min(FFN_TC, cap)
    wspec = pl.BlockSpec((None, D_MODEL, D_MODEL), lambda ei, ci: (ei, 0, 0))
    return pl.pallas_call(
        _moe_ffn_kernel,
        out_shape=jax.ShapeDtypeStruct((e, cap, D_MODEL), BF16),
        grid=(e, cap // tc),
        in_specs=[pl.BlockSpec((None, tc, D_MODEL), lambda ei, ci: (ei, ci, 0)),
                  pl.BlockSpec((None, tc, 1), lambda ei, ci: (ei, ci, 0)),
                  wspec, wspec, wspec],
        out_specs=pl.BlockSpec((None, tc, D_MODEL), lambda ei, ci: (ei, ci, 0)),
        compiler_params=_cparams(("parallel", "arbitrary")),
        name="moe_ffn",
    )(xe, gate.reshape(e, cap, 1), w1, w3, w2)


def _final_kernel(x1_ref, y_ref, p_ref, wple_ref, wpg_ref, gple_ref, gfin_ref, out_ref):
    x2 = x1_ref[...] + y_ref[...]
    ms = jnp.mean(x2 * x2, axis=-1, keepdims=True)
    hp = (x2 * lax.rsqrt(ms + EPS) * gple_ref[...]).astype(BF16)
    gt = jax.nn.sigmoid(jnp.dot(hp, wpg_ref[...], preferred_element_type=F32))
    emb = jnp.dot(p_ref[...].astype(BF16), wple_ref[...], preferred_element_type=F32)
    x3 = x2 + emb * gt
    ms3 = jnp.mean(x3 * x3, axis=-1, keepdims=True)
    out_ref[...] = x3 * lax.rsqrt(ms3 + EPS) * gfin_ref[...]


def _final(x1, y, pf, w_ple, w_ple_gate, g_ple, g_final):
    n = x1.shape[0]
    tm = ROW_TILE
    consts = [w_ple.astype(BF16), w_ple_gate.astype(BF16), g_ple.reshape(1, D_MODEL), g_final.reshape(1, D_MODEL)]

    def full(a):
        return pl.BlockSpec(a.shape, lambda i: (0,) * a.ndim)

    def rows(w):
        return pl.BlockSpec((tm, w), lambda i: (i, 0))

    return pl.pallas_call(
        _final_kernel,
        out_shape=jax.ShapeDtypeStruct((n, D_MODEL), F32),
        grid=(n // tm,),
        in_specs=[rows(D_MODEL), rows(D_MODEL), rows(PLE_DIM)] + [full(a) for a in consts],
        out_specs=rows(D_MODEL),
        compiler_params=_cparams(("parallel",)),
        name="final",
    )(x1, y, pf, *consts)


def _run_group(x, p, g_mix, w_in, g_qn, g_kn, w_a_proj, w_b_proj, w_out, g_ffn, w_router,
               w1, w3, w2, g_ple, w_ple, w_ple_gate, g_final):
    b, s, _ = x.shape
    n = b * s
    xf = x.reshape(n, D_MODEL)
    tab = _rope_tables(s)
    qa, ka, va, qb, kb, vb, gates = _in_proj(xf, tab, s, g_mix, w_in, g_qn, g_kn)

    oas, lses = [], []
    for gi, (window, dil) in enumerate(A_PATTERNS):
        assert window // (2 * dil) == A_RADIUS
        o, lse = _attn_a(qa, ka, va, gi, dil, b, s)
        oas.append(o)
        lses.append(lse)

    qt = qb.reshape(b, s, B_KV_HEADS, B_GROUP_WIDTH).transpose(0, 2, 3, 1)
    k4 = kb.reshape(b, s, B_KV_HEADS, HEAD_DIM).transpose(0, 2, 1, 3)
    vt = vb.reshape(b, s, B_KV_HEADS, HEAD_DIM).transpose(0, 2, 3, 1)
    ob = _flash_b(qt, k4, vt, b, s)

    x1, h2, aff_t = _post_attn(xf, oas, lses, ob, gates, b, s, w_a_proj, w_b_proj, w_out, g_ffn, w_router)

    cap = CAPACITY_FACTOR * n // N_EXPERTS
    gate, idx = lax.top_k(aff_t, cap)
    xe = h2[idx]
    ye = _moe_ffn(xe, gate, w1, w3, w2)
    y = jnp.zeros((n, D_MODEL), F32).at[idx.reshape(-1)].add(ye.reshape(-1, D_MODEL).astype(F32))

    out = _final(x1, y, p.reshape(n, PLE_DIM), w_ple, w_ple_gate, g_ple, g_final)
    return out.reshape(b, s, D_MODEL)


def kernel(x_prompt, x_sample, p_prompt, p_sample, g_mix, w_in, g_qn, g_kn, w_a_proj, w_b_proj, w_out, g_ffn,
           w_router, w_exp_gate, w_exp_up, w_exp_down, g_ple, w_ple, w_ple_gate, g_final):
    assert g_mix.shape[0] == 1, "single layer"
    w1 = w_exp_gate[0].astype(BF16)
    w3 = w_exp_up[0].astype(BF16)
    w2 = w_exp_down[0].astype(BF16)
    args = (g_mix[0], w_in[0], g_qn[0], g_kn[0], w_a_proj[0], w_b_proj[0], w_out[0], g_ffn[0], w_router[0],
            w1, w3, w2, g_ple[0], w_ple[0], w_ple_gate[0], g_final)
    y_prompt = _run_group(x_prompt, p_prompt[0], *args)
    y_sample = _run_group(x_sample, p_sample[0], *args)
    return (y_prompt, y_sample)
```

```python
import functools
import math

import jax
import jax.numpy as jnp
from jax import lax
from jax.experimental import pallas as pl
from jax.experimental.pallas import tpu as pltpu

F32 = jnp.float32
BF16 = jnp.bfloat16

D_MODEL = 1024
HEAD_DIM = 64
A_PATTERNS = ((128, 1), (512, 4), (2048, 16))
A_DILATIONS = tuple(d for _, d in A_PATTERNS)
A_GROUPS = len(A_PATTERNS)
A_HEADS_PER_GROUP = 4
A_WIDTH = 768
A_GROUP_WIDTH = A_HEADS_PER_GROUP * HEAD_DIM
A_ROT_DIM = 16
A_RADIUS = 64
ROPE_THETA = 500000.0
B_Q_WIDTH = 512
B_KV_HEADS = 2
B_KV_WIDTH = 128
B_GROUP_HEADS = 4
B_GROUP_WIDTH = B_GROUP_HEADS * HEAD_DIM
AXIAL_THETA = 10000.0
GRID_W = 64
GATE_WIDTH = 2048
N_EXPERTS = 16
CAPACITY_FACTOR = 2
PLE_DIM = 256
EPS = 1e-6
NEG_INF = -1e30
LN2 = math.log(2.0)
Q_SCALE = (HEAD_DIM ** -0.5) / LN2
B_FIXED_SHIFT_MAX = 50.0

LANES = 128
SUBLANES = 8
VMEM_LIMIT = 48 * 1024 * 1024

ROW_TILE = 512
A_TQ = 256
B_TQ = ROW_TILE
B_TK = 2 * ROW_TILE
FFN_TC = 512


def _cparams(sem):
    return pltpu.CompilerParams(dimension_semantics=sem, vmem_limit_bytes=VMEM_LIMIT)


def _full_spec(a):
    return pl.BlockSpec(a.shape, lambda *_: (0,) * a.ndim)


def _rope_tables(s):
    pos = jnp.arange(s)
    posf = pos.astype(F32)
    inv_a = jnp.power(ROPE_THETA, -jnp.arange(0, A_ROT_DIM, 2, dtype=F32) / A_ROT_DIM)
    ang = posf[:, None] * inv_a[None, :]
    ca, sa = jnp.cos(ang), jnp.sin(ang)
    z8 = jnp.zeros_like(sa)
    rest = HEAD_DIM - A_ROT_DIM
    cos_a = jnp.concatenate([ca, ca, jnp.ones((s, rest), F32)], axis=1)
    sp_a = jnp.concatenate([z8, sa, jnp.zeros((s, rest), F32)], axis=1)
    sm_a = jnp.concatenate([-sa, z8, jnp.zeros((s, rest), F32)], axis=1)

    hb = HEAD_DIM // 2
    inv_b = jnp.power(AXIAL_THETA, -jnp.arange(0, hb, 2, dtype=F32) / hb)
    row = (pos // GRID_W).astype(F32)
    col = (pos % GRID_W).astype(F32)
    ar = row[:, None] * inv_b[None, :]
    ac = col[:, None] * inv_b[None, :]
    cr, sr, cc, sc = jnp.cos(ar), jnp.sin(ar), jnp.cos(ac), jnp.sin(ac)
    z16 = jnp.zeros_like(sr)
    cos_b = jnp.concatenate([cr, cr, cc, cc], axis=1)
    sp_b = jnp.concatenate([z16, sr, z16, sc], axis=1)
    sm_b = jnp.concatenate([-sr, z16, -sc, z16], axis=1)
    heads_per_tile = LANES // HEAD_DIM
    parts = [jnp.tile(t, (1, heads_per_tile)) for t in (cos_a, sp_a, sm_a, cos_b, sp_b, sm_b)]
    return jnp.concatenate(parts, axis=1)


def _rope_chunk(x, tab_ref, base, half):
    cos = tab_ref[:, base:base + LANES]
    sp = tab_ref[:, base + LANES:base + 2 * LANES]
    sm = tab_ref[:, base + 2 * LANES:base + 3 * LANES]
    return x * cos + pltpu.roll(x, half, 1) * sp + pltpu.roll(x, LANES - half, 1) * sm


def _head_mean_sq(acc, blk_ref):
    sq = acc * acc
    hi = sq.astype(BF16)
    lo = (sq - hi.astype(F32)).astype(BF16)
    blk = blk_ref[...]
    return (jnp.dot(hi, blk, preferred_element_type=F32)
            + jnp.dot(lo, blk, preferred_element_type=F32))


def _store_by_class(out_refs, chunk_idx, chunk, stage_ref):
    g, half = divmod(chunk_idx, A_GROUP_WIDTH // LANES)
    dil = A_DILATIONS[g]
    lanes = slice(half * LANES, (half + 1) * LANES)
    if dil == 1:
        out_refs[g][0, :, lanes] = chunk.astype(BF16)
        return
    tm = chunk.shape[0]
    stage_ref[...] = chunk
    for r in range(dil):
        out_refs[g][r, :, lanes] = stage_ref[pl.ds(r, tm // dil, stride=dil), :].astype(BF16)


def _in_proj_kernel(x_ref, tab_ref, gmix_ref, wqa_ref, wka_ref, wva_ref, wqb_ref, wkb_ref, wvb_ref,
                    wg_ref, gq_ref, gk_ref, blkq_ref, blkk_ref,
                    qa0_ref, qa1_ref, qa2_ref, ka0_ref, ka1_ref, ka2_ref, va0_ref, va1_ref, va2_ref,
                    qt_ref, kb_ref, vt_ref, gates_ref, stage_ref):
    x = x_ref[...]
    tm = x.shape[0]
    ms = jnp.mean(x * x, axis=-1, keepdims=True)
    h = (x * lax.rsqrt(ms + EPS) * gmix_ref[...]).astype(BF16)

    acc = jnp.dot(h, wqa_ref[...], preferred_element_type=F32)
    for c in range(A_WIDTH // LANES):
        roped = _rope_chunk(acc[:, c * LANES:(c + 1) * LANES], tab_ref, 0, A_ROT_DIM // 2) * Q_SCALE
        _store_by_class((qa0_ref, qa1_ref, qa2_ref), c, roped, stage_ref)

    acc = jnp.dot(h, wka_ref[...], preferred_element_type=F32)
    for c in range(A_WIDTH // LANES):
        roped = _rope_chunk(acc[:, c * LANES:(c + 1) * LANES], tab_ref, 0, A_ROT_DIM // 2)
        _store_by_class((ka0_ref, ka1_ref, ka2_ref), c, roped, stage_ref)

    acc = jnp.dot(h, wva_ref[...], preferred_element_type=F32)
    for c in range(A_WIDTH // LANES):
        _store_by_class((va0_ref, va1_ref, va2_ref), c, acc[:, c * LANES:(c + 1) * LANES], stage_ref)

    acc = jnp.dot(h, wqb_ref[...], preferred_element_type=F32)
    acc = acc * lax.rsqrt(_head_mean_sq(acc, blkq_ref) + EPS) * gq_ref[...]
    heads_per_chunk = LANES // HEAD_DIM
    for c in range(B_Q_WIDTH // LANES):
        roped = _rope_chunk(acc[:, c * LANES:(c + 1) * LANES], tab_ref, 3 * LANES, HEAD_DIM // 4) * Q_SCALE
        rt = roped.T.astype(BF16)
        for hh in range(heads_per_chunk):
            head = c * heads_per_chunk + hh
            g, hg = divmod(head, B_GROUP_HEADS)
            cols = slice(hg * tm, (hg + 1) * tm)
            qt_ref[g, g * HEAD_DIM:(g + 1) * HEAD_DIM, cols] = rt[hh * HEAD_DIM:(hh + 1) * HEAD_DIM, :]
            qt_ref[g, (1 - g) * HEAD_DIM:(2 - g) * HEAD_DIM, cols] = jnp.zeros((HEAD_DIM, tm), BF16)

    acc = jnp.dot(h, wkb_ref[...], preferred_element_type=F32)
    acc = acc * lax.rsqrt(_head_mean_sq(acc, blkk_ref) + EPS) * gk_ref[...]
    kb_ref[...] = _rope_chunk(acc, tab_ref, 3 * LANES, HEAD_DIM // 4).astype(BF16)

    vt = jnp.dot(h, wvb_ref[...], preferred_element_type=F32).T.astype(BF16)
    for g in range(B_KV_HEADS):
        vt_ref[g] = vt[g * HEAD_DIM:(g + 1) * HEAD_DIM, :]

    gates_ref[...] = jax.nn.sigmoid(jnp.dot(h, wg_ref[...], preferred_element_type=F32)).astype(BF16)


def _in_proj(xf, tab, b, s, gmix, w_in, gq, gk):
    n = xf.shape[0]
    tm = ROW_TILE
    nts = s // tm
    c1, c2, c3 = A_WIDTH, 2 * A_WIDTH, 3 * A_WIDTH
    c4 = c3 + B_Q_WIDTH
    c5 = c4 + B_KV_WIDTH
    c6 = c5 + B_KV_WIDTH
    wb = w_in.astype(BF16)
    ws = [wb[:, :c1], wb[:, c1:c2], wb[:, c2:c3], wb[:, c3:c4], wb[:, c4:c5], wb[:, c5:c6], wb[:, c6:]]

    def head_blockdiag(width):
        hid = jnp.arange(width) // HEAD_DIM
        return jnp.where(hid[:, None] == hid[None, :], 1.0 / HEAD_DIM, 0.0).astype(BF16)

    consts = [gmix.reshape(1, D_MODEL)] + ws + [
        jnp.tile(gq, B_Q_WIDTH // HEAD_DIM).reshape(1, B_Q_WIDTH),
        jnp.tile(gk, B_KV_WIDTH // HEAD_DIM).reshape(1, B_KV_WIDTH),
        head_blockdiag(B_Q_WIDTH), head_blockdiag(B_KV_WIDTH)]

    a_shapes, a_specs = [], []
    for _ in range(3):
        for dil in A_DILATIONS:
            a_shapes.append(jax.ShapeDtypeStruct((b, dil, s // dil, A_GROUP_WIDTH), BF16))
            a_specs.append(pl.BlockSpec((None, dil, tm // dil, A_GROUP_WIDTH),
                                        lambda i: (i // nts, 0, i % nts, 0)))
    kpb = B_TK // tm
    b_shapes = [jax.ShapeDtypeStruct((b, B_KV_HEADS, nts, B_KV_WIDTH, B_GROUP_HEADS * tm), BF16),
                jax.ShapeDtypeStruct((n, B_KV_WIDTH), BF16),
                jax.ShapeDtypeStruct((b, B_KV_HEADS, s // B_TK, HEAD_DIM, B_TK), BF16),
                jax.ShapeDtypeStruct((n, GATE_WIDTH), BF16)]
    b_specs = [pl.BlockSpec((None, B_KV_HEADS, None, B_KV_WIDTH, B_GROUP_HEADS * tm),
                            lambda i: (i // nts, 0, i % nts, 0, 0)),
               pl.BlockSpec((tm, B_KV_WIDTH), lambda i: (i, 0)),
               pl.BlockSpec((None, B_KV_HEADS, None, HEAD_DIM, tm),
                            lambda i: (i // nts, 0, (i % nts) // kpb, 0, i % kpb)),
               pl.BlockSpec((tm, GATE_WIDTH), lambda i: (i, 0))]

    return pl.pallas_call(
        _in_proj_kernel,
        out_shape=a_shapes + b_shapes,
        grid=(n // tm,),
        in_specs=[pl.BlockSpec((tm, D_MODEL), lambda i: (i, 0)),
                  pl.BlockSpec((tm, tab.shape[1]), lambda i: (i % nts, 0))] + [_full_spec(a) for a in consts],
        out_specs=a_specs + b_specs,
        scratch_shapes=[pltpu.VMEM((tm, LANES), F32)],
        compiler_params=_cparams(("parallel",)),
        name="in_proj",
    )(xf, tab, *consts)


def _attn_a_kernel(q_ref, kp_ref, kc_ref, kn_ref, vp_ref, vc_ref, vn_ref, o_ref, lse_ref, *, class_len):
    tq = q_ref.shape[0]
    tk = tq + 2 * A_RADIUS
    i = pl.program_id(2)
    q = q_ref[...]
    k = jnp.concatenate([kp_ref[...], kc_ref[...], kn_ref[...]], axis=0)
    v = jnp.concatenate([vp_ref[...], vc_ref[...], vn_ref[...]], axis=0)
    qpos = i * tq + lax.broadcasted_iota(jnp.int32, (tq, tk), 0)
    kpos = i * tq - A_RADIUS + lax.broadcasted_iota(jnp.int32, (tq, tk), 1)
    valid = (jnp.abs(kpos - qpos) <= A_RADIUS) & (kpos >= 0) & (kpos < class_len)
    for h in range(A_HEADS_PER_GROUP):
        sl = slice(h * HEAD_DIM, (h + 1) * HEAD_DIM)
        sc = lax.dot_general(q[:, sl], k[:, sl], (((1,), (1,)), ((), ())), preferred_element_type=F32)
        sc = jnp.where(valid, sc, NEG_INF)
        m = jnp.max(sc, axis=1, keepdims=True)
        p = jnp.exp2(sc - m)
        den = jnp.sum(p, axis=1, keepdims=True)
        o = jnp.dot(p.astype(BF16), v[:, sl], preferred_element_type=F32) / den
        o_ref[:, sl] = o
        lse_ref[:, sl] = jnp.broadcast_to(m * LN2 + jnp.log(den), (tq, HEAD_DIM))


def _attn_a(q, k, v, gi):
    b, dil, cl, _ = q.shape
    tq = min(A_TQ, cl)
    hb = tq // A_RADIUS
    n_halo = cl // A_RADIUS
    main = pl.BlockSpec((None, None, tq, A_GROUP_WIDTH), lambda bi, r, i: (bi, r, i, 0))
    prev = pl.BlockSpec((None, None, A_RADIUS, A_GROUP_WIDTH),
                        lambda bi, r, i: (bi, r, jnp.maximum(i * hb - 1, 0), 0))
    nxt = pl.BlockSpec((None, None, A_RADIUS, A_GROUP_WIDTH),
                       lambda bi, r, i: (bi, r, jnp.minimum((i + 1) * hb, n_halo - 1), 0))
    return pl.pallas_call(
        functools.partial(_attn_a_kernel, class_len=cl),
        out_shape=[jax.ShapeDtypeStruct((b, dil, cl, A_GROUP_WIDTH), F32)] * 2,
        grid=(b, dil, cl // tq),
        in_specs=[main, prev, main, nxt, prev, main, nxt],
        out_specs=[main, main],
        compiler_params=_cparams(("parallel", "parallel", "parallel")),
        name=f"attn_a{gi}",
    )(q, k, k, k, v, v, v)


def _flash_b_kernel(shift_ref, qt_ref, k_ref, vt_ref, o_ref, acc_sc, l_sc, m_sc, *, online):
    nk, tk, _ = k_ref.shape
    width = qt_ref.shape[1]
    tq = width // B_GROUP_HEADS
    acc_sc[...] = jnp.zeros(acc_sc.shape, F32)
    l_sc[...] = jnp.zeros(l_sc.shape, F32)
    if online:
        m_sc[...] = jnp.full(m_sc.shape, NEG_INF, F32)
    shift = shift_ref[0]

    def body(j, carry):
        sc = jnp.dot(k_ref[j], qt_ref[...], preferred_element_type=F32)
        if online:
            m_old = m_sc[...]
            m_new = jnp.maximum(m_old, jnp.max(sc, axis=0, keepdims=True))
            alpha = jnp.exp2(m_old - m_new)
            m_sc[...] = m_new
            p = jnp.exp2(sc - m_new)
            l_sc[...] = alpha * l_sc[...] + p.reshape(tk // SUBLANES, SUBLANES, width).sum(axis=0)
            acc_sc[...] = alpha * acc_sc[...]
        else:
            p = jnp.exp2(sc - shift)
            l_sc[...] += p.reshape(tk // SUBLANES, SUBLANES, width).sum(axis=0)
        pb = p.astype(BF16)
        vt = vt_ref[j]
        for h in range(B_GROUP_HEADS):
            cols = slice(h * tq, (h + 1) * tq)
            acc_sc[:, cols] += jnp.dot(vt, pb[:, cols], preferred_element_type=F32)
        return carry

    lax.fori_loop(0, nk, body, 0)
    o = acc_sc[...] / jnp.sum(l_sc[...], axis=0, keepdims=True)
    o = jnp.concatenate([o[:, h * tq:(h + 1) * tq] for h in range(B_GROUP_HEADS)], axis=0)
    o_ref[...] = o.T.astype(o_ref.dtype)


def _flash_b(shift, qt, kb, vt, online):
    b, _, nq, _, width = qt.shape
    tq = width // B_GROUP_HEADS
    _, nk, tk, _ = kb.shape
    return pl.pallas_call(
        functools.partial(_flash_b_kernel, online=online),
        out_shape=jax.ShapeDtypeStruct((b, B_KV_HEADS, nq * tq, B_GROUP_WIDTH), BF16),
        grid=(b, B_KV_HEADS, nq),
        in_specs=[pl.BlockSpec(memory_space=pltpu.SMEM),
                  pl.BlockSpec((None, None, None, B_KV_WIDTH, width), lambda bi, g, i: (bi, g, i, 0, 0)),
                  pl.BlockSpec((None, nk, tk, B_KV_WIDTH), lambda bi, g, i: (bi, 0, 0, 0)),
                  pl.BlockSpec((None, None, nk, HEAD_DIM, tk), lambda bi, g, i: (bi, g, 0, 0, 0))],
        out_specs=pl.BlockSpec((None, None, tq, B_GROUP_WIDTH), lambda bi, g, i: (bi, g, i, 0)),
        scratch_shapes=[pltpu.VMEM((HEAD_DIM, width), F32), pltpu.VMEM((SUBLANES, width), F32),
                        pltpu.VMEM((1, width), F32)],
        compiler_params=_cparams(("parallel", "parallel", "parallel")),
        name="flash_b_online" if online else "flash_b",
    )(shift, qt, kb, vt)


def _gather_classes(blk_ref, stage_ref):
    dil, rows, width = blk_ref.shape
    if dil == 1:
        return blk_ref[0]
    for c in range(width // LANES):
        for r in range(dil):
            stage_ref[c, pl.ds(r, rows, stride=dil), :] = blk_ref[r, :, c * LANES:(c + 1) * LANES]
    return jnp.concatenate([stage_ref[c] for c in range(width // LANES)], axis=1)


def _post_attn_kernel(x_ref, o0_ref, o1_ref, o2_ref, l0_ref, l1_ref, l2_ref, ob0_ref, ob1_ref, gates_ref,
                      wa_ref, wb0_ref, wb1_ref, wout_ref, gffn_ref, wrh_ref, wrl_ref,
                      x1_ref, h2_ref, aff_ref, so1, so2, sl1, sl2):
    l0 = _gather_classes(l0_ref, None)
    l1 = _gather_classes(l1_ref, sl1)
    l2 = _gather_classes(l2_ref, sl2)
    m = jnp.maximum(jnp.maximum(l0, l1), l2)
    e0, e1, e2 = jnp.exp(l0 - m), jnp.exp(l1 - m), jnp.exp(l2 - m)
    oa = (e0 * _gather_classes(o0_ref, None) + e1 * _gather_classes(o1_ref, so1)
          + e2 * _gather_classes(o2_ref, so2)) / (e0 + e1 + e2)
    ya = jnp.dot(oa.astype(BF16), wa_ref[...], preferred_element_type=F32)
    yb = (jnp.dot(ob0_ref[...], wb0_ref[...], preferred_element_type=F32)
          + jnp.dot(ob1_ref[...], wb1_ref[...], preferred_element_type=F32))
    ga = gates_ref[:, :D_MODEL].astype(F32)
    gb = gates_ref[:, D_MODEL:].astype(F32)
    z = (ga * ya + gb * yb).astype(BF16)
    x1 = x_ref[...] + jnp.dot(z, wout_ref[...], preferred_element_type=F32)
    x1_ref[...] = x1
    ms = jnp.mean(x1 * x1, axis=-1, keepdims=True)
    h2 = x1 * lax.rsqrt(ms + EPS) * gffn_ref[...]
    hi = h2.astype(BF16)
    lo = (h2 - hi.astype(F32)).astype(BF16)
    h2_ref[...] = hi
    nt = (((1,), (1,)), ((), ()))
    wrh = wrh_ref[...]
    logits = (lax.dot_general(wrh, hi, nt, preferred_element_type=F32)
              + lax.dot_general(wrh, lo, nt, preferred_element_type=F32)
              + lax.dot_general(wrl_ref[...], hi, nt, preferred_element_type=F32))
    logits = logits - jnp.max(logits, axis=0, keepdims=True)
    e = jnp.exp(logits)
    aff_ref[...] = e / jnp.sum(e, axis=0, keepdims=True)


def _post_attn(xf, oas, lses, ob, gates, b, s, w_a, w_b, w_out, g_ffn, w_router):
    n = xf.shape[0]
    tm = ROW_TILE
    nts = s // tm
    wa = w_a.astype(BF16)
    wb = w_b.astype(BF16)
    wb0, wb1 = wb[:B_GROUP_WIDTH], wb[B_GROUP_WIDTH:]
    wout = w_out.astype(BF16)
    wrt = w_router.T
    wrh = wrt.astype(BF16)
    wrl = (wrt - wrh.astype(F32)).astype(BF16)
    consts = [wa, wb0, wb1, wout, g_ffn.reshape(1, D_MODEL), wrh, wrl]

    def rows(w):
        return pl.BlockSpec((tm, w), lambda i: (i, 0))

    def class_spec(dil):
        return pl.BlockSpec((None, dil, tm // dil, A_GROUP_WIDTH), lambda i: (i // nts, 0, i % nts, 0))

    def ob_spec(g):
        return pl.BlockSpec((None, None, tm, B_GROUP_WIDTH), lambda i: (i // nts, g, i % nts, 0))

    a_specs = [class_spec(d) for d in A_DILATIONS]
    return pl.pallas_call(
        _post_attn_kernel,
        out_shape=[jax.ShapeDtypeStruct((n, D_MODEL), F32),
                   jax.ShapeDtypeStruct((n, D_MODEL), BF16),
                   jax.ShapeDtypeStruct((N_EXPERTS, n), F32)],
        grid=(n // tm,),
        in_specs=[rows(D_MODEL)] + a_specs + a_specs + [ob_spec(0), ob_spec(1), rows(GATE_WIDTH)]
                 + [_full_spec(a) for a in consts],
        out_specs=[rows(D_MODEL), rows(D_MODEL), pl.BlockSpec((N_EXPERTS, tm), lambda i: (0, i))],
        scratch_shapes=[pltpu.VMEM((A_GROUP_WIDTH // LANES, tm, LANES), F32)] * 4,
        compiler_params=_cparams(("parallel",)),
        name="post_attn",
    )(xf, *oas, *lses, ob, ob, gates, *consts)


def _moe_ffn_kernel(xe_ref, gate_ref, w1_ref, w3_ref, w2_ref, ye_ref):
    xe = xe_ref[...]
    a = jnp.dot(xe, w1_ref[...], preferred_element_type=F32)
    u = jnp.dot(xe, w3_ref[...], preferred_element_type=F32)
    hid = (a * jax.nn.sigmoid(a) * u).astype(BF16)
    ye = jnp.dot(hid, w2_ref[...], preferred_element_type=F32) * gate_ref[...]
    ye_ref[...] = ye.astype(ye_ref.dtype)


def _moe_ffn(xe, gate, w1, w3, w2):
    e, cap, _ = xe.shape
    tc = min(FFN_TC, cap)
    wspec = pl.BlockSpec((None, D_MODEL, D_MODEL), lambda ei, ci: (ei, 0, 0))
    return pl.pallas_call(
        _moe_ffn_kernel,
        out_shape=jax.ShapeDtypeStruct((e, cap, D_MODEL), BF16),
        grid=(e, cap // tc),
        in_specs=[pl.BlockSpec((None, tc, D_MODEL), lambda ei, ci: (ei, ci, 0)),
                  pl.BlockSpec((None, tc, 1), lambda ei, ci: (ei, ci, 0)),
                  wspec, wspec, wspec],
        out_specs=pl.BlockSpec((None, tc, D_MODEL), lambda ei, ci: (ei, ci, 0)),
        compiler_params=_cparams(("parallel", "arbitrary")),
        name="moe_ffn",
    )(xe, gate.reshape(e, cap, 1), w1, w3, w2)


def _final_kernel(x1_ref, y_ref, p_ref, wple_ref, wpg_ref, gple_ref, gfin_ref, out_ref):
    x2 = x1_ref[...] + y_ref[...]
    ms = jnp.mean(x2 * x2, axis=-1, keepdims=True)
    hp = (x2 * lax.rsqrt(ms + EPS) * gple_ref[...]).astype(BF16)
    gt = jax.nn.sigmoid(jnp.dot(hp, wpg_ref[...], preferred_element_type=F32))
    emb = jnp.dot(p_ref[...].astype(BF16), wple_ref[...], preferred_element_type=F32)
    x3 = x2 + emb * gt
    ms3 = jnp.mean(x3 * x3, axis=-1, keepdims=True)
    out_ref[...] = x3 * lax.rsqrt(ms3 + EPS) * gfin_ref[...]


def _final(x1, y, pf, w_ple, w_ple_gate, g_ple, g_final):
    n = x1.shape[0]
    tm = ROW_TILE
    consts = [w_ple.astype(BF16), w_ple_gate.astype(BF16), g_ple.reshape(1, D_MODEL), g_final.reshape(1, D_MODEL)]

    def rows(w):
        return pl.BlockSpec((tm, w), lambda i: (i, 0))

    return pl.pallas_call(
        _final_kernel,
        out_shape=jax.ShapeDtypeStruct((n, D_MODEL), F32),
        grid=(n // tm,),
        in_specs=[rows(D_MODEL), rows(D_MODEL), rows(PLE_DIM)] + [_full_spec(a) for a in consts],
        out_specs=rows(D_MODEL),
        compiler_params=_cparams(("parallel",)),
        name="final",
    )(x1, y, pf, *consts)


def _run_group(x, p, g_mix, w_in, g_qn, g_kn, w_a_proj, w_b_proj, w_out, g_ffn, w_router,
               w1, w3, w2, g_ple, w_ple, w_ple_gate, g_final):
    b, s, _ = x.shape
    n = b * s
    assert s % B_TK == 0 and s % (A_DILATIONS[-1] * A_RADIUS) == 0
    xf = x.reshape(n, D_MODEL)
    tab = _rope_tables(s)
    outs = _in_proj(xf, tab, b, s, g_mix, w_in, g_qn, g_kn)
    qas, kas, vas = outs[0:3], outs[3:6], outs[6:9]
    qt, kb, vt, gates = outs[9:]

    oas, lses = [], []
    for gi, (window, dil) in enumerate(A_PATTERNS):
        assert window // (2 * dil) == A_RADIUS
        o, lse = _attn_a(qas[gi], kas[gi], vas[gi], gi)
        oas.append(o)
        lses.append(lse)

    shift = (HEAD_DIM * Q_SCALE * 1.02) * jnp.max(jnp.abs(g_qn)) * jnp.max(jnp.abs(g_kn))
    shift = shift.reshape(1).astype(F32)
    kb4 = kb.reshape(b, s // B_TK, B_TK, B_KV_WIDTH)
    ob = lax.cond(shift[0] <= B_FIXED_SHIFT_MAX,
                  lambda: _flash_b(shift, qt, kb4, vt, online=False),
                  lambda: _flash_b(shift, qt, kb4, vt, online=True))

    x1, h2, aff_t = _post_attn(xf, oas, lses, ob, gates, b, s, w_a_proj, w_b_proj, w_out, g_ffn, w_router)

    cap = CAPACITY_FACTOR * n // N_EXPERTS
    gate, idx = lax.top_k(aff_t, cap)
    xe = h2[idx]
    ye = _moe_ffn(xe, gate, w1, w3, w2)
    y = jnp.zeros((n, D_MODEL), F32).at[idx.reshape(-1)].add(ye.reshape(-1, D_MODEL).astype(F32))

    out = _final(x1, y, p.reshape(n, PLE_DIM), w_ple, w_ple_gate, g_ple, g_final)
    return out.reshape(b, s, D_MODEL)


def kernel(x_prompt, x_sample, p_prompt, p_sample, g_mix, w_in, g_qn, g_kn, w_a_proj, w_b_proj, w_out, g_ffn,
           w_router, w_exp_gate, w_exp_up, w_exp_down, g_ple, w_ple, w_ple_gate, g_final):
    assert g_mix.shape[0] == 1, "single layer"
    w1 = w_exp_gate[0].astype(BF16)
    w3 = w_exp_up[0].astype(BF16)
    w2 = w_exp_down[0].astype(BF16)
    args = (g_mix[0], w_in[0], g_qn[0], g_kn[0], w_a_proj[0], w_b_proj[0], w_out[0], g_ffn[0], w_router[0],
            w1, w3, w2, g_ple[0], w_ple[0], w_ple_gate[0], g_final)
    y_prompt = _run_group(x_prompt, p_prompt[0], *args)
    y_sample = _run_group(x_sample, p_sample[0], *args)
    return (y_prompt, y_sample)
```

```python
import functools
import math

import jax
import jax.numpy as jnp
from jax import lax
from jax.experimental import pallas as pl
from jax.experimental.pallas import tpu as pltpu

F32 = jnp.float32
BF16 = jnp.bfloat16

D_MODEL = 1024
HEAD_DIM = 64
A_PATTERNS = ((128, 1), (512, 4), (2048, 16))
A_DILATIONS = tuple(d for _, d in A_PATTERNS)
A_GROUPS = len(A_PATTERNS)
A_HEADS_PER_GROUP = 4
A_WIDTH = 768
A_GROUP_WIDTH = A_HEADS_PER_GROUP * HEAD_DIM
A_ROT_DIM = 16
A_RADIUS = 64
ROPE_THETA = 500000.0
B_Q_WIDTH = 512
B_KV_HEADS = 2
B_KV_WIDTH = 128
B_GROUP_HEADS = 4
B_GROUP_WIDTH = B_GROUP_HEADS * HEAD_DIM
AXIAL_THETA = 10000.0
GRID_W = 64
GATE_WIDTH = 2048
N_EXPERTS = 16
CAPACITY_FACTOR = 2
PLE_DIM = 256
EPS = 1e-6
NEG_INF = -1e30
LN2 = math.log(2.0)
Q_SCALE = (HEAD_DIM ** -0.5) / LN2
B_FIXED_SHIFT_MAX = 50.0

LANES = 128
SUBLANES = 8
VMEM_LIMIT = 48 * 1024 * 1024

ROW_TILE = 512
A_TQ = 256
B_TQ = ROW_TILE
B_TK = 2 * ROW_TILE
FFN_TC = 512
MOE_T = ROW_TILE
MOE_WS_SHIFT = 7
MOE_WS = 1 << MOE_WS_SHIFT
MOE_ALIGN_SHIFT = 4
MOE_ALIGN = 1 << MOE_ALIGN_SHIFT
XS_WIDTH = D_MODEL + LANES
TM_POS = N_EXPERTS
TM_GATE_LO = 2 * N_EXPERTS


def _cparams(sem):
    return pltpu.CompilerParams(dimension_semantics=sem, vmem_limit_bytes=VMEM_LIMIT)


def _full_spec(a):
    return pl.BlockSpec(a.shape, lambda *_: (0,) * a.ndim)


def _rope_tables(s):
    pos = jnp.arange(s)
    posf = pos.astype(F32)
    inv_a = jnp.power(ROPE_THETA, -jnp.arange(0, A_ROT_DIM, 2, dtype=F32) / A_ROT_DIM)
    ang = posf[:, None] * inv_a[None, :]
    ca, sa = jnp.cos(ang), jnp.sin(ang)
    z8 = jnp.zeros_like(sa)
    rest = HEAD_DIM - A_ROT_DIM
    cos_a = jnp.concatenate([ca, ca, jnp.ones((s, rest), F32)], axis=1)
    sp_a = jnp.concatenate([z8, sa, jnp.zeros((s, rest), F32)], axis=1)
    sm_a = jnp.concatenate([-sa, z8, jnp.zeros((s, rest), F32)], axis=1)

    hb = HEAD_DIM // 2
    inv_b = jnp.power(AXIAL_THETA, -jnp.arange(0, hb, 2, dtype=F32) / hb)
    row = (pos // GRID_W).astype(F32)
    col = (pos % GRID_W).astype(F32)
    ar = row[:, None] * inv_b[None, :]
    ac = col[:, None] * inv_b[None, :]
    cr, sr, cc, sc = jnp.cos(ar), jnp.sin(ar), jnp.cos(ac), jnp.sin(ac)
    z16 = jnp.zeros_like(sr)
    cos_b = jnp.concatenate([cr, cr, cc, cc], axis=1)
    sp_b = jnp.concatenate([z16, sr, z16, sc], axis=1)
    sm_b = jnp.concatenate([-sr, z16, -sc, z16], axis=1)
    heads_per_tile = LANES // HEAD_DIM
    parts = [jnp.tile(t, (1, heads_per_tile)) for t in (cos_a, sp_a, sm_a, cos_b, sp_b, sm_b)]
    return jnp.concatenate(parts, axis=1)


def _rope_chunk(x, tab_ref, base, half):
    cos = tab_ref[:, base:base + LANES]
    sp = tab_ref[:, base + LANES:base + 2 * LANES]
    sm = tab_ref[:, base + 2 * LANES:base + 3 * LANES]
    return x * cos + pltpu.roll(x, half, 1) * sp + pltpu.roll(x, LANES - half, 1) * sm


def _head_mean_sq(acc, blk_ref):
    sq = acc * acc
    hi = sq.astype(BF16)
    lo = (sq - hi.astype(F32)).astype(BF16)
    blk = blk_ref[...]
    return (jnp.dot(hi, blk, preferred_element_type=F32)
            + jnp.dot(lo, blk, preferred_element_type=F32))


def _store_by_class(out_refs, chunk_idx, chunk, stage_ref):
    g, half = divmod(chunk_idx, A_GROUP_WIDTH // LANES)
    dil = A_DILATIONS[g]
    lanes = slice(half * LANES, (half + 1) * LANES)
    if dil == 1:
        out_refs[g][0, :, lanes] = chunk.astype(BF16)
        return
    tm = chunk.shape[0]
    stage_ref[...] = chunk
    for r in range(dil):
        out_refs[g][r, :, lanes] = stage_ref[pl.ds(r, tm // dil, stride=dil), :].astype(BF16)


def _in_proj_kernel(x_ref, tab_ref, gmix_ref, wqa_ref, wka_ref, wva_ref, wqb_ref, wkb_ref, wvb_ref,
                    wg_ref, gq_ref, gk_ref, blkq_ref, blkk_ref,
                    qa0_ref, qa1_ref, qa2_ref, ka0_ref, ka1_ref, ka2_ref, va0_ref, va1_ref, va2_ref,
                    qt_ref, kb_ref, vt_ref, gates_ref, stage_ref):
    x = x_ref[...]
    tm = x.shape[0]
    ms = jnp.mean(x * x, axis=-1, keepdims=True)
    h = (x * lax.rsqrt(ms + EPS) * gmix_ref[...]).astype(BF16)

    acc = jnp.dot(h, wqa_ref[...], preferred_element_type=F32)
    for c in range(A_WIDTH // LANES):
        roped = _rope_chunk(acc[:, c * LANES:(c + 1) * LANES], tab_ref, 0, A_ROT_DIM // 2) * Q_SCALE
        _store_by_class((qa0_ref, qa1_ref, qa2_ref), c, roped, stage_ref)

    acc = jnp.dot(h, wka_ref[...], preferred_element_type=F32)
    for c in range(A_WIDTH // LANES):
        roped = _rope_chunk(acc[:, c * LANES:(c + 1) * LANES], tab_ref, 0, A_ROT_DIM // 2)
        _store_by_class((ka0_ref, ka1_ref, ka2_ref), c, roped, stage_ref)

    acc = jnp.dot(h, wva_ref[...], preferred_element_type=F32)
    for c in range(A_WIDTH // LANES):
        _store_by_class((va0_ref, va1_ref, va2_ref), c, acc[:, c * LANES:(c + 1) * LANES], stage_ref)

    acc = jnp.dot(h, wqb_ref[...], preferred_element_type=F32)
    acc = acc * lax.rsqrt(_head_mean_sq(acc, blkq_ref) + EPS) * gq_ref[...]
    heads_per_chunk = LANES // HEAD_DIM
    for c in range(B_Q_WIDTH // LANES):
        roped = _rope_chunk(acc[:, c * LANES:(c + 1) * LANES], tab_ref, 3 * LANES, HEAD_DIM // 4) * Q_SCALE
        rt = roped.T.astype(BF16)
        for hh in range(heads_per_chunk):
            head = c * heads_per_chunk + hh
            g, hg = divmod(head, B_GROUP_HEADS)
            cols = slice(hg * tm, (hg + 1) * tm)
            qt_ref[g, g * HEAD_DIM:(g + 1) * HEAD_DIM, cols] = rt[hh * HEAD_DIM:(hh + 1) * HEAD_DIM, :]
            qt_ref[g, (1 - g) * HEAD_DIM:(2 - g) * HEAD_DIM, cols] = jnp.zeros((HEAD_DIM, tm), BF16)

    acc = jnp.dot(h, wkb_ref[...], preferred_element_type=F32)
    acc = acc * lax.rsqrt(_head_mean_sq(acc, blkk_ref) + EPS) * gk_ref[...]
    kb_ref[...] = _rope_chunk(acc, tab_ref, 3 * LANES, HEAD_DIM // 4).astype(BF16)

    vt = jnp.dot(h, wvb_ref[...], preferred_element_type=F32).T.astype(BF16)
    for g in range(B_KV_HEADS):
        vt_ref[g] = vt[g * HEAD_DIM:(g + 1) * HEAD_DIM, :]

    gates_ref[...] = jax.nn.sigmoid(jnp.dot(h, wg_ref[...], preferred_element_type=F32)).astype(BF16)


def _in_proj(xf, tab, b, s, gmix, w_in, gq, gk):
    n = xf.shape[0]
    tm = ROW_TILE
    nts = s // tm
    c1, c2, c3 = A_WIDTH, 2 * A_WIDTH, 3 * A_WIDTH
    c4 = c3 + B_Q_WIDTH
    c5 = c4 + B_KV_WIDTH
    c6 = c5 + B_KV_WIDTH
    wb = w_in.astype(BF16)
    ws = [wb[:, :c1], wb[:, c1:c2], wb[:, c2:c3], wb[:, c3:c4], wb[:, c4:c5], wb[:, c5:c6], wb[:, c6:]]

    def head_blockdiag(width):
        hid = jnp.arange(width) // HEAD_DIM
        return jnp.where(hid[:, None] == hid[None, :], 1.0 / HEAD_DIM, 0.0).astype(BF16)

    consts = [gmix.reshape(1, D_MODEL)] + ws + [
        jnp.tile(gq, B_Q_WIDTH // HEAD_DIM).reshape(1, B_Q_WIDTH),
        jnp.tile(gk, B_KV_WIDTH // HEAD_DIM).reshape(1, B_KV_WIDTH),
        head_blockdiag(B_Q_WIDTH), head_blockdiag(B_KV_WIDTH)]

    a_shapes, a_specs = [], []
    for _ in range(3):
        for dil in A_DILATIONS:
            a_shapes.append(jax.ShapeDtypeStruct((b, dil, s // dil, A_GROUP_WIDTH), BF16))
            a_specs.append(pl.BlockSpec((None, dil, tm // dil, A_GROUP_WIDTH),
                                        lambda i: (i // nts, 0, i % nts, 0)))
    kpb = B_TK // tm
    b_shapes = [jax.ShapeDtypeStruct((b, B_KV_HEADS, nts, B_KV_WIDTH, B_GROUP_HEADS * tm), BF16),
                jax.ShapeDtypeStruct((n, B_KV_WIDTH), BF16),
                jax.ShapeDtypeStruct((b, B_KV_HEADS, s // B_TK, HEAD_DIM, B_TK), BF16),
                jax.ShapeDtypeStruct((n, GATE_WIDTH), BF16)]
    b_specs = [pl.BlockSpec((None, B_KV_HEADS, None, B_KV_WIDTH, B_GROUP_HEADS * tm),
                            lambda i: (i // nts, 0, i % nts, 0, 0)),
               pl.BlockSpec((tm, B_KV_WIDTH), lambda i: (i, 0)),
               pl.BlockSpec((None, B_KV_HEADS, None, HEAD_DIM, tm),
                            lambda i: (i // nts, 0, (i % nts) // kpb, 0, i % kpb)),
               pl.BlockSpec((tm, GATE_WIDTH), lambda i: (i, 0))]

    return pl.pallas_call(
        _in_proj_kernel,
        out_shape=a_shapes + b_shapes,
        grid=(n // tm,),
        in_specs=[pl.BlockSpec((tm, D_MODEL), lambda i: (i, 0)),
                  pl.BlockSpec((tm, tab.shape[1]), lambda i: (i % nts, 0))] + [_full_spec(a) for a in consts],
        out_specs=a_specs + b_specs,
        scratch_shapes=[pltpu.VMEM((tm, LANES), F32)],
        compiler_params=_cparams(("parallel",)),
        name="in_proj",
    )(xf, tab, *consts)


def _attn_a_kernel(q_ref, kp_ref, kc_ref, kn_ref, vp_ref, vc_ref, vn_ref, o_ref, lse_ref, *, class_len):
    tq = q_ref.shape[0]
    tk = tq + 2 * A_RADIUS
    i = pl.program_id(2)
    q = q_ref[...]
    k = jnp.concatenate([kp_ref[...], kc_ref[...], kn_ref[...]], axis=0)
    v = jnp.concatenate([vp_ref[...], vc_ref[...], vn_ref[...]], axis=0)
    qpos = i * tq + lax.broadcasted_iota(jnp.int32, (tq, tk), 0)
    kpos = i * tq - A_RADIUS + lax.broadcasted_iota(jnp.int32, (tq, tk), 1)
    valid = (jnp.abs(kpos - qpos) <= A_RADIUS) & (kpos >= 0) & (kpos < class_len)
    for h in range(A_HEADS_PER_GROUP):
        sl = slice(h * HEAD_DIM, (h + 1) * HEAD_DIM)
        sc = lax.dot_general(q[:, sl], k[:, sl], (((1,), (1,)), ((), ())), preferred_element_type=F32)
        sc = jnp.where(valid, sc, NEG_INF)
        m = jnp.max(sc, axis=1, keepdims=True)
        p = jnp.exp2(sc - m)
        den = jnp.sum(p, axis=1, keepdims=True)
        o = jnp.dot(p.astype(BF16), v[:, sl], preferred_element_type=F32) / den
        o_ref[:, sl] = o
        lse_ref[:, sl] = jnp.broadcast_to(m * LN2 + jnp.log(den), (tq, HEAD_DIM))


def _attn_a(q, k, v, gi):
    b, dil, cl, _ = q.shape
    tq = min(A_TQ, cl)
    hb = tq // A_RADIUS
    n_halo = cl // A_RADIUS
    main = pl.BlockSpec((None, None, tq, A_GROUP_WIDTH), lambda bi, r, i: (bi, r, i, 0))
    prev = pl.BlockSpec((None, None, A_RADIUS, A_GROUP_WIDTH),
                        lambda bi, r, i: (bi, r, jnp.maximum(i * hb - 1, 0), 0))
    nxt = pl.BlockSpec((None, None, A_RADIUS, A_GROUP_WIDTH),
                       lambda bi, r, i: (bi, r, jnp.minimum((i + 1) * hb, n_halo - 1), 0))
    return pl.pallas_call(
        functools.partial(_attn_a_kernel, class_len=cl),
        out_shape=[jax.ShapeDtypeStruct((b, dil, cl, A_GROUP_WIDTH), F32)] * 2,
        grid=(b, dil, cl // tq),
        in_specs=[main, prev, main, nxt, prev, main, nxt],
        out_specs=[main, main],
        compiler_params=_cparams(("parallel", "parallel", "parallel")),
        name=f"attn_a{gi}",
    )(q, k, k, k, v, v, v)


def _flash_b_kernel(shift_ref, qt_ref, k_ref, vt_ref, o_ref, acc_sc, l_sc, m_sc, *, online):
    nk, tk, _ = k_ref.shape
    width = qt_ref.shape[1]
    tq = width // B_GROUP_HEADS
    acc_sc[...] = jnp.zeros(acc_sc.shape, F32)
    l_sc[...] = jnp.zeros(l_sc.shape, F32)
    if online:
        m_sc[...] = jnp.full(m_sc.shape, NEG_INF, F32)
    shift = shift_ref[0]

    def body(j, carry):
        sc = jnp.dot(k_ref[j], qt_ref[...], preferred_element_type=F32)
        if online:
            m_old = m_sc[...]
            m_new = jnp.maximum(m_old, jnp.max(sc, axis=0, keepdims=True))
            alpha = jnp.exp2(m_old - m_new)
            m_sc[...] = m_new
            p = jnp.exp2(sc - m_new)
            l_sc[...] = alpha * l_sc[...] + p.reshape(tk // SUBLANES, SUBLANES, width).sum(axis=0)
            acc_sc[...] = alpha * acc_sc[...]
        else:
            p = jnp.exp2(sc - shift)
            l_sc[...] += p.reshape(tk // SUBLANES, SUBLANES, width).sum(axis=0)
        pb = p.astype(BF16)
        vt = vt_ref[j]
        for h in range(B_GROUP_HEADS):
            cols = slice(h * tq, (h + 1) * tq)
            acc_sc[:, cols] += jnp.dot(vt, pb[:, cols], preferred_element_type=F32)
        return carry

    lax.fori_loop(0, nk, body, 0)
    o = acc_sc[...] / jnp.sum(l_sc[...], axis=0, keepdims=True)
    o = jnp.concatenate([o[:, h * tq:(h + 1) * tq] for h in range(B_GROUP_HEADS)], axis=0)
    o_ref[...] = o.T.astype(o_ref.dtype)


def _flash_b(shift, qt, kb, vt, online):
    b, _, nq, _, width = qt.shape
    tq = width // B_GROUP_HEADS
    _, nk, tk, _ = kb.shape
    return pl.pallas_call(
        functools.partial(_flash_b_kernel, online=online),
        out_shape=jax.ShapeDtypeStruct((b, B_KV_HEADS, nq * tq, B_GROUP_WIDTH), BF16),
        grid=(b, B_KV_HEADS, nq),
        in_specs=[pl.BlockSpec(memory_space=pltpu.SMEM),
                  pl.BlockSpec((None, None, None, B_KV_WIDTH, width), lambda bi, g, i: (bi, g, i, 0, 0)),
                  pl.BlockSpec((None, nk, tk, B_KV_WIDTH), lambda bi, g, i: (bi, 0, 0, 0)),
                  pl.BlockSpec((None, None, nk, HEAD_DIM, tk), lambda bi, g, i: (bi, g, 0, 0, 0))],
        out_specs=pl.BlockSpec((None, None, tq, B_GROUP_WIDTH), lambda bi, g, i: (bi, g, i, 0)),
        scratch_shapes=[pltpu.VMEM((HEAD_DIM, width), F32), pltpu.VMEM((SUBLANES, width), F32),
                        pltpu.VMEM((1, width), F32)],
        compiler_params=_cparams(("parallel", "parallel", "parallel")),
        name="flash_b_online" if online else "flash_b",
    )(shift, qt, kb, vt)


def _gather_classes(blk_ref, stage_ref):
    dil, rows, width = blk_ref.shape
    if dil == 1:
        return blk_ref[0]
    for c in range(width // LANES):
        for r in range(dil):
            stage_ref.at[c][pl.ds(r, rows, stride=dil), :] = blk_ref[r, :, c * LANES:(c + 1) * LANES]
    return jnp.concatenate([stage_ref[c] for c in range(width // LANES)], axis=1)


def _post_attn_kernel(x_ref, o0_ref, o1_ref, o2_ref, l0_ref, l1_ref, l2_ref, ob0_ref, ob1_ref, gates_ref,
                      wa_ref, wb0_ref, wb1_ref, wout_ref, gffn_ref, wrh_ref, wrl_ref,
                      x1_ref, h2_ref, aff_ref, so1, so2, sl1, sl2):
    l0 = _gather_classes(l0_ref, None)
    l1 = _gather_classes(l1_ref, sl1)
    l2 = _gather_classes(l2_ref, sl2)
    m = jnp.maximum(jnp.maximum(l0, l1), l2)
    e0, e1, e2 = jnp.exp(l0 - m), jnp.exp(l1 - m), jnp.exp(l2 - m)
    oa = (e0 * _gather_classes(o0_ref, None) + e1 * _gather_classes(o1_ref, so1)
          + e2 * _gather_classes(o2_ref, so2)) / (e0 + e1 + e2)
    ya = jnp.dot(oa.astype(BF16), wa_ref[...], preferred_element_type=F32)
    yb = (jnp.dot(ob0_ref[...], wb0_ref[...], preferred_element_type=F32)
          + jnp.dot(ob1_ref[...], wb1_ref[...], preferred_element_type=F32))
    ga = gates_ref[:, :D_MODEL].astype(F32)
    gb = gates_ref[:, D_MODEL:].astype(F32)
    z = (ga * ya + gb * yb).astype(BF16)
    x1 = x_ref[...] + jnp.dot(z, wout_ref[...], preferred_element_type=F32)
    x1_ref[...] = x1
    ms = jnp.mean(x1 * x1, axis=-1, keepdims=True)
    h2 = x1 * lax.rsqrt(ms + EPS) * gffn_ref[...]
    hi = h2.astype(BF16)
    lo = (h2 - hi.astype(F32)).astype(BF16)
    h2_ref[...] = hi
    nt = (((1,), (1,)), ((), ()))
    wrh = wrh_ref[...]
    logits = (lax.dot_general(wrh, hi, nt, preferred_element_type=F32)
              + lax.dot_general(wrh, lo, nt, preferred_element_type=F32)
              + lax.dot_general(wrl_ref[...], hi, nt, preferred_element_type=F32))
    logits = logits - jnp.max(logits, axis=0, keepdims=True)
    e = jnp.exp(logits)
    aff = e / jnp.sum(e, axis=0, keepdims=True)
    for j in range(aff_ref.shape[0]):
        aff_ref[j] = aff[:, j * LANES:(j + 1) * LANES]


def _post_attn(xf, oas, lses, ob, gates, b, s, w_a, w_b, w_out, g_ffn, w_router):
    n = xf.shape[0]
    tm = ROW_TILE
    nts = s // tm
    wa = w_a.astype(BF16)
    wb = w_b.astype(BF16)
    wb0, wb1 = wb[:B_GROUP_WIDTH], wb[B_GROUP_WIDTH:]
    wout = w_out.astype(BF16)
    wrt = w_router.T
    wrh = wrt.astype(BF16)
    wrl = (wrt - wrh.astype(F32)).astype(BF16)
    consts = [wa, wb0, wb1, wout, g_ffn.reshape(1, D_MODEL), wrh, wrl]

    def rows(w):
        return pl.BlockSpec((tm, w), lambda i: (i, 0))

    def class_spec(dil):
        return pl.BlockSpec((None, dil, tm // dil, A_GROUP_WIDTH), lambda i: (i // nts, 0, i % nts, 0))

    def ob_spec(g):
        return pl.BlockSpec((None, None, tm, B_GROUP_WIDTH), lambda i: (i // nts, g, i % nts, 0))

    a_specs = [class_spec(d) for d in A_DILATIONS]
    return pl.pallas_call(
        _post_attn_kernel,
        out_shape=[jax.ShapeDtypeStruct((n, D_MODEL), F32),
                   jax.ShapeDtypeStruct((n, D_MODEL), BF16),
                   jax.ShapeDtypeStruct((n // LANES, N_EXPERTS, LANES), F32)],
        grid=(n // tm,),
        in_specs=[rows(D_MODEL)] + a_specs + a_specs + [ob_spec(0), ob_spec(1), rows(GATE_WIDTH)]
                 + [_full_spec(a) for a in consts],
        out_specs=[rows(D_MODEL), rows(D_MODEL),
                   pl.BlockSpec((tm // LANES, N_EXPERTS, LANES), lambda i: (i, 0, 0))],
        scratch_shapes=[pltpu.VMEM((A_GROUP_WIDTH // LANES, tm, LANES), F32)] * 4,
        compiler_params=_cparams(("parallel",)),
        name="post_attn",
    )(xf, *oas, *lses, ob, ob, gates, *consts)


def _route_kernel(aff_ref, tri_ref, posl_ref, starts_ref, sel_sc, *, cap, idx_bits, tiles_per_block):
    nlt = aff_ref.shape[0]
    capf = float(cap)

    def count(mask):
        part = jnp.sum(jnp.where(mask, 1.0, 0.0), axis=0)
        return jnp.sum(part, axis=1, keepdims=True)[None]

    def value_body(i, tau):
        bits = pltpu.bitcast(aff_ref[...], jnp.int32)
        cand = tau | lax.shift_left(jnp.int32(1), 30 - i)
        return jnp.where(count(bits >= cand) >= capf, cand, tau)

    tau = lax.fori_loop(0, 31, value_body, jnp.zeros((1, N_EXPERTS, 1), jnp.int32))
    bits = pltpu.bitcast(aff_ref[...], jnp.int32)
    need = capf - count(bits > tau)
    tok = (lax.broadcasted_iota(jnp.int32, bits.shape, 0) * LANES
           + lax.broadcasted_iota(jnp.int32, bits.shape, 2))

    def index_body(i, last):
        b = pltpu.bitcast(aff_ref[...], jnp.int32)
        cand = last | lax.shift_left(jnp.int32(1), idx_bits - 1 - i)
        return jnp.where(count((b == tau) & (tok < cand)) < need, cand, last)

    last = lax.fori_loop(0, idx_bits, index_body, jnp.zeros((1, N_EXPERTS, 1), jnp.int32))
    sel_sc[...] = jnp.where((bits > tau) | ((bits == tau) & (tok <= last)), 1.0, 0.0)

    tri = tri_ref[...]
    ones = jnp.ones((LANES, LANES), BF16)

    def prefix_body(c, before):
        @pl.when(c % tiles_per_block == 0)
        def _():
            starts_ref[c // tiles_per_block] = before.astype(jnp.int32)

        s = sel_sc[c]
        sb = s.astype(BF16)
        pos = before + jnp.dot(sb, tri, preferred_element_type=F32) - s
        posl_ref[c] = jnp.where(s > 0, pos, -1.0).astype(jnp.int32)
        return before + jnp.dot(sb, ones, preferred_element_type=F32)

    lax.fori_loop(0, nlt, prefix_body, jnp.zeros((N_EXPERTS, LANES), F32))


def _route(aff3, cap):
    nlt = aff3.shape[0]
    n = nlt * LANES
    tpb = MOE_T // LANES
    tri = (jnp.arange(LANES)[:, None] <= jnp.arange(LANES)[None, :]).astype(BF16)
    vm = pl.BlockSpec(memory_space=pltpu.VMEM)
    return pl.pallas_call(
        functools.partial(_route_kernel, cap=cap, idx_bits=(n - 1).bit_length(), tiles_per_block=tpb),
        out_shape=[jax.ShapeDtypeStruct((nlt, N_EXPERTS, LANES), jnp.int32),
                   jax.ShapeDtypeStruct((n // MOE_T, N_EXPERTS, LANES), jnp.int32)],
        in_specs=[vm, vm],
        out_specs=[vm, vm],
        scratch_shapes=[pltpu.VMEM((nlt, N_EXPERTS, LANES), F32)],
        compiler_params=pltpu.CompilerParams(vmem_limit_bytes=VMEM_LIMIT),
        name="route",
    )(aff3, tri)


def _block_slots(s_ref, i, nt):
    return [(s_ref[e * (nt + 1) + i], s_ref[e * (nt + 1) + i + 1]) for e in range(N_EXPERTS)]


def _align_down(x):
    return (x >> MOE_ALIGN_SHIFT) << MOE_ALIGN_SHIFT


def _moe_gather_kernel(s_ref, aff_ref, posl_ref, h2_ref, xs_hbm, tm_ref,
                       haug_sc, pall_sc, res_sc, stage_sc, carry_sc, pending_sc, sem, *, nt, cap):
    i = pl.program_id(0)
    nj = posl_ref.shape[0]
    ws = MOE_WS

    @pl.when(i == 0)
    def _():
        carry_sc[...] = jnp.zeros(carry_sc.shape, F32)
        pending_sc[0] = 0
        pad = xs_hbm.shape[1] - cap
        stage_sc[...] = jnp.zeros(stage_sc.shape, BF16)
        fills = [pltpu.make_async_copy(stage_sc.at[pl.ds(0, pad)], xs_hbm.at[e, pl.ds(cap, pad)], sem)
                 for e in range(N_EXPERTS)]
        for cp in fills:
            cp.start()
        for cp in fills:
            cp.wait()

    for j in range(nj):
        pj = posl_ref[j]
        gj = jnp.where(pj >= 0, aff_ref[j], 0.0)
        blk = jnp.concatenate([gj, pj.astype(F32), gj, jnp.zeros((LANES - 3 * N_EXPERTS, LANES), F32)], axis=0)
        tm_ref[j * LANES:(j + 1) * LANES, :] = blk.T
    tmv = tm_ref[...]
    lane = lax.broadcasted_iota(jnp.int32, tmv.shape, 1)
    ghi = tmv.astype(BF16)
    glo = (tmv - ghi.astype(F32)).astype(BF16)
    zero = jnp.zeros_like(ghi)
    haug_sc[:, :D_MODEL] = h2_ref[...]
    haug_sc[:, D_MODEL:] = jnp.where(lane < N_EXPERTS, ghi,
                                     jnp.where((lane >= TM_GATE_LO) & (lane < TM_GATE_LO + N_EXPERTS), glo, zero))

    astarts, kks, offcs = [], [], []
    nw = jnp.int32(0)
    for st, en in _block_slots(s_ref, i, nt):
        ast = _align_down(st)
        span = _align_down(en) - ast
        kk = span >> MOE_WS_SHIFT
        astarts.append(ast)
        kks.append(kk)
        offcs.append(span - kk * ws)
        nw = jnp.maximum(nw, kk + 1)

    def wait_window():
        for _ in range(N_EXPERTS):
            pltpu.make_async_copy(stage_sc.at[pl.ds(0, ws)], xs_hbm.at[0, pl.ds(0, ws)], sem).wait()

    row = lax.broadcasted_iota(jnp.int32, (ws, LANES), 0)

    def window(k, carry):
        for e in range(N_EXPERTS):
            base = astarts[e] + k * ws
            for j in range(nj):
                hit = (posl_ref[j, e:e + 1, :] - base) == row
                pall_sc[e * ws:(e + 1) * ws, j * LANES:(j + 1) * LANES] = jnp.where(hit, 1.0, 0.0).astype(BF16)
        res_sc[...] = jnp.dot(pall_sc[...], haug_sc[...], preferred_element_type=F32)

        @pl.when(k == 0)
        def _():
            for e in range(N_EXPERTS):
                res_sc[e * ws:e * ws + MOE_ALIGN, :] += carry_sc[e]

        for e in range(N_EXPERTS):
            @pl.when(k == kks[e])
            def _():
                carry_sc[e] = res_sc[pl.ds(pl.multiple_of(e * ws + offcs[e], MOE_ALIGN), MOE_ALIGN), :]

        @pl.when(pending_sc[0] == 1)
        def _():
            wait_window()

        stage_sc[...] = res_sc[...].astype(BF16)
        for e in range(N_EXPERTS):
            base = pl.multiple_of(astarts[e] + k * ws, MOE_ALIGN)
            pltpu.make_async_copy(stage_sc.at[pl.ds(e * ws, ws)], xs_hbm.at[e, pl.ds(base, ws)], sem).start()
        pending_sc[0] = 1
        return carry

    lax.fori_loop(0, nw, window, 0)

    @pl.when((i == nt - 1) & (pending_sc[0] == 1))
    def _():
        wait_window()
        pending_sc[0] = 0


def _moe_gather(s_flat, aff3, posl, h2, cap):
    n = h2.shape[0]
    t = MOE_T
    nt = n // t
    cap_pad = cap + (t // MOE_WS + 1) * MOE_WS
    blk3 = pl.BlockSpec((t // LANES, N_EXPERTS, LANES), lambda i, s: (i, 0, 0))
    grid_spec = pltpu.PrefetchScalarGridSpec(
        num_scalar_prefetch=1,
        grid=(nt,),
        in_specs=[blk3, blk3, pl.BlockSpec((t, D_MODEL), lambda i, s: (i, 0))],
        out_specs=[pl.BlockSpec(memory_space=pl.ANY), pl.BlockSpec((t, LANES), lambda i, s: (i, 0))],
        scratch_shapes=[pltpu.VMEM((t, XS_WIDTH), BF16),
                        pltpu.VMEM((N_EXPERTS * MOE_WS, t), BF16),
                        pltpu.VMEM((N_EXPERTS * MOE_WS, XS_WIDTH), F32),
                        pltpu.VMEM((N_EXPERTS * MOE_WS, XS_WIDTH), BF16),
                        pltpu.VMEM((N_EXPERTS, MOE_ALIGN, XS_WIDTH), F32),
                        pltpu.SMEM((1,), jnp.int32),
                        pltpu.SemaphoreType.DMA(())])
    return pl.pallas_call(
        functools.partial(_moe_gather_kernel, nt=nt, cap=cap),
        out_shape=[jax.ShapeDtypeStruct((N_EXPERTS, cap_pad, XS_WIDTH), BF16),
                   jax.ShapeDtypeStruct((n, LANES), F32)],
        grid_spec=grid_spec,
        compiler_params=_cparams(("arbitrary",)),
        name="moe_gather",
    )(s_flat, aff3, posl, h2)


def _moe_ffn_kernel(xs_ref, w1_ref, w3_ref, w2_ref, ye_ref):
    xe = xs_ref[:, :D_MODEL]
    route = xs_ref[:, D_MODEL:].astype(F32)
    lane = lax.broadcasted_iota(jnp.int32, route.shape, 1)
    e = pl.program_id(0)
    gate = jnp.sum(jnp.where((lane == e) | (lane == e + TM_GATE_LO), route, 0.0), axis=1, keepdims=True)
    a = jnp.dot(xe, w1_ref[...], preferred_element_type=F32)
    u = jnp.dot(xe, w3_ref[...], preferred_element_type=F32)
    hid = (a * jax.nn.sigmoid(a) * u).astype(BF16)
    ye = jnp.dot(hid, w2_ref[...], preferred_element_type=F32) * gate
    ye_ref[...] = ye.astype(ye_ref.dtype)


def _moe_ffn(xs, cap, w1, w3, w2):
    e = xs.shape[0]
    tc = min(FFN_TC, cap)
    wspec = pl.BlockSpec((None, D_MODEL, D_MODEL), lambda ei, ci: (ei, 0, 0))
    return pl.pallas_call(
        _moe_ffn_kernel,
        out_shape=jax.ShapeDtypeStruct((e, cap, D_MODEL), BF16),
        grid=(e, cap // tc),
        in_specs=[pl.BlockSpec((None, tc, XS_WIDTH), lambda ei, ci: (ei, ci, 0)), wspec, wspec, wspec],
        out_specs=pl.BlockSpec((None, tc, D_MODEL), lambda ei, ci: (ei, ci, 0)),
        compiler_params=_cparams(("parallel", "arbitrary")),
        name="moe_ffn",
    )(xs, w1, w3, w2)


def _final_kernel(s_ref, x1_ref, tm_ref, p_ref, ye_hbm, wple_ref, wpg_ref, gple_ref, gfin_ref, out_ref,
                  buf_sc, pall_sc, y_sc, sem, *, nt, cap):
    i = pl.program_id(0)
    ws = MOE_WS
    t = x1_ref.shape[0]
    lows, nw = [], jnp.int32(0)
    for st, en in _block_slots(s_ref, i, nt):
        ast = _align_down(st)
        lows.append(ast)
        nw = jnp.maximum(nw, (en - ast + ws - 1) >> MOE_WS_SHIFT)
    y_sc[...] = jnp.zeros(y_sc.shape, F32)
    col = lax.broadcasted_iota(jnp.int32, (t, ws), 1).astype(F32)

    def window(k, carry):
        copies = []
        for e in range(N_EXPERTS):
            lo = lows[e] + k * ws
            base = pl.multiple_of(jnp.minimum(lo, cap - ws), MOE_ALIGN)
            cp = pltpu.make_async_copy(ye_hbm.at[e, pl.ds(base, ws)], buf_sc.at[pl.ds(e * ws, ws)], sem)
            cp.start()
            copies.append(cp)
            slot = tm_ref[:, TM_POS + e:TM_POS + e + 1]
            hit = ((slot - base.astype(F32)) == col) & (slot >= lo.astype(F32))
            pall_sc[:, e * ws:(e + 1) * ws] = jnp.where(hit, 1.0, 0.0).astype(BF16)
        for cp in copies:
            cp.wait()
        y_sc[...] += jnp.dot(pall_sc[...], buf_sc[...], preferred_element_type=F32)
        return carry

    lax.fori_loop(0, nw, window, 0)

    x2 = x1_ref[...] + y_sc[...]
    ms = jnp.mean(x2 * x2, axis=-1, keepdims=True)
    hp = (x2 * lax.rsqrt(ms + EPS) * gple_ref[...]).astype(BF16)
    gt = jax.nn.sigmoid(jnp.dot(hp, wpg_ref[...], preferred_element_type=F32))
    emb = jnp.dot(p_ref[...].astype(BF16), wple_ref[...], preferred_element_type=F32)
    x3 = x2 + emb * gt
    ms3 = jnp.mean(x3 * x3, axis=-1, keepdims=True)
    out_ref[...] = x3 * lax.rsqrt(ms3 + EPS) * gfin_ref[...]


def _final(s_flat, x1, tm, pf, ye, w_ple, w_ple_gate, g_ple, g_final):
    n = x1.shape[0]
    t = MOE_T
    nt = n // t
    cap = ye.shape[1]
    consts = [w_ple.astype(BF16), w_ple_gate.astype(BF16), g_ple.reshape(1, D_MODEL), g_final.reshape(1, D_MODEL)]

    def rows(w):
        return pl.BlockSpec((t, w), lambda i, s: (i, 0))

    grid_spec = pltpu.PrefetchScalarGridSpec(
        num_scalar_prefetch=1,
        grid=(nt,),
        in_specs=[rows(D_MODEL), rows(LANES), rows(PLE_DIM), pl.BlockSpec(memory_space=pl.ANY)]
                 + [_full_spec(a) for a in consts],
        out_specs=rows(D_MODEL),
        scratch_shapes=[pltpu.VMEM((N_EXPERTS * MOE_WS, D_MODEL), BF16),
                        pltpu.VMEM((t, N_EXPERTS * MOE_WS), BF16),
                        pltpu.VMEM((t, D_MODEL), F32),
                        pltpu.SemaphoreType.DMA(())])
    return pl.pallas_call(
        functools.partial(_final_kernel, nt=nt, cap=cap),
        out_shape=jax.ShapeDtypeStruct((n, D_MODEL), F32),
        grid_spec=grid_spec,
        compiler_params=_cparams(("arbitrary",)),
        name="final",
    )(s_flat, x1, tm, pf, ye, *consts)


def _run_group(x, p, g_mix, w_in, g_qn, g_kn, w_a_proj, w_b_proj, w_out, g_ffn, w_router,
               w1, w3, w2, g_ple, w_ple, w_ple_gate, g_final):
    b, s, _ = x.shape
    n = b * s
    assert s % B_TK == 0 and s % (A_DILATIONS[-1] * A_RADIUS) == 0
    xf = x.reshape(n, D_MODEL)
    tab = _rope_tables(s)
    outs = _in_proj(xf, tab, b, s, g_mix, w_in, g_qn, g_kn)
    qas, kas, vas = outs[0:3], outs[3:6], outs[6:9]
    qt, kb, vt, gates = outs[9:]

    oas, lses = [], []
    for gi, (window, dil) in enumerate(A_PATTERNS):
        assert window // (2 * dil) == A_RADIUS
        o, lse = _attn_a(qas[gi], kas[gi], vas[gi], gi)
        oas.append(o)
        lses.append(lse)

    shift = (HEAD_DIM * Q_SCALE * 1.02) * jnp.max(jnp.abs(g_qn)) * jnp.max(jnp.abs(g_kn))
    shift = shift.reshape(1).astype(F32)
    kb4 = kb.reshape(b, s // B_TK, B_TK, B_KV_WIDTH)
    ob = lax.cond(shift[0] <= B_FIXED_SHIFT_MAX,
                  lambda: _flash_b(shift, qt, kb4, vt, online=False),
                  lambda: _flash_b(shift, qt, kb4, vt, online=True))

    x1, h2, aff3 = _post_attn(xf, oas, lses, ob, gates, b, s, w_a_proj, w_b_proj, w_out, g_ffn, w_router)

    cap = CAPACITY_FACTOR * n // N_EXPERTS
    assert cap % MOE_WS == 0 and cap >= MOE_WS
    posl, starts = _route(aff3, cap)
    s_flat = jnp.concatenate([starts[:, :, 0].T, jnp.full((N_EXPERTS, 1), cap, jnp.int32)], axis=1).reshape(-1)
    xs, tm = _moe_gather(s_flat, aff3, posl, h2, cap)
    ye = _moe_ffn(xs, cap, w1, w3, w2)
    out = _final(s_flat, x1, tm, p.reshape(n, PLE_DIM), ye, w_ple, w_ple_gate, g_ple, g_final)
    return out.reshape(b, s, D_MODEL)


def kernel(x_prompt, x_sample, p_prompt, p_sample, g_mix, w_in, g_qn, g_kn, w_a_proj, w_b_proj, w_out, g_ffn,
           w_router, w_exp_gate, w_exp_up, w_exp_down, g_ple, w_ple, w_ple_gate, g_final):
    assert g_mix.shape[0] == 1, "single layer"
    w1 = w_exp_gate[0].astype(BF16)
    w3 = w_exp_up[0].astype(BF16)
    w2 = w_exp_down[0].astype(BF16)
    args = (g_mix[0], w_in[0], g_qn[0], g_kn[0], w_a_proj[0], w_b_proj[0], w_out[0], g_ffn[0], w_router[0],
            w1, w3, w2, g_ple[0], w_ple[0], w_ple_gate[0], g_final)
    y_prompt = _run_group(x_prompt, p_prompt[0], *args)
    y_sample = _run_group(x_sample, p_sample[0], *args)
    return (y_prompt, y_sample)
```

```python
import functools
import math

import jax
import jax.numpy as jnp
from jax import lax
from jax.experimental import pallas as pl
from jax.experimental.pallas import tpu as pltpu

F32 = jnp.float32
BF16 = jnp.bfloat16

D_MODEL = 1024
HEAD_DIM = 64
A_PATTERNS = ((128, 1), (512, 4), (2048, 16))
A_DILATIONS = tuple(d for _, d in A_PATTERNS)
A_GROUPS = len(A_PATTERNS)
A_HEADS_PER_GROUP = 4
A_WIDTH = 768
A_GROUP_WIDTH = A_HEADS_PER_GROUP * HEAD_DIM
A_ROT_DIM = 16
A_RADIUS = 64
ROPE_THETA = 500000.0
B_Q_WIDTH = 512
B_KV_HEADS = 2
B_KV_WIDTH = 128
B_GROUP_HEADS = 4
B_GROUP_WIDTH = B_GROUP_HEADS * HEAD_DIM
AXIAL_THETA = 10000.0
GRID_W = 64
GATE_WIDTH = 2048
N_EXPERTS = 16
CAPACITY_FACTOR = 2
PLE_DIM = 256
EPS = 1e-6
NEG_INF = -1e30
LN2 = math.log(2.0)
Q_SCALE = (HEAD_DIM ** -0.5) / LN2
B_FIXED_SHIFT_MAX = 50.0

LANES = 128
SUBLANES = 8
VMEM_LIMIT = 48 * 1024 * 1024

ROW_TILE = 512
A_TQ = 512
A_SUB = 128
B_TQ = ROW_TILE
B_TK = 4 * ROW_TILE
POST_SPLIT = 1
FFN_TC = 512
MOE_T = ROW_TILE
MOE_WS_SHIFT = 7
MOE_WS = 1 << MOE_WS_SHIFT
MOE_ALIGN_SHIFT = 4
MOE_ALIGN = 1 << MOE_ALIGN_SHIFT
XS_WIDTH = D_MODEL + LANES
TM_POS = N_EXPERTS
TM_GATE_LO = 2 * N_EXPERTS


def _cparams(sem):
    return pltpu.CompilerParams(dimension_semantics=sem, vmem_limit_bytes=VMEM_LIMIT)


def _full_spec(a):
    return pl.BlockSpec(a.shape, lambda *_: (0,) * a.ndim)


def _rope_tables(s):
    pos = jnp.arange(s)
    posf = pos.astype(F32)
    inv_a = jnp.power(ROPE_THETA, -jnp.arange(0, A_ROT_DIM, 2, dtype=F32) / A_ROT_DIM)
    ang = posf[:, None] * inv_a[None, :]
    ca, sa = jnp.cos(ang), jnp.sin(ang)
    z8 = jnp.zeros_like(sa)
    rest = HEAD_DIM - A_ROT_DIM
    cos_a = jnp.concatenate([ca, ca, jnp.ones((s, rest), F32)], axis=1)
    sp_a = jnp.concatenate([z8, sa, jnp.zeros((s, rest), F32)], axis=1)
    sm_a = jnp.concatenate([-sa, z8, jnp.zeros((s, rest), F32)], axis=1)

    hb = HEAD_DIM // 2
    inv_b = jnp.power(AXIAL_THETA, -jnp.arange(0, hb, 2, dtype=F32) / hb)
    row = (pos // GRID_W).astype(F32)
    col = (pos % GRID_W).astype(F32)
    ar = row[:, None] * inv_b[None, :]
    ac = col[:, None] * inv_b[None, :]
    cr, sr, cc, sc = jnp.cos(ar), jnp.sin(ar), jnp.cos(ac), jnp.sin(ac)
    z16 = jnp.zeros_like(sr)
    cos_b = jnp.concatenate([cr, cr, cc, cc], axis=1)
    sp_b = jnp.concatenate([z16, sr, z16, sc], axis=1)
    sm_b = jnp.concatenate([-sr, z16, -sc, z16], axis=1)
    heads_per_tile = LANES // HEAD_DIM
    parts = [jnp.tile(t, (1, heads_per_tile)) for t in (cos_a, sp_a, sm_a, cos_b, sp_b, sm_b)]
    return jnp.concatenate(parts, axis=1)


def _rope_chunk(x, tab_ref, base, half):
    cos = tab_ref[:, base:base + LANES]
    sp = tab_ref[:, base + LANES:base + 2 * LANES]
    sm = tab_ref[:, base + 2 * LANES:base + 3 * LANES]
    return x * cos + pltpu.roll(x, half, 1) * sp + pltpu.roll(x, LANES - half, 1) * sm


def _head_mean_sq(acc, blk_ref):
    sq = acc * acc
    hi = sq.astype(BF16)
    lo = (sq - hi.astype(F32)).astype(BF16)
    blk = blk_ref[...]
    return (jnp.dot(hi, blk, preferred_element_type=F32)
            + jnp.dot(lo, blk, preferred_element_type=F32))


def _store_by_class(out_refs, chunk_idx, chunk, stage_ref):
    g, half = divmod(chunk_idx, A_GROUP_WIDTH // LANES)
    dil = A_DILATIONS[g]
    lanes = slice(half * LANES, (half + 1) * LANES)
    if dil == 1:
        out_refs[g][0, :, lanes] = chunk.astype(BF16)
        return
    tm = chunk.shape[0]
    stage_ref[...] = chunk
    for r in range(dil):
        out_refs[g][r, :, lanes] = stage_ref[pl.ds(r, tm // dil, stride=dil), :].astype(BF16)


def _in_proj_kernel(x_ref, tab_ref, gmix_ref, wqa_ref, wka_ref, wva_ref, wqb_ref, wkb_ref, wvb_ref,
                    wg_ref, gq_ref, gk_ref, blkq_ref, blkk_ref,
                    qa0_ref, qa1_ref, qa2_ref, ka0_ref, ka1_ref, ka2_ref, va0_ref, va1_ref, va2_ref,
                    qt_ref, kb_ref, vt_ref, gates_ref, stage_ref):
    x = x_ref[...]
    tm = x.shape[0]
    ms = jnp.mean(x * x, axis=-1, keepdims=True)
    h = (x * lax.rsqrt(ms + EPS) * gmix_ref[...]).astype(BF16)

    acc = jnp.dot(h, wqa_ref[...], preferred_element_type=F32)
    for c in range(A_WIDTH // LANES):
        roped = _rope_chunk(acc[:, c * LANES:(c + 1) * LANES], tab_ref, 0, A_ROT_DIM // 2) * Q_SCALE
        _store_by_class((qa0_ref, qa1_ref, qa2_ref), c, roped, stage_ref)

    acc = jnp.dot(h, wka_ref[...], preferred_element_type=F32)
    for c in range(A_WIDTH // LANES):
        roped = _rope_chunk(acc[:, c * LANES:(c + 1) * LANES], tab_ref, 0, A_ROT_DIM // 2)
        _store_by_class((ka0_ref, ka1_ref, ka2_ref), c, roped, stage_ref)

    acc = jnp.dot(h, wva_ref[...], preferred_element_type=F32)
    for c in range(A_WIDTH // LANES):
        _store_by_class((va0_ref, va1_ref, va2_ref), c, acc[:, c * LANES:(c + 1) * LANES], stage_ref)

    acc = jnp.dot(h, wqb_ref[...], preferred_element_type=F32)
    acc = acc * lax.rsqrt(_head_mean_sq(acc, blkq_ref) + EPS) * gq_ref[...]
    heads_per_chunk = LANES // HEAD_DIM
    for c in range(B_Q_WIDTH // LANES):
        roped = _rope_chunk(acc[:, c * LANES:(c + 1) * LANES], tab_ref, 3 * LANES, HEAD_DIM // 4) * Q_SCALE
        rt = roped.T.astype(BF16)
        for hh in range(heads_per_chunk):
            head = c * heads_per_chunk + hh
            g, hg = divmod(head, B_GROUP_HEADS)
            cols = slice(hg * tm, (hg + 1) * tm)
            qt_ref[g, g * HEAD_DIM:(g + 1) * HEAD_DIM, cols] = rt[hh * HEAD_DIM:(hh + 1) * HEAD_DIM, :]
            qt_ref[g, (1 - g) * HEAD_DIM:(2 - g) * HEAD_DIM, cols] = jnp.zeros((HEAD_DIM, tm), BF16)

    acc = jnp.dot(h, wkb_ref[...], preferred_element_type=F32)
    acc = acc * lax.rsqrt(_head_mean_sq(acc, blkk_ref) + EPS) * gk_ref[...]
    kb_ref[...] = _rope_chunk(acc, tab_ref, 3 * LANES, HEAD_DIM // 4).astype(BF16)

    vt = jnp.dot(h, wvb_ref[...], preferred_element_type=F32).T.astype(BF16)
    for g in range(B_KV_HEADS):
        vt_ref[g] = vt[g * HEAD_DIM:(g + 1) * HEAD_DIM, :]

    gates_ref[...] = jax.nn.sigmoid(jnp.dot(h, wg_ref[...], preferred_element_type=F32)).astype(BF16)


def _in_proj(xf, tab, b, s, gmix, w_in, gq, gk):
    n = xf.shape[0]
    tm = ROW_TILE
    nts = s // tm
    c1, c2, c3 = A_WIDTH, 2 * A_WIDTH, 3 * A_WIDTH
    c4 = c3 + B_Q_WIDTH
    c5 = c4 + B_KV_WIDTH
    c6 = c5 + B_KV_WIDTH
    wb = w_in.astype(BF16)
    ws = [wb[:, :c1], wb[:, c1:c2], wb[:, c2:c3], wb[:, c3:c4], wb[:, c4:c5], wb[:, c5:c6], wb[:, c6:]]

    def head_blockdiag(width):
        hid = jnp.arange(width) // HEAD_DIM
        return jnp.where(hid[:, None] == hid[None, :], 1.0 / HEAD_DIM, 0.0).astype(BF16)

    consts = [gmix.reshape(1, D_MODEL)] + ws + [
        jnp.tile(gq, B_Q_WIDTH // HEAD_DIM).reshape(1, B_Q_WIDTH),
        jnp.tile(gk, B_KV_WIDTH // HEAD_DIM).reshape(1, B_KV_WIDTH),
        head_blockdiag(B_Q_WIDTH), head_blockdiag(B_KV_WIDTH)]

    a_shapes, a_specs = [], []
    for _ in range(3):
        for dil in A_DILATIONS:
            a_shapes.append(jax.ShapeDtypeStruct((b, dil, s // dil, A_GROUP_WIDTH), BF16))
            a_specs.append(pl.BlockSpec((None, dil, tm // dil, A_GROUP_WIDTH),
                                        lambda i: (i // nts, 0, i % nts, 0)))
    kpb = B_TK // tm
    b_shapes = [jax.ShapeDtypeStruct((b, B_KV_HEADS, nts, B_KV_WIDTH, B_GROUP_HEADS * tm), BF16),
                jax.ShapeDtypeStruct((n, B_KV_WIDTH), BF16),
                jax.ShapeDtypeStruct((b, B_KV_HEADS, s // B_TK, HEAD_DIM, B_TK), BF16),
                jax.ShapeDtypeStruct((n, GATE_WIDTH), BF16)]
    b_specs = [pl.BlockSpec((None, B_KV_HEADS, None, B_KV_WIDTH, B_GROUP_HEADS * tm),
                            lambda i: (i // nts, 0, i % nts, 0, 0)),
               pl.BlockSpec((tm, B_KV_WIDTH), lambda i: (i, 0)),
               pl.BlockSpec((None, B_KV_HEADS, None, HEAD_DIM, tm),
                            lambda i: (i // nts, 0, (i % nts) // kpb, 0, i % kpb)),
               pl.BlockSpec((tm, GATE_WIDTH), lambda i: (i, 0))]

    return pl.pallas_call(
        _in_proj_kernel,
        out_shape=a_shapes + b_shapes,
        grid=(n // tm,),
        in_specs=[pl.BlockSpec((tm, D_MODEL), lambda i: (i, 0)),
                  pl.BlockSpec((tm, tab.shape[1]), lambda i: (i % nts, 0))] + [_full_spec(a) for a in consts],
        out_specs=a_specs + b_specs,
        scratch_shapes=[pltpu.VMEM((tm, LANES), F32)],
        compiler_params=_cparams(("parallel",)),
        name="in_proj",
    )(xf, tab, *consts)


def _band_bias():
    qi = jnp.arange(A_SUB)[:, None]
    kj = jnp.arange(A_SUB + 2 * A_RADIUS)[None, :]
    band = jnp.abs(kj - A_RADIUS - qi) <= A_RADIUS
    first = band & (kj >= A_RADIUS)
    last = band & (kj < A_SUB + A_RADIUS)
    return jnp.where(jnp.stack([band, first, last]), 0.0, NEG_INF).astype(F32)


def _attn_a_kernel(bias_ref, q_ref, kp_ref, kc_ref, kn_ref, vp_ref, vc_ref, vn_ref, o_ref, lse_ref):
    tq = q_ref.shape[0]
    nsub = tq // A_SUB
    i = pl.program_id(2)
    last_tile = pl.num_programs(2) - 1
    first_head = lax.broadcasted_iota(jnp.int32, (1, LANES), 1) < HEAD_DIM
    nt_dims = (((1,), (1,)), ((), ()))

    def window(p_ref, c_ref, n_ref, u, cols):
        lo, hi = u * A_SUB - A_RADIUS, (u + 1) * A_SUB + A_RADIUS
        parts = [p_ref[:, cols]] if lo < 0 else []
        parts.append(c_ref[max(lo, 0):min(hi, tq), cols])
        if hi > tq:
            parts.append(n_ref[:, cols])
        return parts[0] if len(parts) == 1 else jnp.concatenate(parts, axis=0)

    for u in range(nsub):
        if u == 0:
            bias = bias_ref[jnp.where(i == 0, 1, 0)]
        elif u == nsub - 1:
            bias = bias_ref[jnp.where(i == last_tile, 2, 0)]
        else:
            bias = bias_ref[0]
        rows = slice(u * A_SUB, (u + 1) * A_SUB)
        for pair in range(A_GROUP_WIDTH // LANES):
            cols = slice(pair * LANES, (pair + 1) * LANES)
            k = window(kp_ref, kc_ref, kn_ref, u, cols)
            v = window(vp_ref, vc_ref, vn_ref, u, cols)
            q = q_ref[rows, cols]
            outs, lses = [], []
            for hh in range(LANES // HEAD_DIM):
                mine = first_head if hh == 0 else jnp.logical_not(first_head)
                qm = jnp.where(mine, q, jnp.zeros_like(q))
                sc = lax.dot_general(qm, k, nt_dims, preferred_element_type=F32) + bias
                m = jnp.max(sc, axis=1, keepdims=True)
                p = jnp.exp2(sc - m)
                den = jnp.sum(p, axis=1, keepdims=True)
                outs.append(jnp.dot(p.astype(BF16), v, preferred_element_type=F32) / den)
                lses.append(m * LN2 + jnp.log(den))
            o_ref[rows, cols] = jnp.where(first_head, outs[0], outs[1])
            lse_ref[rows, cols] = jnp.where(first_head, lses[0], lses[1])


def _attn_a(q, k, v, gi):
    b, dil, cl, _ = q.shape
    tq = min(A_TQ, cl)
    assert tq >= 2 * A_SUB and A_SUB == 2 * A_RADIUS
    hb = tq // A_RADIUS
    n_halo = cl // A_RADIUS
    bias = _band_bias()
    main = pl.BlockSpec((None, None, tq, A_GROUP_WIDTH), lambda bi, r, i: (bi, r, i, 0))
    prev = pl.BlockSpec((None, None, A_RADIUS, A_GROUP_WIDTH),
                        lambda bi, r, i: (bi, r, jnp.maximum(i * hb - 1, 0), 0))
    nxt = pl.BlockSpec((None, None, A_RADIUS, A_GROUP_WIDTH),
                       lambda bi, r, i: (bi, r, jnp.minimum((i + 1) * hb, n_halo - 1), 0))
    return pl.pallas_call(
        _attn_a_kernel,
        out_shape=[jax.ShapeDtypeStruct((b, dil, cl, A_GROUP_WIDTH), F32)] * 2,
        grid=(b, dil, cl // tq),
        in_specs=[_full_spec(bias), main, prev, main, nxt, prev, main, nxt],
        out_specs=[main, main],
        compiler_params=_cparams(("parallel", "parallel", "parallel")),
        name=f"attn_a{gi}",
    )(bias, q, k, k, k, v, v, v)


def _flash_b_kernel(shift_ref, qt_ref, k_ref, vt_ref, o_ref, acc_sc, l_sc, m_sc, *, online):
    nk, tk, _ = k_ref.shape
    width = qt_ref.shape[1]
    tq = width // B_GROUP_HEADS
    acc_sc[...] = jnp.zeros(acc_sc.shape, F32)
    l_sc[...] = jnp.zeros(l_sc.shape, F32)
    if online:
        m_sc[...] = jnp.full(m_sc.shape, NEG_INF, F32)
    shift = shift_ref[0]

    def body(j, carry):
        sc = jnp.dot(k_ref[j], qt_ref[...], preferred_element_type=F32)
        if online:
            m_old = m_sc[...]
            m_new = jnp.maximum(m_old, jnp.max(sc, axis=0, keepdims=True))
            alpha = jnp.exp2(m_old - m_new)
            m_sc[...] = m_new
            p = jnp.exp2(sc - m_new)
            l_sc[...] = alpha * l_sc[...] + p.reshape(tk // SUBLANES, SUBLANES, width).sum(axis=0)
            acc_sc[...] = alpha * acc_sc[...]
        else:
            p = jnp.exp2(sc - shift)
            l_sc[...] += p.reshape(tk // SUBLANES, SUBLANES, width).sum(axis=0)
        pb = p.astype(BF16)
        vt = vt_ref[j]
        for h in range(B_GROUP_HEADS):
            cols = slice(h * tq, (h + 1) * tq)
            acc_sc[:, cols] += jnp.dot(vt, pb[:, cols], preferred_element_type=F32)
        return carry

    lax.fori_loop(0, nk, body, 0)
    o = acc_sc[...] / jnp.sum(l_sc[...], axis=0, keepdims=True)
    o = jnp.concatenate([o[:, h * tq:(h + 1) * tq] for h in range(B_GROUP_HEADS)], axis=0)
    o_ref[...] = o.T.astype(o_ref.dtype)


def _flash_b(shift, qt, kb, vt, online):
    b, _, nq, _, width = qt.shape
    tq = width // B_GROUP_HEADS
    _, nk, tk, _ = kb.shape
    return pl.pallas_call(
        functools.partial(_flash_b_kernel, online=online),
        out_shape=jax.ShapeDtypeStruct((b, B_KV_HEADS, nq * tq, B_GROUP_WIDTH), BF16),
        grid=(b, B_KV_HEADS, nq),
        in_specs=[pl.BlockSpec(memory_space=pltpu.SMEM),
                  pl.BlockSpec((None, None, None, B_KV_WIDTH, width), lambda bi, g, i: (bi, g, i, 0, 0)),
                  pl.BlockSpec((None, nk, tk, B_KV_WIDTH), lambda bi, g, i: (bi, 0, 0, 0)),
                  pl.BlockSpec((None, None, nk, HEAD_DIM, tk), lambda bi, g, i: (bi, g, 0, 0, 0))],
        out_specs=pl.BlockSpec((None, None, tq, B_GROUP_WIDTH), lambda bi, g, i: (bi, g, i, 0)),
        scratch_shapes=[pltpu.VMEM((HEAD_DIM, width), F32), pltpu.VMEM((SUBLANES, width), F32),
                        pltpu.VMEM((1, width), F32)],
        compiler_params=_cparams(("parallel", "parallel", "parallel")),
        name="flash_b_online" if online else "flash_b",
    )(shift, qt, kb, vt)


def _stage_classes(blk_ref, stage_ref):
    dil, rows, width = blk_ref.shape
    if dil == 1:
        return
    for c in range(width // LANES):
        for r in range(dil):
            stage_ref.at[c][pl.ds(r, rows, stride=dil), :] = blk_ref[r, :, c * LANES:(c + 1) * LANES]


def _token_rows(blk_ref, stage_ref, rows):
    if blk_ref.shape[0] == 1:
        return blk_ref[0, rows, :]
    return jnp.concatenate([stage_ref[c, rows, :] for c in range(stage_ref.shape[0])], axis=1)


def _post_attn_kernel(x_ref, o0_ref, o1_ref, o2_ref, l0_ref, l1_ref, l2_ref, ob0_ref, ob1_ref, gates_ref,
                      wa_ref, wb0_ref, wb1_ref, wout_ref, gffn_ref, wrh_ref, wrl_ref,
                      x1_ref, h2_ref, aff_ref, so1, so2, sl1, sl2):
    tm = x_ref.shape[0]
    for blk, stage in ((l1_ref, sl1), (l2_ref, sl2), (o1_ref, so1), (o2_ref, so2)):
        _stage_classes(blk, stage)
    nt = (((1,), (1,)), ((), ()))
    hr = tm // POST_SPLIT
    for half in range(POST_SPLIT):
        rows = slice(half * hr, (half + 1) * hr)
        l0 = _token_rows(l0_ref, None, rows)
        l1 = _token_rows(l1_ref, sl1, rows)
        l2 = _token_rows(l2_ref, sl2, rows)
        m = jnp.maximum(jnp.maximum(l0, l1), l2)
        e0, e1, e2 = jnp.exp(l0 - m), jnp.exp(l1 - m), jnp.exp(l2 - m)
        oa = (e0 * _token_rows(o0_ref, None, rows) + e1 * _token_rows(o1_ref, so1, rows)
              + e2 * _token_rows(o2_ref, so2, rows)) / (e0 + e1 + e2)
        ya = jnp.dot(oa.astype(BF16), wa_ref[...], preferred_element_type=F32)
        yb = (jnp.dot(ob0_ref[rows, :], wb0_ref[...], preferred_element_type=F32)
              + jnp.dot(ob1_ref[rows, :], wb1_ref[...], preferred_element_type=F32))
        ga = gates_ref[rows, :D_MODEL].astype(F32)
        gb = gates_ref[rows, D_MODEL:].astype(F32)
        z = (ga * ya + gb * yb).astype(BF16)
        x1 = x_ref[rows, :] + jnp.dot(z, wout_ref[...], preferred_element_type=F32)
        x1_ref[rows, :] = x1
        ms = jnp.mean(x1 * x1, axis=-1, keepdims=True)
        h2 = x1 * lax.rsqrt(ms + EPS) * gffn_ref[...]
        hi = h2.astype(BF16)
        lo = (h2 - hi.astype(F32)).astype(BF16)
        h2_ref[rows, :] = hi
        wrh = wrh_ref[...]
        logits = (lax.dot_general(wrh, hi, nt, preferred_element_type=F32)
                  + lax.dot_general(wrh, lo, nt, preferred_element_type=F32)
                  + lax.dot_general(wrl_ref[...], hi, nt, preferred_element_type=F32))
        logits = logits - jnp.max(logits, axis=0, keepdims=True)
        e = jnp.exp(logits)
        aff = e / jnp.sum(e, axis=0, keepdims=True)
        for j in range(hr // LANES):
            aff_ref[half * (hr // LANES) + j] = aff[:, j * LANES:(j + 1) * LANES]


def _post_attn(xf, oas, lses, ob, gates, b, s, w_a, w_b, w_out, g_ffn, w_router):
    n = xf.shape[0]
    tm = ROW_TILE
    nts = s // tm
    wa = w_a.astype(BF16)
    wb = w_b.astype(BF16)
    wb0, wb1 = wb[:B_GROUP_WIDTH], wb[B_GROUP_WIDTH:]
    wout = w_out.astype(BF16)
    wrt = w_router.T
    wrh = wrt.astype(BF16)
    wrl = (wrt - wrh.astype(F32)).astype(BF16)
    consts = [wa, wb0, wb1, wout, g_ffn.reshape(1, D_MODEL), wrh, wrl]

    def rows(w):
        return pl.BlockSpec((tm, w), lambda i: (i, 0))

    def class_spec(dil):
        return pl.BlockSpec((None, dil, tm // dil, A_GROUP_WIDTH), lambda i: (i // nts, 0, i % nts, 0))

    def ob_spec(g):
        return pl.BlockSpec((None, None, tm, B_GROUP_WIDTH), lambda i: (i // nts, g, i % nts, 0))

    a_specs = [class_spec(d) for d in A_DILATIONS]
    return pl.pallas_call(
        _post_attn_kernel,
        out_shape=[jax.ShapeDtypeStruct((n, D_MODEL), F32),
                   jax.ShapeDtypeStruct((n, D_MODEL), BF16),
                   jax.ShapeDtypeStruct((n // LANES, N_EXPERTS, LANES), F32)],
        grid=(n // tm,),
        in_specs=[rows(D_MODEL)] + a_specs + a_specs + [ob_spec(0), ob_spec(1), rows(GATE_WIDTH)]
                 + [_full_spec(a) for a in consts],
        out_specs=[rows(D_MODEL), rows(D_MODEL),
                   pl.BlockSpec((tm // LANES, N_EXPERTS, LANES), lambda i: (i, 0, 0))],
        scratch_shapes=[pltpu.VMEM((A_GROUP_WIDTH // LANES, tm, LANES), F32)] * 4,
        compiler_params=_cparams(("parallel",)),
        name="post_attn",
    )(xf, *oas, *lses, ob, ob, gates, *consts)


def _route_kernel(aff_ref, tri_ref, posl_ref, starts_ref, sel_sc, *, cap, idx_bits, tiles_per_block):
    nlt = aff_ref.shape[0]
    capf = float(cap)

    def count(mask):
        part = jnp.sum(jnp.where(mask, 1.0, 0.0), axis=0)
        return jnp.sum(part, axis=1, keepdims=True)[None]

    def value_body(i, tau):
        bits = pltpu.bitcast(aff_ref[...], jnp.int32)
        cand = tau | lax.shift_left(jnp.int32(1), 30 - i)
        return jnp.where(count(bits >= cand) >= capf, cand, tau)

    tau = lax.fori_loop(0, 31, value_body, jnp.zeros((1, N_EXPERTS, 1), jnp.int32))
    bits = pltpu.bitcast(aff_ref[...], jnp.int32)
    need = capf - count(bits > tau)
    tok = (lax.broadcasted_iota(jnp.int32, bits.shape, 0) * LANES
           + lax.broadcasted_iota(jnp.int32, bits.shape, 2))

    def index_body(i, last):
        b = pltpu.bitcast(aff_ref[...], jnp.int32)
        cand = last | lax.shift_left(jnp.int32(1), idx_bits - 1 - i)
        return jnp.where(count((b == tau) & (tok < cand)) < need, cand, last)

    last = lax.fori_loop(0, idx_bits, index_body, jnp.zeros((1, N_EXPERTS, 1), jnp.int32))
    sel_sc[...] = jnp.where((bits > tau) | ((bits == tau) & (tok <= last)), 1.0, 0.0)

    tri = tri_ref[...]
    ones = jnp.ones((LANES, LANES), BF16)

    def prefix_body(c, before):
        @pl.when(c % tiles_per_block == 0)
        def _():
            starts_ref[c // tiles_per_block] = before.astype(jnp.int32)

        s = sel_sc[c]
        sb = s.astype(BF16)
        pos = before + jnp.dot(sb, tri, preferred_element_type=F32) - s
        posl_ref[c] = jnp.where(s > 0, pos, -1.0).astype(jnp.int32)
        return before + jnp.dot(sb, ones, preferred_element_type=F32)

    lax.fori_loop(0, nlt, prefix_body, jnp.zeros((N_EXPERTS, LANES), F32))


def _route(aff3, cap):
    nlt = aff3.shape[0]
    n = nlt * LANES
    tpb = MOE_T // LANES
    tri = (jnp.arange(LANES)[:, None] <= jnp.arange(LANES)[None, :]).astype(BF16)
    vm = pl.BlockSpec(memory_space=pltpu.VMEM)
    return pl.pallas_call(
        functools.partial(_route_kernel, cap=cap, idx_bits=(n - 1).bit_length(), tiles_per_block=tpb),
        out_shape=[jax.ShapeDtypeStruct((nlt, N_EXPERTS, LANES), jnp.int32),
                   jax.ShapeDtypeStruct((n // MOE_T, N_EXPERTS, LANES), jnp.int32)],
        in_specs=[vm, vm],
        out_specs=[vm, vm],
        scratch_shapes=[pltpu.VMEM((nlt, N_EXPERTS, LANES), F32)],
        compiler_params=pltpu.CompilerParams(vmem_limit_bytes=VMEM_LIMIT),
        name="route",
    )(aff3, tri)


def _block_slots(s_ref, i, nt):
    return [(s_ref[e * (nt + 1) + i], s_ref[e * (nt + 1) + i + 1]) for e in range(N_EXPERTS)]


def _align_down(x):
    return (x >> MOE_ALIGN_SHIFT) << MOE_ALIGN_SHIFT


def _moe_gather_kernel(s_ref, aff_ref, posl_ref, h2_ref, xs_hbm, tm_ref,
                       haug_sc, pall_sc, res_sc, stage_sc, carry_sc, pending_sc, sem, *, nt, cap):
    i = pl.program_id(0)
    nj = posl_ref.shape[0]
    ws = MOE_WS

    @pl.when(i == 0)
    def _():
        carry_sc[...] = jnp.zeros(carry_sc.shape, F32)
        pending_sc[0] = 0
        pad = xs_hbm.shape[1] - cap
        stage_sc[...] = jnp.zeros(stage_sc.shape, BF16)
        fills = [pltpu.make_async_copy(stage_sc.at[pl.ds(0, pad)], xs_hbm.at[e, pl.ds(cap, pad)], sem)
                 for e in range(N_EXPERTS)]
        for cp in fills:
            cp.start()
        for cp in fills:
            cp.wait()

    for j in range(nj):
        pj = posl_ref[j]
        gj = jnp.where(pj >= 0, aff_ref[j], 0.0)
        blk = jnp.concatenate([gj, pj.astype(F32), gj, jnp.zeros((LANES - 3 * N_EXPERTS, LANES), F32)], axis=0)
        tm_ref[j * LANES:(j + 1) * LANES, :] = blk.T
    tmv = tm_ref[...]
    lane = lax.broadcasted_iota(jnp.int32, tmv.shape, 1)
    ghi = tmv.astype(BF16)
    glo = (tmv - ghi.astype(F32)).astype(BF16)
    zero = jnp.zeros_like(ghi)
    haug_sc[:, :D_MODEL] = h2_ref[...]
    haug_sc[:, D_MODEL:] = jnp.where(lane < N_EXPERTS, ghi,
                                     jnp.where((lane >= TM_GATE_LO) & (lane < TM_GATE_LO + N_EXPERTS), glo, zero))

    astarts, kks, offcs = [], [], []
    nw = jnp.int32(0)
    for st, en in _block_slots(s_ref, i, nt):
        ast = _align_down(st)
        span = _align_down(en) - ast
        kk = span >> MOE_WS_SHIFT
        astarts.append(ast)
        kks.append(kk)
        offcs.append(span - kk * ws)
        nw = jnp.maximum(nw, kk + 1)

    def wait_window():
        for _ in range(N_EXPERTS):
            pltpu.make_async_copy(stage_sc.at[pl.ds(0, ws)], xs_hbm.at[0, pl.ds(0, ws)], sem).wait()

    row = lax.broadcasted_iota(jnp.int32, (ws, LANES), 0)

    def window(k, carry):
        for e in range(N_EXPERTS):
            base = astarts[e] + k * ws
            for j in range(nj):
                hit = (posl_ref[j, e:e + 1, :] - base) == row
                pall_sc[e * ws:(e + 1) * ws, j * LANES:(j + 1) * LANES] = jnp.where(hit, 1.0, 0.0).astype(BF16)
        res_sc[...] = jnp.dot(pall_sc[...], haug_sc[...], preferred_element_type=F32)

        @pl.when(k == 0)
        def _():
            for e in range(N_EXPERTS):
                res_sc[e * ws:e * ws + MOE_ALIGN, :] += carry_sc[e]

        for e in range(N_EXPERTS):
            @pl.when(k == kks[e])
            def _():
                carry_sc[e] = res_sc[pl.ds(pl.multiple_of(e * ws + offcs[e], MOE_ALIGN), MOE_ALIGN), :]

        @pl.when(pending_sc[0] == 1)
        def _():
            wait_window()

        stage_sc[...] = res_sc[...].astype(BF16)
        for e in range(N_EXPERTS):
            base = pl.multiple_of(astarts[e] + k * ws, MOE_ALIGN)
            pltpu.make_async_copy(stage_sc.at[pl.ds(e * ws, ws)], xs_hbm.at[e, pl.ds(base, ws)], sem).start()
        pending_sc[0] = 1
        return carry

    lax.fori_loop(0, nw, window, 0)

    @pl.when((i == nt - 1) & (pending_sc[0] == 1))
    def _():
        wait_window()
        pending_sc[0] = 0


def _moe_gather(s_flat, aff3, posl, h2, cap):
    n = h2.shape[0]
    t = MOE_T
    nt = n // t
    cap_pad = cap + (t // MOE_WS + 1) * MOE_WS
    blk3 = pl.BlockSpec((t // LANES, N_EXPERTS, LANES), lambda i, s: (i, 0, 0))
    grid_spec = pltpu.PrefetchScalarGridSpec(
        num_scalar_prefetch=1,
        grid=(nt,),
        in_specs=[blk3, blk3, pl.BlockSpec((t, D_MODEL), lambda i, s: (i, 0))],
        out_specs=[pl.BlockSpec(memory_space=pl.ANY), pl.BlockSpec((t, LANES), lambda i, s: (i, 0))],
        scratch_shapes=[pltpu.VMEM((t, XS_WIDTH), BF16),
                        pltpu.VMEM((N_EXPERTS * MOE_WS, t), BF16),
                        pltpu.VMEM((N_EXPERTS * MOE_WS, XS_WIDTH), F32),
                        pltpu.VMEM((N_EXPERTS * MOE_WS, XS_WIDTH), BF16),
                        pltpu.VMEM((N_EXPERTS, MOE_ALIGN, XS_WIDTH), F32),
                        pltpu.SMEM((1,), jnp.int32),
                        pltpu.SemaphoreType.DMA(())])
    return pl.pallas_call(
        functools.partial(_moe_gather_kernel, nt=nt, cap=cap),
        out_shape=[jax.ShapeDtypeStruct((N_EXPERTS, cap_pad, XS_WIDTH), BF16),
                   jax.ShapeDtypeStruct((n, LANES), F32)],
        grid_spec=grid_spec,
        compiler_params=_cparams(("arbitrary",)),
        name="moe_gather",
    )(s_flat, aff3, posl, h2)


def _moe_ffn_kernel(xs_ref, w1_ref, w3_ref, w2_ref, ye_ref):
    xe = xs_ref[:, :D_MODEL]
    route = xs_ref[:, D_MODEL:].astype(F32)
    lane = lax.broadcasted_iota(jnp.int32, route.shape, 1)
    e = pl.program_id(0)
    gate = jnp.sum(jnp.where((lane == e) | (lane == e + TM_GATE_LO), route, 0.0), axis=1, keepdims=True)
    a = jnp.dot(xe, w1_ref[...], preferred_element_type=F32)
    u = jnp.dot(xe, w3_ref[...], preferred_element_type=F32)
    hid = (a * jax.nn.sigmoid(a) * u).astype(BF16)
    ye = jnp.dot(hid, w2_ref[...], preferred_element_type=F32) * gate
    ye_ref[...] = ye.astype(ye_ref.dtype)


def _moe_ffn(xs, cap, w1, w3, w2):
    e = xs.shape[0]
    tc = min(FFN_TC, cap)
    wspec = pl.BlockSpec((None, D_MODEL, D_MODEL), lambda ei, ci: (ei, 0, 0))
    return pl.pallas_call(
        _moe_ffn_kernel,
        out_shape=jax.ShapeDtypeStruct((e, cap, D_MODEL), BF16),
        grid=(e, cap // tc),
        in_specs=[pl.BlockSpec((None, tc, XS_WIDTH), lambda ei, ci: (ei, ci, 0)), wspec, wspec, wspec],
        out_specs=pl.BlockSpec((None, tc, D_MODEL), lambda ei, ci: (ei, ci, 0)),
        compiler_params=_cparams(("parallel", "arbitrary")),
        name="moe_ffn",
    )(xs, w1, w3, w2)


def _final_kernel(s_ref, x1_ref, tm_ref, p_ref, ye_hbm, wple_ref, wpg_ref, gple_ref, gfin_ref, expand_ref, out_ref,
                  buf_sc, pall_sc, y_sc, sem, *, nt, cap):
    i = pl.program_id(0)
    ws = MOE_WS
    t = x1_ref.shape[0]
    lows, nw = [], jnp.int32(0)
    for st, en in _block_slots(s_ref, i, nt):
        ast = _align_down(st)
        lows.append(ast)
        nw = jnp.maximum(nw, (en - ast + ws - 1) >> MOE_WS_SHIFT)
    y_sc[...] = jnp.zeros(y_sc.shape, F32)
    lane = lax.broadcasted_iota(jnp.int32, (1, LANES), 1)
    col = (lax.broadcasted_iota(jnp.int32, (1, N_EXPERTS * ws), 1) & (ws - 1)).astype(F32)

    def window(k, carry):
        copies = []
        basev = jnp.zeros((1, LANES), F32)
        lov = jnp.full((1, LANES), float(2 ** 30), F32)
        for e in range(N_EXPERTS):
            lo = lows[e] + k * ws
            base = pl.multiple_of(jnp.minimum(lo, cap - ws), MOE_ALIGN)
            cp = pltpu.make_async_copy(ye_hbm.at[e, pl.ds(base, ws)], buf_sc.at[pl.ds(e * ws, ws)], sem)
            cp.start()
            copies.append(cp)
            basev = jnp.where(lane == TM_POS + e, base.astype(F32), basev)
            lov = jnp.where(lane == TM_POS + e, lo.astype(F32), lov)
        tmv = tm_ref[...]
        off = tmv - basev
        off = jnp.where((tmv >= lov) & (off >= 0.0) & (off < float(ws)), off, -1.0)
        spread = jnp.dot(off.astype(BF16), expand_ref[...], preferred_element_type=F32)
        pall_sc[...] = jnp.where(spread == col, 1.0, 0.0).astype(BF16)
        for cp in copies:
            cp.wait()
        y_sc[...] += jnp.dot(pall_sc[...], buf_sc[...], preferred_element_type=F32)
        return carry

    lax.fori_loop(0, nw, window, 0)

    hr = t // POST_SPLIT
    for half in range(POST_SPLIT):
        rows = slice(half * hr, (half + 1) * hr)
        x2 = x1_ref[rows, :] + y_sc[rows, :]
        ms = jnp.mean(x2 * x2, axis=-1, keepdims=True)
        hp = (x2 * lax.rsqrt(ms + EPS) * gple_ref[...]).astype(BF16)
        gt = jax.nn.sigmoid(jnp.dot(hp, wpg_ref[...], preferred_element_type=F32))
        emb = jnp.dot(p_ref[rows, :].astype(BF16), wple_ref[...], preferred_element_type=F32)
        x3 = x2 + emb * gt
        ms3 = jnp.mean(x3 * x3, axis=-1, keepdims=True)
        out_ref[rows, :] = x3 * lax.rsqrt(ms3 + EPS) * gfin_ref[...]


def _final(s_flat, x1, tm, pf, ye, w_ple, w_ple_gate, g_ple, g_final):
    n = x1.shape[0]
    t = MOE_T
    nt = n // t
    cap = ye.shape[1]
    src = jnp.arange(LANES)[:, None] - TM_POS
    dst = jnp.arange(N_EXPERTS * MOE_WS)[None, :] >> MOE_WS_SHIFT
    expand = (src == dst).astype(BF16)
    consts = [w_ple.astype(BF16), w_ple_gate.astype(BF16), g_ple.reshape(1, D_MODEL), g_final.reshape(1, D_MODEL),
              expand]

    def rows(w):
        return pl.BlockSpec((t, w), lambda i, s: (i, 0))

    grid_spec = pltpu.PrefetchScalarGridSpec(
        num_scalar_prefetch=1,
        grid=(nt,),
        in_specs=[rows(D_MODEL), rows(LANES), rows(PLE_DIM), pl.BlockSpec(memory_space=pl.ANY)]
                 + [_full_spec(a) for a in consts],
        out_specs=rows(D_MODEL),
        scratch_shapes=[pltpu.VMEM((N_EXPERTS * MOE_WS, D_MODEL), BF16),
                        pltpu.VMEM((t, N_EXPERTS * MOE_WS), BF16),
                        pltpu.VMEM((t, D_MODEL), F32),
                        pltpu.SemaphoreType.DMA(())])
    return pl.pallas_call(
        functools.partial(_final_kernel, nt=nt, cap=cap),
        out_shape=jax.ShapeDtypeStruct((n, D_MODEL), F32),
        grid_spec=grid_spec,
        compiler_params=_cparams(("arbitrary",)),
        name="final",
    )(s_flat, x1, tm, pf, ye, *consts)


def _run_group(x, p, g_mix, w_in, g_qn, g_kn, w_a_proj, w_b_proj, w_out, g_ffn, w_router,
               w1, w3, w2, g_ple, w_ple, w_ple_gate, g_final):
    b, s, _ = x.shape
    n = b * s
    assert s % B_TK == 0 and s % (A_DILATIONS[-1] * A_RADIUS) == 0
    xf = x.reshape(n, D_MODEL)
    tab = _rope_tables(s)
    outs = _in_proj(xf, tab, b, s, g_mix, w_in, g_qn, g_kn)
    qas, kas, vas = outs[0:3], outs[3:6], outs[6:9]
    qt, kb, vt, gates = outs[9:]

    oas, lses = [], []
    for gi, (window, dil) in enumerate(A_PATTERNS):
        assert window // (2 * dil) == A_RADIUS
        o, lse = _attn_a(qas[gi], kas[gi], vas[gi], gi)
        oas.append(o)
        lses.append(lse)

    shift = (HEAD_DIM * Q_SCALE * 1.02) * jnp.max(jnp.abs(g_qn)) * jnp.max(jnp.abs(g_kn))
    shift = shift.reshape(1).astype(F32)
    kb4 = kb.reshape(b, s // B_TK, B_TK, B_KV_WIDTH)
    ob = lax.cond(shift[0] <= B_FIXED_SHIFT_MAX,
                  lambda: _flash_b(shift, qt, kb4, vt, online=False),
                  lambda: _flash_b(shift, qt, kb4, vt, online=True))

    x1, h2, aff3 = _post_attn(xf, oas, lses, ob, gates, b, s, w_a_proj, w_b_proj, w_out, g_ffn, w_router)

    cap = CAPACITY_FACTOR * n // N_EXPERTS
    assert cap % MOE_WS == 0 and cap >= MOE_WS
    posl, starts = _route(aff3, cap)
    s_flat = jnp.concatenate([starts[:, :, 0].T, jnp.full((N_EXPERTS, 1), cap, jnp.int32)], axis=1).reshape(-1)
    xs, tm = _moe_gather(s_flat, aff3, posl, h2, cap)
    ye = _moe_ffn(xs, cap, w1, w3, w2)
    out = _final(s_flat, x1, tm, p.reshape(n, PLE_DIM), ye, w_ple, w_ple_gate, g_ple, g_final)
    return out.reshape(b, s, D_MODEL)


def kernel(x_prompt, x_sample, p_prompt, p_sample, g_mix, w_in, g_qn, g_kn, w_a_proj, w_b_proj, w_out, g_ffn,
           w_router, w_exp_gate, w_exp_up, w_exp_down, g_ple, w_ple, w_ple_gate, g_final):
    assert g_mix.shape[0] == 1, "single layer"
    w1 = w_exp_gate[0].astype(BF16)
    w3 = w_exp_up[0].astype(BF16)
    w2 = w_exp_down[0].astype(BF16)
    args = (g_mix[0], w_in[0], g_qn[0], g_kn[0], w_a_proj[0], w_b_proj[0], w_out[0], g_ffn[0], w_router[0],
            w1, w3, w2, g_ple[0], w_ple[0], w_ple_gate[0], g_final)
    y_prompt = _run_group(x_prompt, p_prompt[0], *args)
    y_sample = _run_group(x_sample, p_sample[0], *args)
    return (y_prompt, y_sample)
```

```python
import functools
import math

import jax
import jax.numpy as jnp
from jax import lax
from jax.experimental import pallas as pl
from jax.experimental.pallas import tpu as pltpu

F32 = jnp.float32
BF16 = jnp.bfloat16

D_MODEL = 1024
HEAD_DIM = 64
A_PATTERNS = ((128, 1), (512, 4), (2048, 16))
A_DILATIONS = tuple(d for _, d in A_PATTERNS)
A_GROUPS = len(A_PATTERNS)
A_HEADS_PER_GROUP = 4
A_WIDTH = 768
A_GROUP_WIDTH = A_HEADS_PER_GROUP * HEAD_DIM
A_ROT_DIM = 16
A_RADIUS = 64
ROPE_THETA = 500000.0
B_Q_WIDTH = 512
B_KV_HEADS = 2
B_KV_WIDTH = 128
B_GROUP_HEADS = 4
B_GROUP_WIDTH = B_GROUP_HEADS * HEAD_DIM
AXIAL_THETA = 10000.0
GRID_W = 64
GATE_WIDTH = 2048
N_EXPERTS = 16
CAPACITY_FACTOR = 2
PLE_DIM = 256
EPS = 1e-6
NEG_INF = -1e30
LN2 = math.log(2.0)
Q_SCALE = (HEAD_DIM ** -0.5) / LN2
B_FIXED_SHIFT_MAX = 50.0

LANES = 128
SUBLANES = 8
VMEM_LIMIT = 48 * 1024 * 1024

ROW_TILE = 512
A_TQ = 512
A_SUB = 128
B_TQ = ROW_TILE
B_TK = 4 * ROW_TILE
POST_SPLIT = 1
FFN_TC = 512
MOE_T = ROW_TILE
MOE_WS_SHIFT = 7
MOE_WS = 1 << MOE_WS_SHIFT
MOE_ALIGN_SHIFT = 4
MOE_ALIGN = 1 << MOE_ALIGN_SHIFT
XS_WIDTH = D_MODEL + LANES
TM_POS = N_EXPERTS
TM_GATE_LO = 2 * N_EXPERTS


def _cparams(sem):
    return pltpu.CompilerParams(dimension_semantics=sem, vmem_limit_bytes=VMEM_LIMIT)


def _full_spec(a):
    return pl.BlockSpec(a.shape, lambda *_: (0,) * a.ndim)


def _rope_tables(s):
    pos = jnp.arange(s)
    posf = pos.astype(F32)
    inv_a = jnp.power(ROPE_THETA, -jnp.arange(0, A_ROT_DIM, 2, dtype=F32) / A_ROT_DIM)
    ang = posf[:, None] * inv_a[None, :]
    ca, sa = jnp.cos(ang), jnp.sin(ang)
    z8 = jnp.zeros_like(sa)
    rest = HEAD_DIM - A_ROT_DIM
    cos_a = jnp.concatenate([ca, ca, jnp.ones((s, rest), F32)], axis=1)
    sp_a = jnp.concatenate([z8, sa, jnp.zeros((s, rest), F32)], axis=1)
    sm_a = jnp.concatenate([-sa, z8, jnp.zeros((s, rest), F32)], axis=1)

    hb = HEAD_DIM // 2
    inv_b = jnp.power(AXIAL_THETA, -jnp.arange(0, hb, 2, dtype=F32) / hb)
    row = (pos // GRID_W).astype(F32)
    col = (pos % GRID_W).astype(F32)
    ar = row[:, None] * inv_b[None, :]
    ac = col[:, None] * inv_b[None, :]
    cr, sr, cc, sc = jnp.cos(ar), jnp.sin(ar), jnp.cos(ac), jnp.sin(ac)
    z16 = jnp.zeros_like(sr)
    cos_b = jnp.concatenate([cr, cr, cc, cc], axis=1)
    sp_b = jnp.concatenate([z16, sr, z16, sc], axis=1)
    sm_b = jnp.concatenate([-sr, z16, -sc, z16], axis=1)
    heads_per_tile = LANES // HEAD_DIM
    parts = [jnp.tile(t, (1, heads_per_tile)) for t in (cos_a, sp_a, sm_a, cos_b, sp_b, sm_b)]
    return jnp.concatenate(parts, axis=1)


def _rope_chunk(x, tab_ref, base, half):
    cos = tab_ref[:, base:base + LANES]
    sp = tab_ref[:, base + LANES:base + 2 * LANES]
    sm = tab_ref[:, base + 2 * LANES:base + 3 * LANES]
    return x * cos + pltpu.roll(x, half, 1) * sp + pltpu.roll(x, LANES - half, 1) * sm


def _head_mean_sq(acc, blk_ref):
    sq = acc * acc
    hi = sq.astype(BF16)
    lo = (sq - hi.astype(F32)).astype(BF16)
    blk = blk_ref[...]
    return (jnp.dot(hi, blk, preferred_element_type=F32)
            + jnp.dot(lo, blk, preferred_element_type=F32))


def _store_by_class(out_refs, chunk_idx, chunk, stage_ref):
    g, half = divmod(chunk_idx, A_GROUP_WIDTH // LANES)
    dil = A_DILATIONS[g]
    lanes = slice(half * LANES, (half + 1) * LANES)
    if dil == 1:
        out_refs[g][0, :, lanes] = chunk.astype(BF16)
        return
    tm = chunk.shape[0]
    stage_ref[...] = chunk
    for r in range(dil):
        out_refs[g][r, :, lanes] = stage_ref[pl.ds(r, tm // dil, stride=dil), :].astype(BF16)


def _in_proj_kernel(x_ref, tab_ref, gmix_ref, wqa_ref, wka_ref, wva_ref, wqb_ref, wkb_ref, wvb_ref,
                    wg_ref, gq_ref, gk_ref, blkq_ref, blkk_ref,
                    qa0_ref, qa1_ref, qa2_ref, ka0_ref, ka1_ref, ka2_ref, va0_ref, va1_ref, va2_ref,
                    qt_ref, kb_ref, vt_ref, gates_ref, stage_ref):
    x = x_ref[...]
    tm = x.shape[0]
    ms = jnp.mean(x * x, axis=-1, keepdims=True)
    h = (x * lax.rsqrt(ms + EPS) * gmix_ref[...]).astype(BF16)

    acc = jnp.dot(h, wqa_ref[...], preferred_element_type=F32)
    for c in range(A_WIDTH // LANES):
        roped = _rope_chunk(acc[:, c * LANES:(c + 1) * LANES], tab_ref, 0, A_ROT_DIM // 2) * Q_SCALE
        _store_by_class((qa0_ref, qa1_ref, qa2_ref), c, roped, stage_ref)

    acc = jnp.dot(h, wka_ref[...], preferred_element_type=F32)
    for c in range(A_WIDTH // LANES):
        roped = _rope_chunk(acc[:, c * LANES:(c + 1) * LANES], tab_ref, 0, A_ROT_DIM // 2)
        _store_by_class((ka0_ref, ka1_ref, ka2_ref), c, roped, stage_ref)

    acc = jnp.dot(h, wva_ref[...], preferred_element_type=F32)
    for c in range(A_WIDTH // LANES):
        _store_by_class((va0_ref, va1_ref, va2_ref), c, acc[:, c * LANES:(c + 1) * LANES], stage_ref)

    acc = jnp.dot(h, wqb_ref[...], preferred_element_type=F32)
    acc = acc * lax.rsqrt(_head_mean_sq(acc, blkq_ref) + EPS) * gq_ref[...]
    heads_per_chunk = LANES // HEAD_DIM
    for c in range(B_Q_WIDTH // LANES):
        roped = _rope_chunk(acc[:, c * LANES:(c + 1) * LANES], tab_ref, 3 * LANES, HEAD_DIM // 4) * Q_SCALE
        rt = roped.T.astype(BF16)
        for hh in range(heads_per_chunk):
            head = c * heads_per_chunk + hh
            g, hg = divmod(head, B_GROUP_HEADS)
            cols = slice(hg * tm, (hg + 1) * tm)
            qt_ref[g, g * HEAD_DIM:(g + 1) * HEAD_DIM, cols] = rt[hh * HEAD_DIM:(hh + 1) * HEAD_DIM, :]
            qt_ref[g, (1 - g) * HEAD_DIM:(2 - g) * HEAD_DIM, cols] = jnp.zeros((HEAD_DIM, tm), BF16)

    acc = jnp.dot(h, wkb_ref[...], preferred_element_type=F32)
    acc = acc * lax.rsqrt(_head_mean_sq(acc, blkk_ref) + EPS) * gk_ref[...]
    kb_ref[...] = _rope_chunk(acc, tab_ref, 3 * LANES, HEAD_DIM // 4).astype(BF16)

    vt = jnp.dot(h, wvb_ref[...], preferred_element_type=F32).T.astype(BF16)
    for g in range(B_KV_HEADS):
        vt_ref[g] = vt[g * HEAD_DIM:(g + 1) * HEAD_DIM, :]

    gates_ref[...] = jax.nn.sigmoid(jnp.dot(h, wg_ref[...], preferred_element_type=F32)).astype(BF16)


def _in_proj(xf, tab, b, s, gmix, w_in, gq, gk):
    n = xf.shape[0]
    tm = ROW_TILE
    nts = s // tm
    c1, c2, c3 = A_WIDTH, 2 * A_WIDTH, 3 * A_WIDTH
    c4 = c3 + B_Q_WIDTH
    c5 = c4 + B_KV_WIDTH
    c6 = c5 + B_KV_WIDTH
    wb = w_in.astype(BF16)
    ws = [wb[:, :c1], wb[:, c1:c2], wb[:, c2:c3], wb[:, c3:c4], wb[:, c4:c5], wb[:, c5:c6], wb[:, c6:]]

    def head_blockdiag(width):
        hid = jnp.arange(width) // HEAD_DIM
        return jnp.where(hid[:, None] == hid[None, :], 1.0 / HEAD_DIM, 0.0).astype(BF16)

    consts = [gmix.reshape(1, D_MODEL)] + ws + [
        jnp.tile(gq, B_Q_WIDTH // HEAD_DIM).reshape(1, B_Q_WIDTH),
        jnp.tile(gk, B_KV_WIDTH // HEAD_DIM).reshape(1, B_KV_WIDTH),
        head_blockdiag(B_Q_WIDTH), head_blockdiag(B_KV_WIDTH)]

    a_shapes, a_specs = [], []
    for _ in range(3):
        for dil in A_DILATIONS:
            a_shapes.append(jax.ShapeDtypeStruct((b, dil, s // dil, A_GROUP_WIDTH), BF16))
            a_specs.append(pl.BlockSpec((None, dil, tm // dil, A_GROUP_WIDTH),
                                        lambda i: (i // nts, 0, i % nts, 0)))
    kpb = B_TK // tm
    b_shapes = [jax.ShapeDtypeStruct((b, B_KV_HEADS, nts, B_KV_WIDTH, B_GROUP_HEADS * tm), BF16),
                jax.ShapeDtypeStruct((n, B_KV_WIDTH), BF16),
                jax.ShapeDtypeStruct((b, B_KV_HEADS, s // B_TK, HEAD_DIM, B_TK), BF16),
                jax.ShapeDtypeStruct((n, GATE_WIDTH), BF16)]
    b_specs = [pl.BlockSpec((None, B_KV_HEADS, None, B_KV_WIDTH, B_GROUP_HEADS * tm),
                            lambda i: (i // nts, 0, i % nts, 0, 0)),
               pl.BlockSpec((tm, B_KV_WIDTH), lambda i: (i, 0)),
               pl.BlockSpec((None, B_KV_HEADS, None, HEAD_DIM, tm),
                            lambda i: (i // nts, 0, (i % nts) // kpb, 0, i % kpb)),
               pl.BlockSpec((tm, GATE_WIDTH), lambda i: (i, 0))]

    return pl.pallas_call(
        _in_proj_kernel,
        out_shape=a_shapes + b_shapes,
        grid=(n // tm,),
        in_specs=[pl.BlockSpec((tm, D_MODEL), lambda i: (i, 0)),
                  pl.BlockSpec((tm, tab.shape[1]), lambda i: (i % nts, 0))] + [_full_spec(a) for a in consts],
        out_specs=a_specs + b_specs,
        scratch_shapes=[pltpu.VMEM((tm, LANES), F32)],
        compiler_params=_cparams(("parallel",)),
        name="in_proj",
    )(xf, tab, *consts)


def _band_bias():
    qi = jnp.arange(A_SUB)[:, None]
    kj = jnp.arange(A_SUB + 2 * A_RADIUS)[None, :]
    band = jnp.abs(kj - A_RADIUS - qi) <= A_RADIUS
    first = band & (kj >= A_RADIUS)
    last = band & (kj < A_SUB + A_RADIUS)
    return jnp.where(jnp.stack([band, first, last]), 0.0, NEG_INF).astype(F32)


def _attn_a_kernel(bias_ref, q_ref, kp_ref, kc_ref, kn_ref, vp_ref, vc_ref, vn_ref, o_ref, lse_ref):
    tq = q_ref.shape[0]
    nsub = tq // A_SUB
    i = pl.program_id(2)
    last_tile = pl.num_programs(2) - 1
    first_head = lax.broadcasted_iota(jnp.int32, (1, LANES), 1) < HEAD_DIM
    nt_dims = (((1,), (1,)), ((), ()))

    def window(p_ref, c_ref, n_ref, u, cols):
        lo, hi = u * A_SUB - A_RADIUS, (u + 1) * A_SUB + A_RADIUS
        parts = [p_ref[:, cols]] if lo < 0 else []
        parts.append(c_ref[max(lo, 0):min(hi, tq), cols])
        if hi > tq:
            parts.append(n_ref[:, cols])
        return parts[0] if len(parts) == 1 else jnp.concatenate(parts, axis=0)

    for u in range(nsub):
        if u == 0:
            bias = bias_ref[jnp.where(i == 0, 1, 0)]
        elif u == nsub - 1:
            bias = bias_ref[jnp.where(i == last_tile, 2, 0)]
        else:
            bias = bias_ref[0]
        rows = slice(u * A_SUB, (u + 1) * A_SUB)
        for pair in range(A_GROUP_WIDTH // LANES):
            cols = slice(pair * LANES, (pair + 1) * LANES)
            k = window(kp_ref, kc_ref, kn_ref, u, cols)
            v = window(vp_ref, vc_ref, vn_ref, u, cols)
            q = q_ref[rows, cols]
            outs, lses = [], []
            for hh in range(LANES // HEAD_DIM):
                mine = first_head if hh == 0 else jnp.logical_not(first_head)
                qm = jnp.where(mine, q, jnp.zeros_like(q))
                sc = lax.dot_general(qm, k, nt_dims, preferred_element_type=F32) + bias
                m = jnp.max(sc, axis=1, keepdims=True)
                p = jnp.exp2(sc - m)
                den = jnp.sum(p, axis=1, keepdims=True)
                outs.append(jnp.dot(p.astype(BF16), v, preferred_element_type=F32) / den)
                lses.append(m * LN2 + jnp.log(den))
            o_ref[rows, cols] = jnp.where(first_head, outs[0], outs[1])
            lse_ref[rows, cols] = jnp.where(first_head, lses[0], lses[1])


def _attn_a(q, k, v, gi):
    b, dil, cl, _ = q.shape
    tq = min(A_TQ, cl)
    assert tq >= 2 * A_SUB and A_SUB == 2 * A_RADIUS
    hb = tq // A_RADIUS
    n_halo = cl // A_RADIUS
    bias = _band_bias()
    main = pl.BlockSpec((None, None, tq, A_GROUP_WIDTH), lambda bi, r, i: (bi, r, i, 0))
    prev = pl.BlockSpec((None, None, A_RADIUS, A_GROUP_WIDTH),
                        lambda bi, r, i: (bi, r, jnp.maximum(i * hb - 1, 0), 0))
    nxt = pl.BlockSpec((None, None, A_RADIUS, A_GROUP_WIDTH),
                       lambda bi, r, i: (bi, r, jnp.minimum((i + 1) * hb, n_halo - 1), 0))
    return pl.pallas_call(
        _attn_a_kernel,
        out_shape=[jax.ShapeDtypeStruct((b, dil, cl, A_GROUP_WIDTH), F32)] * 2,
        grid=(b, dil, cl // tq),
        in_specs=[_full_spec(bias), main, prev, main, nxt, prev, main, nxt],
        out_specs=[main, main],
        compiler_params=_cparams(("parallel", "parallel", "parallel")),
        name=f"attn_a{gi}",
    )(bias, q, k, k, k, v, v, v)


def _flash_b_kernel(shift_ref, qt_ref, k_ref, vt_ref, o_ref, acc_sc, l_sc, m_sc, p0_sc, p1_sc, *, online):
    nk, tk, _ = k_ref.shape
    width = qt_ref.shape[1]
    tq = width // B_GROUP_HEADS
    acc_sc[...] = jnp.zeros(acc_sc.shape, F32)
    l_sc[...] = jnp.zeros(l_sc.shape, F32)
    shift = shift_ref[0]

    def scores(j):
        return jnp.dot(k_ref[j], qt_ref[...], preferred_element_type=F32)

    def column_sums(p):
        return p.reshape(tk // SUBLANES, SUBLANES, width).sum(axis=0)

    def accumulate(j, pb):
        vt = vt_ref[j]
        for h in range(B_GROUP_HEADS):
            cols = slice(h * tq, (h + 1) * tq)
            acc_sc[:, cols] += jnp.dot(vt, pb[:, cols], preferred_element_type=F32)

    if online:
        m_sc[...] = jnp.full(m_sc.shape, NEG_INF, F32)

        def body(j, carry):
            sc = scores(j)
            m_old = m_sc[...]
            m_new = jnp.maximum(m_old, jnp.max(sc, axis=0, keepdims=True))
            alpha = jnp.exp2(m_old - m_new)
            m_sc[...] = m_new
            p = jnp.exp2(sc - m_new)
            l_sc[...] = alpha * l_sc[...] + column_sums(p)
            acc_sc[...] = alpha * acc_sc[...]
            accumulate(j, p.astype(BF16))
            return carry

        lax.fori_loop(0, nk, body, 0)
    else:
        def exponentials(sc, p_out):
            p = jnp.exp2(sc - shift)
            l_sc[...] += column_sums(p)
            p_out[...] = p.astype(BF16)

        def stage(j, p_in, p_out):
            sc = scores(j)
            accumulate(j - 1, p_in[...])
            exponentials(sc, p_out)

        exponentials(scores(0), p0_sc)

        def body(jj, carry):
            stage(2 * jj + 1, p0_sc, p1_sc)
            stage(2 * jj + 2, p1_sc, p0_sc)
            return carry

        lax.fori_loop(0, nk // 2 - 1, body, 0)
        stage(nk - 1, p0_sc, p1_sc)
        accumulate(nk - 1, p1_sc[...])
    o = acc_sc[...] / jnp.sum(l_sc[...], axis=0, keepdims=True)
    o = jnp.concatenate([o[:, h * tq:(h + 1) * tq] for h in range(B_GROUP_HEADS)], axis=0)
    o_ref[...] = o.T.astype(o_ref.dtype)


def _flash_b(shift, qt, kb, vt, online):
    b, _, nq, _, width = qt.shape
    tq = width // B_GROUP_HEADS
    _, nk, tk, _ = kb.shape
    return pl.pallas_call(
        functools.partial(_flash_b_kernel, online=online),
        out_shape=jax.ShapeDtypeStruct((b, B_KV_HEADS, nq * tq, B_GROUP_WIDTH), BF16),
        grid=(b, B_KV_HEADS, nq),
        in_specs=[pl.BlockSpec(memory_space=pltpu.SMEM),
                  pl.BlockSpec((None, None, None, B_KV_WIDTH, width), lambda bi, g, i: (bi, g, i, 0, 0)),
                  pl.BlockSpec((None, nk, tk, B_KV_WIDTH), lambda bi, g, i: (bi, 0, 0, 0)),
                  pl.BlockSpec((None, None, nk, HEAD_DIM, tk), lambda bi, g, i: (bi, g, 0, 0, 0))],
        out_specs=pl.BlockSpec((None, None, tq, B_GROUP_WIDTH), lambda bi, g, i: (bi, g, i, 0)),
        scratch_shapes=[pltpu.VMEM((HEAD_DIM, width), F32), pltpu.VMEM((SUBLANES, width), F32),
                        pltpu.VMEM((1, width), F32),
                        pltpu.VMEM((tk, width), BF16), pltpu.VMEM((tk, width), BF16)],
        compiler_params=_cparams(("parallel", "parallel", "parallel")),
        name="flash_b_online" if online else "flash_b",
    )(shift, qt, kb, vt)


def _stage_classes(blk_ref, stage_ref):
    dil, rows, width = blk_ref.shape
    if dil == 1:
        return
    for c in range(width // LANES):
        for r in range(dil):
            stage_ref.at[c][pl.ds(r, rows, stride=dil), :] = blk_ref[r, :, c * LANES:(c + 1) * LANES]


def _token_rows(blk_ref, stage_ref, rows):
    if blk_ref.shape[0] == 1:
        return blk_ref[0, rows, :]
    return jnp.concatenate([stage_ref[c, rows, :] for c in range(stage_ref.shape[0])], axis=1)


def _post_attn_kernel(x_ref, o0_ref, o1_ref, o2_ref, l0_ref, l1_ref, l2_ref, ob0_ref, ob1_ref, gates_ref,
                      wa_ref, wb0_ref, wb1_ref, wout_ref, gffn_ref, wrh_ref, wrl_ref,
                      x1_ref, h2_ref, aff_ref, so1, so2, sl1, sl2):
    tm = x_ref.shape[0]
    for blk, stage in ((l1_ref, sl1), (l2_ref, sl2), (o1_ref, so1), (o2_ref, so2)):
        _stage_classes(blk, stage)
    nt = (((1,), (1,)), ((), ()))
    hr = tm // POST_SPLIT
    for half in range(POST_SPLIT):
        rows = slice(half * hr, (half + 1) * hr)
        l0 = _token_rows(l0_ref, None, rows)
        l1 = _token_rows(l1_ref, sl1, rows)
        l2 = _token_rows(l2_ref, sl2, rows)
        m = jnp.maximum(jnp.maximum(l0, l1), l2)
        e0, e1, e2 = jnp.exp(l0 - m), jnp.exp(l1 - m), jnp.exp(l2 - m)
        oa = (e0 * _token_rows(o0_ref, None, rows) + e1 * _token_rows(o1_ref, so1, rows)
              + e2 * _token_rows(o2_ref, so2, rows)) / (e0 + e1 + e2)
        ya = jnp.dot(oa.astype(BF16), wa_ref[...], preferred_element_type=F32)
        yb = (jnp.dot(ob0_ref[rows, :], wb0_ref[...], preferred_element_type=F32)
              + jnp.dot(ob1_ref[rows, :], wb1_ref[...], preferred_element_type=F32))
        ga = gates_ref[rows, :D_MODEL].astype(F32)
        gb = gates_ref[rows, D_MODEL:].astype(F32)
        z = (ga * ya + gb * yb).astype(BF16)
        x1 = x_ref[rows, :] + jnp.dot(z, wout_ref[...], preferred_element_type=F32)
        x1_ref[rows, :] = x1
        ms = jnp.mean(x1 * x1, axis=-1, keepdims=True)
        h2 = x1 * lax.rsqrt(ms + EPS) * gffn_ref[...]
        hi = h2.astype(BF16)
        lo = (h2 - hi.astype(F32)).astype(BF16)
        h2_ref[rows, :] = hi
        wrh = wrh_ref[...]
        logits = (lax.dot_general(wrh, hi, nt, preferred_element_type=F32)
                  + lax.dot_general(wrh, lo, nt, preferred_element_type=F32)
                  + lax.dot_general(wrl_ref[...], hi, nt, preferred_element_type=F32))
        logits = logits - jnp.max(logits, axis=0, keepdims=True)
        e = jnp.exp(logits)
        aff = e / jnp.sum(e, axis=0, keepdims=True)
        for j in range(hr // LANES):
            aff_ref[half * (hr // LANES) + j] = aff[:, j * LANES:(j + 1) * LANES]


def _post_attn(xf, oas, lses, ob, gates, b, s, w_a, w_b, w_out, g_ffn, w_router):
    n = xf.shape[0]
    tm = ROW_TILE
    nts = s // tm
    wa = w_a.astype(BF16)
    wb = w_b.astype(BF16)
    wb0, wb1 = wb[:B_GROUP_WIDTH], wb[B_GROUP_WIDTH:]
    wout = w_out.astype(BF16)
    wrt = w_router.T
    wrh = wrt.astype(BF16)
    wrl = (wrt - wrh.astype(F32)).astype(BF16)
    consts = [wa, wb0, wb1, wout, g_ffn.reshape(1, D_MODEL), wrh, wrl]

    def rows(w):
        return pl.BlockSpec((tm, w), lambda i: (i, 0))

    def class_spec(dil):
        return pl.BlockSpec((None, dil, tm // dil, A_GROUP_WIDTH), lambda i: (i // nts, 0, i % nts, 0))

    def ob_spec(g):
        return pl.BlockSpec((None, None, tm, B_GROUP_WIDTH), lambda i: (i // nts, g, i % nts, 0))

    a_specs = [class_spec(d) for d in A_DILATIONS]
    return pl.pallas_call(
        _post_attn_kernel,
        out_shape=[jax.ShapeDtypeStruct((n, D_MODEL), F32),
                   jax.ShapeDtypeStruct((n, D_MODEL), BF16),
                   jax.ShapeDtypeStruct((n // LANES, N_EXPERTS, LANES), F32)],
        grid=(n // tm,),
        in_specs=[rows(D_MODEL)] + a_specs + a_specs + [ob_spec(0), ob_spec(1), rows(GATE_WIDTH)]
                 + [_full_spec(a) for a in consts],
        out_specs=[rows(D_MODEL), rows(D_MODEL),
                   pl.BlockSpec((tm // LANES, N_EXPERTS, LANES), lambda i: (i, 0, 0))],
        scratch_shapes=[pltpu.VMEM((A_GROUP_WIDTH // LANES, tm, LANES), F32)] * 4,
        compiler_params=_cparams(("parallel",)),
        name="post_attn",
    )(xf, *oas, *lses, ob, ob, gates, *consts)


def _route_kernel(aff_ref, tri_ref, posl_ref, starts_ref, sel_sc, *, cap, idx_bits, tiles_per_block):
    nlt = aff_ref.shape[0]
    capf = float(cap)

    def count(mask):
        part = jnp.sum(jnp.where(mask, 1.0, 0.0), axis=0)
        return jnp.sum(part, axis=1, keepdims=True)[None]

    def value_body(i, tau):
        bits = pltpu.bitcast(aff_ref[...], jnp.int32)
        cand = tau | lax.shift_left(jnp.int32(1), 30 - i)
        return jnp.where(count(bits >= cand) >= capf, cand, tau)

    tau = lax.fori_loop(0, 31, value_body, jnp.zeros((1, N_EXPERTS, 1), jnp.int32))
    bits = pltpu.bitcast(aff_ref[...], jnp.int32)
    need = capf - count(bits > tau)
    tok = (lax.broadcasted_iota(jnp.int32, bits.shape, 0) * LANES
           + lax.broadcasted_iota(jnp.int32, bits.shape, 2))

    def index_body(i, last):
        b = pltpu.bitcast(aff_ref[...], jnp.int32)
        cand = last | lax.shift_left(jnp.int32(1), idx_bits - 1 - i)
        return jnp.where(count((b == tau) & (tok < cand)) < need, cand, last)

    last = lax.fori_loop(0, idx_bits, index_body, jnp.zeros((1, N_EXPERTS, 1), jnp.int32))
    sel_sc[...] = jnp.where((bits > tau) | ((bits == tau) & (tok <= last)), 1.0, 0.0)

    tri = tri_ref[...]
    ones = jnp.ones((LANES, LANES), BF16)

    def prefix_body(c, before):
        @pl.when(c % tiles_per_block == 0)
        def _():
            starts_ref[c // tiles_per_block] = before.astype(jnp.int32)

        s = sel_sc[c]
        sb = s.astype(BF16)
        pos = before + jnp.dot(sb, tri, preferred_element_type=F32) - s
        posl_ref[c] = jnp.where(s > 0, pos, -1.0).astype(jnp.int32)
        return before + jnp.dot(sb, ones, preferred_element_type=F32)

    lax.fori_loop(0, nlt, prefix_body, jnp.zeros((N_EXPERTS, LANES), F32))


def _route(aff3, cap):
    nlt = aff3.shape[0]
    n = nlt * LANES
    tpb = MOE_T // LANES
    tri = (jnp.arange(LANES)[:, None] <= jnp.arange(LANES)[None, :]).astype(BF16)
    vm = pl.BlockSpec(memory_space=pltpu.VMEM)
    return pl.pallas_call(
        functools.partial(_route_kernel, cap=cap, idx_bits=(n - 1).bit_length(), tiles_per_block=tpb),
        out_shape=[jax.ShapeDtypeStruct((nlt, N_EXPERTS, LANES), jnp.int32),
                   jax.ShapeDtypeStruct((n // MOE_T, N_EXPERTS, LANES), jnp.int32)],
        in_specs=[vm, vm],
        out_specs=[vm, vm],
        scratch_shapes=[pltpu.VMEM((nlt, N_EXPERTS, LANES), F32)],
        compiler_params=pltpu.CompilerParams(vmem_limit_bytes=VMEM_LIMIT),
        name="route",
    )(aff3, tri)


def _block_slots(s_ref, i, nt):
    return [(s_ref[e * (nt + 1) + i], s_ref[e * (nt + 1) + i + 1]) for e in range(N_EXPERTS)]


def _align_down(x):
    return (x >> MOE_ALIGN_SHIFT) << MOE_ALIGN_SHIFT


def _moe_gather_kernel(s_ref, aff_ref, posl_ref, h2_ref, xs_hbm, tm_ref,
                       haug_sc, pall_sc, res_sc, stage_sc, carry_sc, pending_sc, sem, *, nt, cap):
    i = pl.program_id(0)
    nj = posl_ref.shape[0]
    ws = MOE_WS

    @pl.when(i == 0)
    def _():
        carry_sc[...] = jnp.zeros(carry_sc.shape, F32)
        pending_sc[0] = 0
        pad = xs_hbm.shape[1] - cap
        stage_sc[...] = jnp.zeros(stage_sc.shape, BF16)
        fills = [pltpu.make_async_copy(stage_sc.at[pl.ds(0, pad)], xs_hbm.at[e, pl.ds(cap, pad)], sem)
                 for e in range(N_EXPERTS)]
        for cp in fills:
            cp.start()
        for cp in fills:
            cp.wait()

    for j in range(nj):
        pj = posl_ref[j]
        gj = jnp.where(pj >= 0, aff_ref[j], 0.0)
        blk = jnp.concatenate([gj, pj.astype(F32), gj, jnp.zeros((LANES - 3 * N_EXPERTS, LANES), F32)], axis=0)
        tm_ref[j * LANES:(j + 1) * LANES, :] = blk.T
    tmv = tm_ref[...]
    lane = lax.broadcasted_iota(jnp.int32, tmv.shape, 1)
    ghi = tmv.astype(BF16)
    glo = (tmv - ghi.astype(F32)).astype(BF16)
    zero = jnp.zeros_like(ghi)
    haug_sc[:, :D_MODEL] = h2_ref[...]
    haug_sc[:, D_MODEL:] = jnp.where(lane < N_EXPERTS, ghi,
                                     jnp.where((lane >= TM_GATE_LO) & (lane < TM_GATE_LO + N_EXPERTS), glo, zero))

    astarts, kks, offcs = [], [], []
    nw = jnp.int32(0)
    for st, en in _block_slots(s_ref, i, nt):
        ast = _align_down(st)
        span = _align_down(en) - ast
        kk = span >> MOE_WS_SHIFT
        astarts.append(ast)
        kks.append(kk)
        offcs.append(span - kk * ws)
        nw = jnp.maximum(nw, kk + 1)

    def wait_window():
        for _ in range(N_EXPERTS):
            pltpu.make_async_copy(stage_sc.at[pl.ds(0, ws)], xs_hbm.at[0, pl.ds(0, ws)], sem).wait()

    row = lax.broadcasted_iota(jnp.int32, (ws, LANES), 0)

    def window(k, carry):
        for e in range(N_EXPERTS):
            base = astarts[e] + k * ws
            for j in range(nj):
                hit = (posl_ref[j, e:e + 1, :] - base) == row
                pall_sc[e * ws:(e + 1) * ws, j * LANES:(j + 1) * LANES] = jnp.where(hit, 1.0, 0.0).astype(BF16)
        res_sc[...] = jnp.dot(pall_sc[...], haug_sc[...], preferred_element_type=F32)

        @pl.when(k == 0)
        def _():
            for e in range(N_EXPERTS):
                res_sc[e * ws:e * ws + MOE_ALIGN, :] += carry_sc[e]

        for e in range(N_EXPERTS):
            @pl.when(k == kks[e])
            def _():
                carry_sc[e] = res_sc[pl.ds(pl.multiple_of(e * ws + offcs[e], MOE_ALIGN), MOE_ALIGN), :]

        @pl.when(pending_sc[0] == 1)
        def _():
            wait_window()

        stage_sc[...] = res_sc[...].astype(BF16)
        for e in range(N_EXPERTS):
            base = pl.multiple_of(astarts[e] + k * ws, MOE_ALIGN)
            pltpu.make_async_copy(stage_sc.at[pl.ds(e * ws, ws)], xs_hbm.at[e, pl.ds(base, ws)], sem).start()
        pending_sc[0] = 1
        return carry

    lax.fori_loop(0, nw, window, 0)

    @pl.when((i == nt - 1) & (pending_sc[0] == 1))
    def _():
        wait_window()
        pending_sc[0] = 0


def _moe_gather(s_flat, aff3, posl, h2, cap):
    n = h2.shape[0]
    t = MOE_T
    nt = n // t
    cap_pad = cap + (t // MOE_WS + 1) * MOE_WS
    blk3 = pl.BlockSpec((t // LANES, N_EXPERTS, LANES), lambda i, s: (i, 0, 0))
    grid_spec = pltpu.PrefetchScalarGridSpec(
        num_scalar_prefetch=1,
        grid=(nt,),
        in_specs=[blk3, blk3, pl.BlockSpec((t, D_MODEL), lambda i, s: (i, 0))],
        out_specs=[pl.BlockSpec(memory_space=pl.ANY), pl.BlockSpec((t, LANES), lambda i, s: (i, 0))],
        scratch_shapes=[pltpu.VMEM((t, XS_WIDTH), BF16),
                        pltpu.VMEM((N_EXPERTS * MOE_WS, t), BF16),
                        pltpu.VMEM((N_EXPERTS * MOE_WS, XS_WIDTH), F32),
                        pltpu.VMEM((N_EXPERTS * MOE_WS, XS_WIDTH), BF16),
                        pltpu.VMEM((N_EXPERTS, MOE_ALIGN, XS_WIDTH), F32),
                        pltpu.SMEM((1,), jnp.int32),
                        pltpu.SemaphoreType.DMA(())])
    return pl.pallas_call(
        functools.partial(_moe_gather_kernel, nt=nt, cap=cap),
        out_shape=[jax.ShapeDtypeStruct((N_EXPERTS, cap_pad, XS_WIDTH), BF16),
                   jax.ShapeDtypeStruct((n, LANES), F32)],
        grid_spec=grid_spec,
        compiler_params=_cparams(("arbitrary",)),
        name="moe_gather",
    )(s_flat, aff3, posl, h2)


def _moe_ffn_kernel(xs_ref, w1_ref, w3_ref, w2_ref, ye_ref):
    xe = xs_ref[:, :D_MODEL]
    route = xs_ref[:, D_MODEL:].astype(F32)
    lane = lax.broadcasted_iota(jnp.int32, route.shape, 1)
    e = pl.program_id(0)
    gate = jnp.sum(jnp.where((lane == e) | (lane == e + TM_GATE_LO), route, 0.0), axis=1, keepdims=True)
    a = jnp.dot(xe, w1_ref[...], preferred_element_type=F32)
    u = jnp.dot(xe, w3_ref[...], preferred_element_type=F32)
    hid = (a * jax.nn.sigmoid(a) * u).astype(BF16)
    ye = jnp.dot(hid, w2_ref[...], preferred_element_type=F32) * gate
    ye_ref[...] = ye.astype(ye_ref.dtype)


def _moe_ffn(xs, cap, w1, w3, w2):
    e = xs.shape[0]
    tc = min(FFN_TC, cap)
    wspec = pl.BlockSpec((None, D_MODEL, D_MODEL), lambda ei, ci: (ei, 0, 0))
    return pl.pallas_call(
        _moe_ffn_kernel,
        out_shape=jax.ShapeDtypeStruct((e, cap, D_MODEL), BF16),
        grid=(e, cap // tc),
        in_specs=[pl.BlockSpec((None, tc, XS_WIDTH), lambda ei, ci: (ei, ci, 0)), wspec, wspec, wspec],
        out_specs=pl.BlockSpec((None, tc, D_MODEL), lambda ei, ci: (ei, ci, 0)),
        compiler_params=_cparams(("parallel", "arbitrary")),
        name="moe_ffn",
    )(xs, w1, w3, w2)


def _final_kernel(s_ref, x1_ref, tm_ref, p_ref, ye_hbm, wple_ref, wpg_ref, gple_ref, gfin_ref, expand_ref, out_ref,
                  buf_sc, pall_sc, y_sc, sem, *, nt, cap):
    i = pl.program_id(0)
    ws = MOE_WS
    t = x1_ref.shape[0]
    slot = i % 2

    def window_rows(blk, k):
        out = []
        for st, _ in _block_slots(s_ref, blk, nt):
            lo = _align_down(st) + k * ws
            out.append((lo, pl.multiple_of(jnp.minimum(lo, cap - ws), MOE_ALIGN)))
        return out

    def copies(rows, buf, which):
        return [pltpu.make_async_copy(ye_hbm.at[e, pl.ds(base, ws)], buf_sc.at[buf, pl.ds(e * ws, ws)], sem.at[which])
                for e, (_, base) in enumerate(rows)]

    @pl.when(i == 0)
    def _():
        for cp in copies(window_rows(0, 0), 0, 0):
            cp.start()

    @pl.when(i + 1 < nt)
    def _():
        for cp in copies(window_rows(i + 1, 0), 1 - slot, 1 - slot):
            cp.start()

    nw = jnp.int32(0)
    for st, en in _block_slots(s_ref, i, nt):
        nw = jnp.maximum(nw, (en - _align_down(st) + ws - 1) >> MOE_WS_SHIFT)
    lane = lax.broadcasted_iota(jnp.int32, (1, LANES), 1)
    col = (lax.broadcasted_iota(jnp.int32, (1, N_EXPERTS * ws), 1) & (ws - 1)).astype(F32)

    def onehot(rows):
        basev = jnp.zeros((1, LANES), F32)
        lov = jnp.full((1, LANES), float(2 ** 30), F32)
        for e, (lo, base) in enumerate(rows):
            basev = jnp.where(lane == TM_POS + e, base.astype(F32), basev)
            lov = jnp.where(lane == TM_POS + e, lo.astype(F32), lov)
        tmv = tm_ref[...]
        off = tmv - basev
        off = jnp.where((tmv >= lov) & (off >= 0.0) & (off < float(ws)), off, -1.0)
        spread = jnp.dot(off.astype(BF16), expand_ref[...], preferred_element_type=F32)
        pall_sc[...] = jnp.where(spread == col, 1.0, 0.0).astype(BF16)

    rows0 = window_rows(i, 0)
    onehot(rows0)
    for cp in copies(rows0, slot, slot):
        cp.wait()
    y_sc[...] = jnp.dot(pall_sc[...], buf_sc[slot], preferred_element_type=F32)

    def window(k, carry):
        rows = window_rows(i, k)
        cps = copies(rows, slot, slot)
        for cp in cps:
            cp.start()
        onehot(rows)
        for cp in cps:
            cp.wait()
        y_sc[...] += jnp.dot(pall_sc[...], buf_sc[slot], preferred_element_type=F32)
        return carry

    lax.fori_loop(1, nw, window, 0)

    hr = t // POST_SPLIT
    for half in range(POST_SPLIT):
        rows = slice(half * hr, (half + 1) * hr)
        x2 = x1_ref[rows, :] + y_sc[rows, :]
        ms = jnp.mean(x2 * x2, axis=-1, keepdims=True)
        hp = (x2 * lax.rsqrt(ms + EPS) * gple_ref[...]).astype(BF16)
        gt = jax.nn.sigmoid(jnp.dot(hp, wpg_ref[...], preferred_element_type=F32))
        emb = jnp.dot(p_ref[rows, :].astype(BF16), wple_ref[...], preferred_element_type=F32)
        x3 = x2 + emb * gt
        ms3 = jnp.mean(x3 * x3, axis=-1, keepdims=True)
        out_ref[rows, :] = x3 * lax.rsqrt(ms3 + EPS) * gfin_ref[...]


def _final(s_flat, x1, tm, pf, ye, w_ple, w_ple_gate, g_ple, g_final):
    n = x1.shape[0]
    t = MOE_T
    nt = n // t
    cap = ye.shape[1]
    src = jnp.arange(LANES)[:, None] - TM_POS
    dst = jnp.arange(N_EXPERTS * MOE_WS)[None, :] >> MOE_WS_SHIFT
    expand = (src == dst).astype(BF16)
    consts = [w_ple.astype(BF16), w_ple_gate.astype(BF16), g_ple.reshape(1, D_MODEL), g_final.reshape(1, D_MODEL),
              expand]

    def rows(w):
        return pl.BlockSpec((t, w), lambda i, s: (i, 0))

    grid_spec = pltpu.PrefetchScalarGridSpec(
        num_scalar_prefetch=1,
        grid=(nt,),
        in_specs=[rows(D_MODEL), rows(LANES), rows(PLE_DIM), pl.BlockSpec(memory_space=pl.ANY)]
                 + [_full_spec(a) for a in consts],
        out_specs=rows(D_MODEL),
        scratch_shapes=[pltpu.VMEM((2, N_EXPERTS * MOE_WS, D_MODEL), BF16),
                        pltpu.VMEM((t, N_EXPERTS * MOE_WS), BF16),
                        pltpu.VMEM((t, D_MODEL), F32),
                        pltpu.SemaphoreType.DMA((2,))])
    return pl.pallas_call(
        functools.partial(_final_kernel, nt=nt, cap=cap),
        out_shape=jax.ShapeDtypeStruct((n, D_MODEL), F32),
        grid_spec=grid_spec,
        compiler_params=_cparams(("arbitrary",)),
        name="final",
    )(s_flat, x1, tm, pf, ye, *consts)


def _run_group(x, p, g_mix, w_in, g_qn, g_kn, w_a_proj, w_b_proj, w_out, g_ffn, w_router,
               w1, w3, w2, g_ple, w_ple, w_ple_gate, g_final):
    b, s, _ = x.shape
    n = b * s
    assert s % B_TK == 0 and s % (A_DILATIONS[-1] * A_RADIUS) == 0
    xf = x.reshape(n, D_MODEL)
    tab = _rope_tables(s)
    outs = _in_proj(xf, tab, b, s, g_mix, w_in, g_qn, g_kn)
    qas, kas, vas = outs[0:3], outs[3:6], outs[6:9]
    qt, kb, vt, gates = outs[9:]

    oas, lses = [], []
    for gi, (window, dil) in enumerate(A_PATTERNS):
        assert window // (2 * dil) == A_RADIUS
        o, lse = _attn_a(qas[gi], kas[gi], vas[gi], gi)
        oas.append(o)
        lses.append(lse)

    shift = (HEAD_DIM * Q_SCALE * 1.02) * jnp.max(jnp.abs(g_qn)) * jnp.max(jnp.abs(g_kn))
    shift = shift.reshape(1).astype(F32)
    kb4 = kb.reshape(b, s // B_TK, B_TK, B_KV_WIDTH)
    ob = lax.cond(shift[0] <= B_FIXED_SHIFT_MAX,
                  lambda: _flash_b(shift, qt, kb4, vt, online=False),
                  lambda: _flash_b(shift, qt, kb4, vt, online=True))

    x1, h2, aff3 = _post_attn(xf, oas, lses, ob, gates, b, s, w_a_proj, w_b_proj, w_out, g_ffn, w_router)

    cap = CAPACITY_FACTOR * n // N_EXPERTS
    assert cap % MOE_WS == 0 and cap >= MOE_WS
    posl, starts = _route(aff3, cap)
    s_flat = jnp.concatenate([starts[:, :, 0].T, jnp.full((N_EXPERTS, 1), cap, jnp.int32)], axis=1).reshape(-1)
    xs, tm = _moe_gather(s_flat, aff3, posl, h2, cap)
    ye = _moe_ffn(xs, cap, w1, w3, w2)
    out = _final(s_flat, x1, tm, p.reshape(n, PLE_DIM), ye, w_ple, w_ple_gate, g_ple, g_final)
    return out.reshape(b, s, D_MODEL)


def kernel(x_prompt, x_sample, p_prompt, p_sample, g_mix, w_in, g_qn, g_kn, w_a_proj, w_b_proj, w_out, g_ffn,
           w_router, w_exp_gate, w_exp_up, w_exp_down, g_ple, w_ple, w_ple_gate, g_final):
    assert g_mix.shape[0] == 1, "single layer"
    w1 = w_exp_gate[0].astype(BF16)
    w3 = w_exp_up[0].astype(BF16)
    w2 = w_exp_down[0].astype(BF16)
    args = (g_mix[0], w_in[0], g_qn[0], g_kn[0], w_a_proj[0], w_b_proj[0], w_out[0], g_ffn[0], w_router[0],
            w1, w3, w2, g_ple[0], w_ple[0], w_ple_gate[0], g_final)
    y_prompt = _run_group(x_prompt, p_prompt[0], *args)
    y_sample = _run_group(x_sample, p_sample[0], *args)
    return (y_prompt, y_sample)
```

```python
import functools
import math

import jax
import jax.numpy as jnp
from jax import lax
from jax.experimental import pallas as pl
from jax.experimental.pallas import tpu as pltpu

F32 = jnp.float32
BF16 = jnp.bfloat16

D_MODEL = 1024
HEAD_DIM = 64
A_PATTERNS = ((128, 1), (512, 4), (2048, 16))
A_DILATIONS = tuple(d for _, d in A_PATTERNS)
A_GROUPS = len(A_PATTERNS)
A_HEADS_PER_GROUP = 4
A_WIDTH = 768
A_GROUP_WIDTH = A_HEADS_PER_GROUP * HEAD_DIM
A_ROT_DIM = 16
A_RADIUS = 64
ROPE_THETA = 500000.0
B_Q_WIDTH = 512
B_KV_HEADS = 2
B_KV_WIDTH = 128
B_GROUP_HEADS = 4
B_GROUP_WIDTH = B_GROUP_HEADS * HEAD_DIM
AXIAL_THETA = 10000.0
GRID_W = 64
GATE_WIDTH = 2048
N_EXPERTS = 16
CAPACITY_FACTOR = 2
PLE_DIM = 256
EPS = 1e-6
NEG_INF = -1e30
LN2 = math.log(2.0)
Q_SCALE = (HEAD_DIM ** -0.5) / LN2
B_UNSHIFTED_SCORE_MAX = 50.0

LANES = 128
SUBLANES = 8
VMEM_LIMIT = 48 * 1024 * 1024

ROW_TILE = 512
A_TQ = 512
A_SUB = 128
B_TQ = ROW_TILE
B_TK = 4 * ROW_TILE
POST_SPLIT = 1
FFN_TC = 512
MOE_T = ROW_TILE
MOE_WS_SHIFT = 7
MOE_WS = 1 << MOE_WS_SHIFT
MOE_ALIGN_SHIFT = 4
MOE_ALIGN = 1 << MOE_ALIGN_SHIFT
XS_WIDTH = D_MODEL + LANES
TM_POS = N_EXPERTS
TM_GATE_LO = 2 * N_EXPERTS


def _cparams(sem):
    return pltpu.CompilerParams(dimension_semantics=sem, vmem_limit_bytes=VMEM_LIMIT)


def _full_spec(a):
    return pl.BlockSpec(a.shape, lambda *_: (0,) * a.ndim)


def _rope_tables(s):
    pos = jnp.arange(s)
    posf = pos.astype(F32)
    inv_a = jnp.power(ROPE_THETA, -jnp.arange(0, A_ROT_DIM, 2, dtype=F32) / A_ROT_DIM)
    ang = posf[:, None] * inv_a[None, :]
    ca, sa = jnp.cos(ang), jnp.sin(ang)
    z8 = jnp.zeros_like(sa)
    rest = HEAD_DIM - A_ROT_DIM
    cos_a = jnp.concatenate([ca, ca, jnp.ones((s, rest), F32)], axis=1)
    sp_a = jnp.concatenate([z8, sa, jnp.zeros((s, rest), F32)], axis=1)
    sm_a = jnp.concatenate([-sa, z8, jnp.zeros((s, rest), F32)], axis=1)

    hb = HEAD_DIM // 2
    inv_b = jnp.power(AXIAL_THETA, -jnp.arange(0, hb, 2, dtype=F32) / hb)
    row = (pos // GRID_W).astype(F32)
    col = (pos % GRID_W).astype(F32)
    ar = row[:, None] * inv_b[None, :]
    ac = col[:, None] * inv_b[None, :]
    cr, sr, cc, sc = jnp.cos(ar), jnp.sin(ar), jnp.cos(ac), jnp.sin(ac)
    z16 = jnp.zeros_like(sr)
    cos_b = jnp.concatenate([cr, cr, cc, cc], axis=1)
    sp_b = jnp.concatenate([z16, sr, z16, sc], axis=1)
    sm_b = jnp.concatenate([-sr, z16, -sc, z16], axis=1)
    heads_per_tile = LANES // HEAD_DIM
    parts = [jnp.tile(t, (1, heads_per_tile)) for t in (cos_a, sp_a, sm_a, cos_b, sp_b, sm_b)]
    return jnp.concatenate(parts, axis=1)


def _rope_chunk(x, tab_ref, base, half):
    cos = tab_ref[:, base:base + LANES]
    sp = tab_ref[:, base + LANES:base + 2 * LANES]
    sm = tab_ref[:, base + 2 * LANES:base + 3 * LANES]
    return x * cos + pltpu.roll(x, half, 1) * sp + pltpu.roll(x, LANES - half, 1) * sm


def _head_mean_sq(acc, blk_ref):
    sq = acc * acc
    hi = sq.astype(BF16)
    lo = (sq - hi.astype(F32)).astype(BF16)
    blk = blk_ref[...]
    return (jnp.dot(hi, blk, preferred_element_type=F32)
            + jnp.dot(lo, blk, preferred_element_type=F32))


def _store_by_class(out_refs, chunk_idx, chunk, stage_ref):
    g, half = divmod(chunk_idx, A_GROUP_WIDTH // LANES)
    dil = A_DILATIONS[g]
    lanes = slice(half * LANES, (half + 1) * LANES)
    if dil == 1:
        out_refs[g][0, :, lanes] = chunk.astype(BF16)
        return
    tm = chunk.shape[0]
    stage_ref[...] = chunk
    for r in range(dil):
        out_refs[g][r, :, lanes] = stage_ref[pl.ds(r, tm // dil, stride=dil), :].astype(BF16)


def _in_proj_kernel(x_ref, tab_ref, gmix_ref, wqa_ref, wka_ref, wva_ref, wqb_ref, wkb_ref, wvb_ref,
                    wg_ref, gq_ref, gk_ref, blkq_ref, blkk_ref,
                    qa0_ref, qa1_ref, qa2_ref, ka0_ref, ka1_ref, ka2_ref, va0_ref, va1_ref, va2_ref,
                    qt_ref, kb_ref, vt_ref, gates_ref, stage_ref):
    x = x_ref[...]
    tm = x.shape[0]
    ms = jnp.mean(x * x, axis=-1, keepdims=True)
    h = (x * lax.rsqrt(ms + EPS) * gmix_ref[...]).astype(BF16)

    acc = jnp.dot(h, wqa_ref[...], preferred_element_type=F32)
    for c in range(A_WIDTH // LANES):
        roped = _rope_chunk(acc[:, c * LANES:(c + 1) * LANES], tab_ref, 0, A_ROT_DIM // 2) * Q_SCALE
        _store_by_class((qa0_ref, qa1_ref, qa2_ref), c, roped, stage_ref)

    acc = jnp.dot(h, wka_ref[...], preferred_element_type=F32)
    for c in range(A_WIDTH // LANES):
        roped = _rope_chunk(acc[:, c * LANES:(c + 1) * LANES], tab_ref, 0, A_ROT_DIM // 2)
        _store_by_class((ka0_ref, ka1_ref, ka2_ref), c, roped, stage_ref)

    acc = jnp.dot(h, wva_ref[...], preferred_element_type=F32)
    for c in range(A_WIDTH // LANES):
        _store_by_class((va0_ref, va1_ref, va2_ref), c, acc[:, c * LANES:(c + 1) * LANES], stage_ref)

    acc = jnp.dot(h, wqb_ref[...], preferred_element_type=F32)
    acc = acc * lax.rsqrt(_head_mean_sq(acc, blkq_ref) + EPS) * gq_ref[...]
    heads_per_chunk = LANES // HEAD_DIM
    for c in range(B_Q_WIDTH // LANES):
        roped = _rope_chunk(acc[:, c * LANES:(c + 1) * LANES], tab_ref, 3 * LANES, HEAD_DIM // 4) * Q_SCALE
        rt = roped.T.astype(BF16)
        for hh in range(heads_per_chunk):
            head = c * heads_per_chunk + hh
            g, hg = divmod(head, B_GROUP_HEADS)
            cols = slice(hg * tm, (hg + 1) * tm)
            qt_ref[g, g * HEAD_DIM:(g + 1) * HEAD_DIM, cols] = rt[hh * HEAD_DIM:(hh + 1) * HEAD_DIM, :]
            qt_ref[g, (1 - g) * HEAD_DIM:(2 - g) * HEAD_DIM, cols] = jnp.zeros((HEAD_DIM, tm), BF16)

    acc = jnp.dot(h, wkb_ref[...], preferred_element_type=F32)
    acc = acc * lax.rsqrt(_head_mean_sq(acc, blkk_ref) + EPS) * gk_ref[...]
    kb_ref[...] = _rope_chunk(acc, tab_ref, 3 * LANES, HEAD_DIM // 4).astype(BF16)

    vt = jnp.dot(h, wvb_ref[...], preferred_element_type=F32).T.astype(BF16)
    for g in range(B_KV_HEADS):
        vt_ref[g] = vt[g * HEAD_DIM:(g + 1) * HEAD_DIM, :]

    gates_ref[...] = jax.nn.sigmoid(jnp.dot(h, wg_ref[...], preferred_element_type=F32)).astype(BF16)


def _in_proj(xf, tab, b, s, gmix, w_in, gq, gk):
    n = xf.shape[0]
    tm = ROW_TILE
    nts = s // tm
    c1, c2, c3 = A_WIDTH, 2 * A_WIDTH, 3 * A_WIDTH
    c4 = c3 + B_Q_WIDTH
    c5 = c4 + B_KV_WIDTH
    c6 = c5 + B_KV_WIDTH
    wb = w_in.astype(BF16)
    ws = [wb[:, :c1], wb[:, c1:c2], wb[:, c2:c3], wb[:, c3:c4], wb[:, c4:c5], wb[:, c5:c6], wb[:, c6:]]

    def head_blockdiag(width):
        hid = jnp.arange(width) // HEAD_DIM
        return jnp.where(hid[:, None] == hid[None, :], 1.0 / HEAD_DIM, 0.0).astype(BF16)

    consts = [gmix.reshape(1, D_MODEL)] + ws + [
        jnp.tile(gq, B_Q_WIDTH // HEAD_DIM).reshape(1, B_Q_WIDTH),
        jnp.tile(gk, B_KV_WIDTH // HEAD_DIM).reshape(1, B_KV_WIDTH),
        head_blockdiag(B_Q_WIDTH), head_blockdiag(B_KV_WIDTH)]

    a_shapes, a_specs = [], []
    for _ in range(3):
        for dil in A_DILATIONS:
            a_shapes.append(jax.ShapeDtypeStruct((b, dil, s // dil, A_GROUP_WIDTH), BF16))
            a_specs.append(pl.BlockSpec((None, dil, tm // dil, A_GROUP_WIDTH),
                                        lambda i: (i // nts, 0, i % nts, 0)))
    kpb = B_TK // tm
    b_shapes = [jax.ShapeDtypeStruct((b, B_KV_HEADS, nts, B_KV_WIDTH, B_GROUP_HEADS * tm), BF16),
                jax.ShapeDtypeStruct((n, B_KV_WIDTH), BF16),
                jax.ShapeDtypeStruct((b, B_KV_HEADS, s // B_TK, HEAD_DIM, B_TK), BF16),
                jax.ShapeDtypeStruct((n, GATE_WIDTH), BF16)]
    b_specs = [pl.BlockSpec((None, B_KV_HEADS, None, B_KV_WIDTH, B_GROUP_HEADS * tm),
                            lambda i: (i // nts, 0, i % nts, 0, 0)),
               pl.BlockSpec((tm, B_KV_WIDTH), lambda i: (i, 0)),
               pl.BlockSpec((None, B_KV_HEADS, None, HEAD_DIM, tm),
                            lambda i: (i // nts, 0, (i % nts) // kpb, 0, i % kpb)),
               pl.BlockSpec((tm, GATE_WIDTH), lambda i: (i, 0))]

    return pl.pallas_call(
        _in_proj_kernel,
        out_shape=a_shapes + b_shapes,
        grid=(n // tm,),
        in_specs=[pl.BlockSpec((tm, D_MODEL), lambda i: (i, 0)),
                  pl.BlockSpec((tm, tab.shape[1]), lambda i: (i % nts, 0))] + [_full_spec(a) for a in consts],
        out_specs=a_specs + b_specs,
        scratch_shapes=[pltpu.VMEM((tm, LANES), F32)],
        compiler_params=_cparams(("parallel",)),
        name="in_proj",
    )(xf, tab, *consts)


def _band_bias():
    qi = jnp.arange(A_SUB)[:, None]
    kj = jnp.arange(A_SUB + 2 * A_RADIUS)[None, :]
    band = jnp.abs(kj - A_RADIUS - qi) <= A_RADIUS
    first = band & (kj >= A_RADIUS)
    last = band & (kj < A_SUB + A_RADIUS)
    return jnp.where(jnp.stack([band, first, last]), 0.0, NEG_INF).astype(F32)


def _attn_a_kernel(bias_ref, q_ref, kp_ref, kc_ref, kn_ref, vp_ref, vc_ref, vn_ref, o_ref, lse_ref):
    tq = q_ref.shape[0]
    nsub = tq // A_SUB
    i = pl.program_id(2)
    last_tile = pl.num_programs(2) - 1
    first_head = lax.broadcasted_iota(jnp.int32, (1, LANES), 1) < HEAD_DIM
    nt_dims = (((1,), (1,)), ((), ()))

    def window(p_ref, c_ref, n_ref, u, cols):
        lo, hi = u * A_SUB - A_RADIUS, (u + 1) * A_SUB + A_RADIUS
        parts = [p_ref[:, cols]] if lo < 0 else []
        parts.append(c_ref[max(lo, 0):min(hi, tq), cols])
        if hi > tq:
            parts.append(n_ref[:, cols])
        return parts[0] if len(parts) == 1 else jnp.concatenate(parts, axis=0)

    for u in range(nsub):
        if u == 0:
            bias = bias_ref[jnp.where(i == 0, 1, 0)]
        elif u == nsub - 1:
            bias = bias_ref[jnp.where(i == last_tile, 2, 0)]
        else:
            bias = bias_ref[0]
        rows = slice(u * A_SUB, (u + 1) * A_SUB)
        for pair in range(A_GROUP_WIDTH // LANES):
            cols = slice(pair * LANES, (pair + 1) * LANES)
            k = window(kp_ref, kc_ref, kn_ref, u, cols)
            v = window(vp_ref, vc_ref, vn_ref, u, cols)
            q = q_ref[rows, cols]
            outs, lses = [], []
            for hh in range(LANES // HEAD_DIM):
                mine = first_head if hh == 0 else jnp.logical_not(first_head)
                qm = jnp.where(mine, q, jnp.zeros_like(q))
                sc = lax.dot_general(qm, k, nt_dims, preferred_element_type=F32) + bias
                m = jnp.max(sc, axis=1, keepdims=True)
                p = jnp.exp2(sc - m)
                den = jnp.sum(p, axis=1, keepdims=True)
                outs.append(jnp.dot(p.astype(BF16), v, preferred_element_type=F32) / den)
                lses.append(m * LN2 + jnp.log(den))
            o_ref[rows, cols] = jnp.where(first_head, outs[0], outs[1])
            lse_ref[rows, cols] = jnp.where(first_head, lses[0], lses[1])


def _attn_a(q, k, v, gi):
    b, dil, cl, _ = q.shape
    tq = min(A_TQ, cl)
    assert tq >= 2 * A_SUB and A_SUB == 2 * A_RADIUS
    hb = tq // A_RADIUS
    n_halo = cl // A_RADIUS
    bias = _band_bias()
    main = pl.BlockSpec((None, None, tq, A_GROUP_WIDTH), lambda bi, r, i: (bi, r, i, 0))
    prev = pl.BlockSpec((None, None, A_RADIUS, A_GROUP_WIDTH),
                        lambda bi, r, i: (bi, r, jnp.maximum(i * hb - 1, 0), 0))
    nxt = pl.BlockSpec((None, None, A_RADIUS, A_GROUP_WIDTH),
                       lambda bi, r, i: (bi, r, jnp.minimum((i + 1) * hb, n_halo - 1), 0))
    return pl.pallas_call(
        _attn_a_kernel,
        out_shape=[jax.ShapeDtypeStruct((b, dil, cl, A_GROUP_WIDTH), F32)] * 2,
        grid=(b, dil, cl // tq),
        in_specs=[_full_spec(bias), main, prev, main, nxt, prev, main, nxt],
        out_specs=[main, main],
        compiler_params=_cparams(("parallel", "parallel", "parallel")),
        name=f"attn_a{gi}",
    )(bias, q, k, k, k, v, v, v)


def _flash_b_kernel(qt_ref, k_ref, vt_ref, o_ref, acc_sc, l_sc, m_sc, p0_sc, p1_sc, *, online):
    nk, tk, _ = k_ref.shape
    width = qt_ref.shape[1]
    tq = width // B_GROUP_HEADS
    acc_sc[...] = jnp.zeros(acc_sc.shape, F32)
    l_sc[...] = jnp.zeros(l_sc.shape, F32)

    def scores(j):
        return jnp.dot(k_ref[j], qt_ref[...], preferred_element_type=F32)

    def column_sums(p):
        return p.reshape(tk // SUBLANES, SUBLANES, width).sum(axis=0)

    def accumulate(j, pb):
        vt = vt_ref[j]
        for h in range(B_GROUP_HEADS):
            cols = slice(h * tq, (h + 1) * tq)
            acc_sc[:, cols] += jnp.dot(vt, pb[:, cols], preferred_element_type=F32)

    if online:
        m_sc[...] = jnp.full(m_sc.shape, NEG_INF, F32)

        def body(j, carry):
            sc = scores(j)
            m_old = m_sc[...]
            m_new = jnp.maximum(m_old, jnp.max(sc, axis=0, keepdims=True))
            alpha = jnp.exp2(m_old - m_new)
            m_sc[...] = m_new
            p = jnp.exp2(sc - m_new)
            l_sc[...] = alpha * l_sc[...] + column_sums(p)
            acc_sc[...] = alpha * acc_sc[...]
            accumulate(j, p.astype(BF16))
            return carry

        lax.fori_loop(0, nk, body, 0)
    else:
        def exponentials(sc, p_out):
            p = jnp.exp2(sc)
            l_sc[...] += column_sums(p)
            p_out[...] = p.astype(BF16)

        def stage(j, p_in, p_out):
            sc = scores(j)
            accumulate(j - 1, p_in[...])
            exponentials(sc, p_out)

        exponentials(scores(0), p0_sc)

        def body(jj, carry):
            stage(2 * jj + 1, p0_sc, p1_sc)
            stage(2 * jj + 2, p1_sc, p0_sc)
            return carry

        lax.fori_loop(0, nk // 2 - 1, body, 0)
        stage(nk - 1, p0_sc, p1_sc)
        accumulate(nk - 1, p1_sc[...])
    o = acc_sc[...] / jnp.sum(l_sc[...], axis=0, keepdims=True)
    o = jnp.concatenate([o[:, h * tq:(h + 1) * tq] for h in range(B_GROUP_HEADS)], axis=0)
    o_ref[...] = o.T.astype(o_ref.dtype)


def _flash_b(qt, kb, vt, online):
    b, _, nq, _, width = qt.shape
    tq = width // B_GROUP_HEADS
    _, nk, tk, _ = kb.shape
    return pl.pallas_call(
        functools.partial(_flash_b_kernel, online=online),
        out_shape=jax.ShapeDtypeStruct((b, B_KV_HEADS, nq * tq, B_GROUP_WIDTH), BF16),
        grid=(b, B_KV_HEADS, nq),
        in_specs=[pl.BlockSpec((None, None, None, B_KV_WIDTH, width), lambda bi, g, i: (bi, g, i, 0, 0)),
                  pl.BlockSpec((None, nk, tk, B_KV_WIDTH), lambda bi, g, i: (bi, 0, 0, 0)),
                  pl.BlockSpec((None, None, nk, HEAD_DIM, tk), lambda bi, g, i: (bi, g, 0, 0, 0))],
        out_specs=pl.BlockSpec((None, None, tq, B_GROUP_WIDTH), lambda bi, g, i: (bi, g, i, 0)),
        scratch_shapes=[pltpu.VMEM((HEAD_DIM, width), F32), pltpu.VMEM((SUBLANES, width), F32),
                        pltpu.VMEM((1, width), F32),
                        pltpu.VMEM((tk, width), BF16), pltpu.VMEM((tk, width), BF16)],
        compiler_params=_cparams(("parallel", "parallel", "parallel")),
        name="flash_b_online" if online else "flash_b",
    )(qt, kb, vt)


def _stage_classes(blk_ref, stage_ref):
    dil, rows, width = blk_ref.shape
    if dil == 1:
        return
    for c in range(width // LANES):
        for r in range(dil):
            stage_ref.at[c][pl.ds(r, rows, stride=dil), :] = blk_ref[r, :, c * LANES:(c + 1) * LANES]


def _token_rows(blk_ref, stage_ref, rows):
    if blk_ref.shape[0] == 1:
        return blk_ref[0, rows, :]
    return jnp.concatenate([stage_ref[c, rows, :] for c in range(stage_ref.shape[0])], axis=1)


def _post_attn_kernel(x_ref, o0_ref, o1_ref, o2_ref, l0_ref, l1_ref, l2_ref, ob0_ref, ob1_ref, gates_ref,
                      wa_ref, wb0_ref, wb1_ref, wout_ref, gffn_ref, wrh_ref, wrl_ref,
                      x1_ref, h2_ref, aff_ref, so1, so2, sl1, sl2):
    tm = x_ref.shape[0]
    for blk, stage in ((l1_ref, sl1), (l2_ref, sl2), (o1_ref, so1), (o2_ref, so2)):
        _stage_classes(blk, stage)
    nt = (((1,), (1,)), ((), ()))
    hr = tm // POST_SPLIT
    for half in range(POST_SPLIT):
        rows = slice(half * hr, (half + 1) * hr)
        l0 = _token_rows(l0_ref, None, rows)
        l1 = _token_rows(l1_ref, sl1, rows)
        l2 = _token_rows(l2_ref, sl2, rows)
        m = jnp.maximum(jnp.maximum(l0, l1), l2)
        e0, e1, e2 = jnp.exp(l0 - m), jnp.exp(l1 - m), jnp.exp(l2 - m)
        oa = (e0 * _token_rows(o0_ref, None, rows) + e1 * _token_rows(o1_ref, so1, rows)
              + e2 * _token_rows(o2_ref, so2, rows)) / (e0 + e1 + e2)
        ya = jnp.dot(oa.astype(BF16), wa_ref[...], preferred_element_type=F32)
        yb = (jnp.dot(ob0_ref[rows, :], wb0_ref[...], preferred_element_type=F32)
              + jnp.dot(ob1_ref[rows, :], wb1_ref[...], preferred_element_type=F32))
        ga = gates_ref[rows, :D_MODEL].astype(F32)
        gb = gates_ref[rows, D_MODEL:].astype(F32)
        z = (ga * ya + gb * yb).astype(BF16)
        x1 = x_ref[rows, :] + jnp.dot(z, wout_ref[...], preferred_element_type=F32)
        x1_ref[rows, :] = x1
        ms = jnp.mean(x1 * x1, axis=-1, keepdims=True)
        h2 = x1 * lax.rsqrt(ms + EPS) * gffn_ref[...]
        hi = h2.astype(BF16)
        lo = (h2 - hi.astype(F32)).astype(BF16)
        h2_ref[rows, :] = hi
        both = lax.dot_general(wrl_ref[...], hi, nt, preferred_element_type=F32)
        logits = (both[:N_EXPERTS] + both[N_EXPERTS:]
                  + lax.dot_general(wrh_ref[...], lo, nt, preferred_element_type=F32))
        logits = logits - jnp.max(logits, axis=0, keepdims=True)
        e = jnp.exp(logits)
        aff = e / jnp.sum(e, axis=0, keepdims=True)
        for j in range(hr // LANES):
            aff_ref[half * (hr // LANES) + j] = aff[:, j * LANES:(j + 1) * LANES]


def _post_attn(xf, oas, lses, ob, gates, b, s, w_a, w_b, w_out, g_ffn, w_router):
    n = xf.shape[0]
    tm = ROW_TILE
    nts = s // tm
    wa = w_a.astype(BF16)
    wb = w_b.astype(BF16)
    wb0, wb1 = wb[:B_GROUP_WIDTH], wb[B_GROUP_WIDTH:]
    wout = w_out.astype(BF16)
    wrt = w_router.T
    wrh = wrt.astype(BF16)
    wrl = jnp.concatenate([wrh, (wrt - wrh.astype(F32)).astype(BF16)], axis=0)
    consts = [wa, wb0, wb1, wout, g_ffn.reshape(1, D_MODEL), wrh, wrl]

    def rows(w):
        return pl.BlockSpec((tm, w), lambda i: (i, 0))

    def class_spec(dil):
        return pl.BlockSpec((None, dil, tm // dil, A_GROUP_WIDTH), lambda i: (i // nts, 0, i % nts, 0))

    def ob_spec(g):
        return pl.BlockSpec((None, None, tm, B_GROUP_WIDTH), lambda i: (i // nts, g, i % nts, 0))

    a_specs = [class_spec(d) for d in A_DILATIONS]
    return pl.pallas_call(
        _post_attn_kernel,
        out_shape=[jax.ShapeDtypeStruct((n, D_MODEL), F32),
                   jax.ShapeDtypeStruct((n, D_MODEL), BF16),
                   jax.ShapeDtypeStruct((n // LANES, N_EXPERTS, LANES), F32)],
        grid=(n // tm,),
        in_specs=[rows(D_MODEL)] + a_specs + a_specs + [ob_spec(0), ob_spec(1), rows(GATE_WIDTH)]
                 + [_full_spec(a) for a in consts],
        out_specs=[rows(D_MODEL), rows(D_MODEL),
                   pl.BlockSpec((tm // LANES, N_EXPERTS, LANES), lambda i: (i, 0, 0))],
        scratch_shapes=[pltpu.VMEM((A_GROUP_WIDTH // LANES, tm, LANES), F32)] * 4,
        compiler_params=_cparams(("parallel",)),
        name="post_attn",
    )(xf, *oas, *lses, ob, ob, gates, *consts)


def _route_kernel(aff_ref, tri_ref, posl_ref, starts_ref, sel_sc, *, cap, idx_bits, tiles_per_block):
    nlt = aff_ref.shape[0]
    capf = float(cap)

    def count(mask):
        part = jnp.sum(jnp.where(mask, 1.0, 0.0), axis=0)
        return jnp.sum(part, axis=1, keepdims=True)[None]

    def value_body(i, tau):
        bits = pltpu.bitcast(aff_ref[...], jnp.int32)
        cand = tau | lax.shift_left(jnp.int32(1), 30 - i)
        return jnp.where(count(bits >= cand) >= capf, cand, tau)

    tau = lax.fori_loop(0, 31, value_body, jnp.zeros((1, N_EXPERTS, 1), jnp.int32))
    bits = pltpu.bitcast(aff_ref[...], jnp.int32)
    need = capf - count(bits > tau)
    tok = (lax.broadcasted_iota(jnp.int32, bits.shape, 0) * LANES
           + lax.broadcasted_iota(jnp.int32, bits.shape, 2))

    def index_body(i, last):
        b = pltpu.bitcast(aff_ref[...], jnp.int32)
        cand = last | lax.shift_left(jnp.int32(1), idx_bits - 1 - i)
        return jnp.where(count((b == tau) & (tok < cand)) < need, cand, last)

    last = lax.fori_loop(0, idx_bits, index_body, jnp.zeros((1, N_EXPERTS, 1), jnp.int32))
    sel_sc[...] = jnp.where((bits > tau) | ((bits == tau) & (tok <= last)), 1.0, 0.0)

    tri = tri_ref[...]
    ones = jnp.ones((LANES, LANES), BF16)

    def prefix_body(c, before):
        @pl.when(c % tiles_per_block == 0)
        def _():
            starts_ref[c // tiles_per_block] = before.astype(jnp.int32)

        s = sel_sc[c]
        sb = s.astype(BF16)
        pos = before + jnp.dot(sb, tri, preferred_element_type=F32) - s
        posl_ref[c] = jnp.where(s > 0, pos, -1.0).astype(jnp.int32)
        return before + jnp.dot(sb, ones, preferred_element_type=F32)

    lax.fori_loop(0, nlt, prefix_body, jnp.zeros((N_EXPERTS, LANES), F32))


def _route(aff3, cap):
    nlt = aff3.shape[0]
    n = nlt * LANES
    tpb = MOE_T // LANES
    tri = (jnp.arange(LANES)[:, None] <= jnp.arange(LANES)[None, :]).astype(BF16)
    vm = pl.BlockSpec(memory_space=pltpu.VMEM)
    return pl.pallas_call(
        functools.partial(_route_kernel, cap=cap, idx_bits=(n - 1).bit_length(), tiles_per_block=tpb),
        out_shape=[jax.ShapeDtypeStruct((nlt, N_EXPERTS, LANES), jnp.int32),
                   jax.ShapeDtypeStruct((n // MOE_T, N_EXPERTS, LANES), jnp.int32)],
        in_specs=[vm, vm],
        out_specs=[vm, vm],
        scratch_shapes=[pltpu.VMEM((nlt, N_EXPERTS, LANES), F32)],
        compiler_params=pltpu.CompilerParams(vmem_limit_bytes=VMEM_LIMIT),
        name="route",
    )(aff3, tri)


def _block_slots(s_ref, i, nt):
    return [(s_ref[e * (nt + 1) + i], s_ref[e * (nt + 1) + i + 1]) for e in range(N_EXPERTS)]


def _align_down(x):
    return (x >> MOE_ALIGN_SHIFT) << MOE_ALIGN_SHIFT


def _moe_gather_kernel(s_ref, aff_ref, posl_ref, h2_ref, xs_hbm, tm_ref,
                       haug_sc, pall_sc, res_sc, stage_sc, carry_sc, pending_sc, sem, *, nt, cap):
    i = pl.program_id(0)
    nj = posl_ref.shape[0]
    ws = MOE_WS

    @pl.when(i == 0)
    def _():
        carry_sc[...] = jnp.zeros(carry_sc.shape, F32)
        pending_sc[0] = 0
        pad = xs_hbm.shape[1] - cap
        stage_sc[...] = jnp.zeros(stage_sc.shape, BF16)
        fills = [pltpu.make_async_copy(stage_sc.at[pl.ds(0, pad)], xs_hbm.at[e, pl.ds(cap, pad)], sem)
                 for e in range(N_EXPERTS)]
        for cp in fills:
            cp.start()
        for cp in fills:
            cp.wait()

    for j in range(nj):
        pj = posl_ref[j]
        gj = jnp.where(pj >= 0, aff_ref[j], 0.0)
        blk = jnp.concatenate([gj, pj.astype(F32), gj, jnp.zeros((LANES - 3 * N_EXPERTS, LANES), F32)], axis=0)
        tm_ref[j * LANES:(j + 1) * LANES, :] = blk.T
    tmv = tm_ref[...]
    lane = lax.broadcasted_iota(jnp.int32, tmv.shape, 1)
    ghi = tmv.astype(BF16)
    glo = (tmv - ghi.astype(F32)).astype(BF16)
    zero = jnp.zeros_like(ghi)
    haug_sc[:, :D_MODEL] = h2_ref[...]
    haug_sc[:, D_MODEL:] = jnp.where(lane < N_EXPERTS, ghi,
                                     jnp.where((lane >= TM_GATE_LO) & (lane < TM_GATE_LO + N_EXPERTS), glo, zero))

    astarts, kks, offcs = [], [], []
    nw = jnp.int32(0)
    for st, en in _block_slots(s_ref, i, nt):
        ast = _align_down(st)
        span = _align_down(en) - ast
        kk = span >> MOE_WS_SHIFT
        astarts.append(ast)
        kks.append(kk)
        offcs.append(span - kk * ws)
        nw = jnp.maximum(nw, kk + 1)

    def wait_window():
        for _ in range(N_EXPERTS):
            pltpu.make_async_copy(stage_sc.at[pl.ds(0, ws)], xs_hbm.at[0, pl.ds(0, ws)], sem).wait()

    row = lax.broadcasted_iota(jnp.int32, (ws, LANES), 0)

    def window(k, carry):
        for e in range(N_EXPERTS):
            base = astarts[e] + k * ws
            for j in range(nj):
                hit = (posl_ref[j, e:e + 1, :] - base) == row
                pall_sc[e * ws:(e + 1) * ws, j * LANES:(j + 1) * LANES] = jnp.where(hit, 1.0, 0.0).astype(BF16)
        res_sc[...] = jnp.dot(pall_sc[...], haug_sc[...], preferred_element_type=F32)

        @pl.when(k == 0)
        def _():
            for e in range(N_EXPERTS):
                res_sc[e * ws:e * ws + MOE_ALIGN, :] += carry_sc[e]

        for e in range(N_EXPERTS):
            @pl.when(k == kks[e])
            def _():
                carry_sc[e] = res_sc[pl.ds(pl.multiple_of(e * ws + offcs[e], MOE_ALIGN), MOE_ALIGN), :]

        @pl.when(pending_sc[0] == 1)
        def _():
            wait_window()

        stage_sc[...] = res_sc[...].astype(BF16)
        for e in range(N_EXPERTS):
            base = pl.multiple_of(astarts[e] + k * ws, MOE_ALIGN)
            pltpu.make_async_copy(stage_sc.at[pl.ds(e * ws, ws)], xs_hbm.at[e, pl.ds(base, ws)], sem).start()
        pending_sc[0] = 1
        return carry

    lax.fori_loop(0, nw, window, 0)

    @pl.when((i == nt - 1) & (pending_sc[0] == 1))
    def _():
        wait_window()
        pending_sc[0] = 0


def _moe_gather(s_flat, aff3, posl, h2, cap):
    n = h2.shape[0]
    t = MOE_T
    nt = n // t
    cap_pad = cap + (t // MOE_WS + 1) * MOE_WS
    blk3 = pl.BlockSpec((t // LANES, N_EXPERTS, LANES), lambda i, s: (i, 0, 0))
    grid_spec = pltpu.PrefetchScalarGridSpec(
        num_scalar_prefetch=1,
        grid=(nt,),
        in_specs=[blk3, blk3, pl.BlockSpec((t, D_MODEL), lambda i, s: (i, 0))],
        out_specs=[pl.BlockSpec(memory_space=pl.ANY), pl.BlockSpec((t, LANES), lambda i, s: (i, 0))],
        scratch_shapes=[pltpu.VMEM((t, XS_WIDTH), BF16),
                        pltpu.VMEM((N_EXPERTS * MOE_WS, t), BF16),
                        pltpu.VMEM((N_EXPERTS * MOE_WS, XS_WIDTH), F32),
                        pltpu.VMEM((N_EXPERTS * MOE_WS, XS_WIDTH), BF16),
                        pltpu.VMEM((N_EXPERTS, MOE_ALIGN, XS_WIDTH), F32),
                        pltpu.SMEM((1,), jnp.int32),
                        pltpu.SemaphoreType.DMA(())])
    return pl.pallas_call(
        functools.partial(_moe_gather_kernel, nt=nt, cap=cap),
        out_shape=[jax.ShapeDtypeStruct((N_EXPERTS, cap_pad, XS_WIDTH), BF16),
                   jax.ShapeDtypeStruct((n, LANES), F32)],
        grid_spec=grid_spec,
        compiler_params=_cparams(("arbitrary",)),
        name="moe_gather",
    )(s_flat, aff3, posl, h2)


def _moe_ffn_kernel(xs_ref, w1_ref, w3_ref, w2_ref, ye_ref):
    xe = xs_ref[:, :D_MODEL]
    route = xs_ref[:, D_MODEL:].astype(F32)
    lane = lax.broadcasted_iota(jnp.int32, route.shape, 1)
    e = pl.program_id(0)
    gate = jnp.sum(jnp.where((lane == e) | (lane == e + TM_GATE_LO), route, 0.0), axis=1, keepdims=True)
    a = jnp.dot(xe, w1_ref[...], preferred_element_type=F32)
    u = jnp.dot(xe, w3_ref[...], preferred_element_type=F32)
    hid = (a * jax.nn.sigmoid(a) * u).astype(BF16)
    ye = jnp.dot(hid, w2_ref[...], preferred_element_type=F32) * gate
    ye_ref[...] = ye.astype(ye_ref.dtype)


def _moe_ffn(xs, cap, w1, w3, w2):
    e = xs.shape[0]
    tc = min(FFN_TC, cap)
    wspec = pl.BlockSpec((None, D_MODEL, D_MODEL), lambda ei, ci: (ei, 0, 0))
    return pl.pallas_call(
        _moe_ffn_kernel,
        out_shape=jax.ShapeDtypeStruct((e, cap, D_MODEL), BF16),
        grid=(e, cap // tc),
        in_specs=[pl.BlockSpec((None, tc, XS_WIDTH), lambda ei, ci: (ei, ci, 0)), wspec, wspec, wspec],
        out_specs=pl.BlockSpec((None, tc, D_MODEL), lambda ei, ci: (ei, ci, 0)),
        compiler_params=_cparams(("parallel", "arbitrary")),
        name="moe_ffn",
    )(xs, w1, w3, w2)


def _final_kernel(s_ref, x1_ref, tm_ref, p_ref, ye_hbm, wple_ref, wpg_ref, gple_ref, gfin_ref, expand_ref, out_ref,
                  buf_sc, pall_sc, y_sc, sem, *, nt, cap):
    i = pl.program_id(0)
    ws = MOE_WS
    t = x1_ref.shape[0]
    slot = i % 2

    def window_rows(blk, k):
        out = []
        for st, _ in _block_slots(s_ref, blk, nt):
            lo = _align_down(st) + k * ws
            out.append((lo, pl.multiple_of(jnp.minimum(lo, cap - ws), MOE_ALIGN)))
        return out

    def copies(rows, buf, which):
        return [pltpu.make_async_copy(ye_hbm.at[e, pl.ds(base, ws)], buf_sc.at[buf, pl.ds(e * ws, ws)], sem.at[which])
                for e, (_, base) in enumerate(rows)]

    @pl.when(i == 0)
    def _():
        for cp in copies(window_rows(0, 0), 0, 0):
            cp.start()

    @pl.when(i + 1 < nt)
    def _():
        for cp in copies(window_rows(i + 1, 0), 1 - slot, 1 - slot):
            cp.start()

    nw = jnp.int32(0)
    for st, en in _block_slots(s_ref, i, nt):
        nw = jnp.maximum(nw, (en - _align_down(st) + ws - 1) >> MOE_WS_SHIFT)
    lane = lax.broadcasted_iota(jnp.int32, (1, LANES), 1)
    col = (lax.broadcasted_iota(jnp.int32, (1, N_EXPERTS * ws), 1) & (ws - 1)).astype(F32)

    def onehot(rows):
        basev = jnp.zeros((1, LANES), F32)
        lov = jnp.full((1, LANES), float(2 ** 30), F32)
        for e, (lo, base) in enumerate(rows):
            basev = jnp.where(lane == TM_POS + e, base.astype(F32), basev)
            lov = jnp.where(lane == TM_POS + e, lo.astype(F32), lov)
        tmv = tm_ref[...]
        off = tmv - basev
        off = jnp.where((tmv >= lov) & (off >= 0.0) & (off < float(ws)), off, -1.0)
        spread = jnp.dot(off.astype(BF16), expand_ref[...], preferred_element_type=F32)
        pall_sc[...] = jnp.where(spread == col, 1.0, 0.0).astype(BF16)

    rows0 = window_rows(i, 0)
    onehot(rows0)
    for cp in copies(rows0, slot, slot):
        cp.wait()
    y_sc[...] = jnp.dot(pall_sc[...], buf_sc[slot], preferred_element_type=F32)

    def window(k, carry):
        rows = window_rows(i, k)
        cps = copies(rows, slot, slot)
        for cp in cps:
            cp.start()
        onehot(rows)
        for cp in cps:
            cp.wait()
        y_sc[...] += jnp.dot(pall_sc[...], buf_sc[slot], preferred_element_type=F32)
        return carry

    lax.fori_loop(1, nw, window, 0)

    hr = t // POST_SPLIT
    for half in range(POST_SPLIT):
        rows = slice(half * hr, (half + 1) * hr)
        x2 = x1_ref[rows, :] + y_sc[rows, :]
        ms = jnp.mean(x2 * x2, axis=-1, keepdims=True)
        hp = (x2 * lax.rsqrt(ms + EPS) * gple_ref[...]).astype(BF16)
        gt = jax.nn.sigmoid(jnp.dot(hp, wpg_ref[...], preferred_element_type=F32))
        emb = jnp.dot(p_ref[rows, :].astype(BF16), wple_ref[...], preferred_element_type=F32)
        x3 = x2 + emb * gt
        ms3 = jnp.mean(x3 * x3, axis=-1, keepdims=True)
        out_ref[rows, :] = x3 * lax.rsqrt(ms3 + EPS) * gfin_ref[...]


def _final(s_flat, x1, tm, pf, ye, w_ple, w_ple_gate, g_ple, g_final):
    n = x1.shape[0]
    t = MOE_T
    nt = n // t
    cap = ye.shape[1]
    src = jnp.arange(LANES)[:, None] - TM_POS
    dst = jnp.arange(N_EXPERTS * MOE_WS)[None, :] >> MOE_WS_SHIFT
    expand = (src == dst).astype(BF16)
    consts = [w_ple.astype(BF16), w_ple_gate.astype(BF16), g_ple.reshape(1, D_MODEL), g_final.reshape(1, D_MODEL),
              expand]

    def rows(w):
        return pl.BlockSpec((t, w), lambda i, s: (i, 0))

    grid_spec = pltpu.PrefetchScalarGridSpec(
        num_scalar_prefetch=1,
        grid=(nt,),
        in_specs=[rows(D_MODEL), rows(LANES), rows(PLE_DIM), pl.BlockSpec(memory_space=pl.ANY)]
                 + [_full_spec(a) for a in consts],
        out_specs=rows(D_MODEL),
        scratch_shapes=[pltpu.VMEM((2, N_EXPERTS * MOE_WS, D_MODEL), BF16),
                        pltpu.VMEM((t, N_EXPERTS * MOE_WS), BF16),
                        pltpu.VMEM((t, D_MODEL), F32),
                        pltpu.SemaphoreType.DMA((2,))])
    return pl.pallas_call(
        functools.partial(_final_kernel, nt=nt, cap=cap),
        out_shape=jax.ShapeDtypeStruct((n, D_MODEL), F32),
        grid_spec=grid_spec,
        compiler_params=_cparams(("arbitrary",)),
        name="final",
    )(s_flat, x1, tm, pf, ye, *consts)


def _run_group(x, p, g_mix, w_in, g_qn, g_kn, w_a_proj, w_b_proj, w_out, g_ffn, w_router,
               w1, w3, w2, g_ple, w_ple, w_ple_gate, g_final):
    b, s, _ = x.shape
    n = b * s
    assert s % B_TK == 0 and s % (A_DILATIONS[-1] * A_RADIUS) == 0
    xf = x.reshape(n, D_MODEL)
    tab = _rope_tables(s)
    outs = _in_proj(xf, tab, b, s, g_mix, w_in, g_qn, g_kn)
    qas, kas, vas = outs[0:3], outs[3:6], outs[6:9]
    qt, kb, vt, gates = outs[9:]

    oas, lses = [], []
    for gi, (window, dil) in enumerate(A_PATTERNS):
        assert window // (2 * dil) == A_RADIUS
        o, lse = _attn_a(qas[gi], kas[gi], vas[gi], gi)
        oas.append(o)
        lses.append(lse)

    score_bound = (HEAD_DIM * Q_SCALE * 1.02) * jnp.max(jnp.abs(g_qn)) * jnp.max(jnp.abs(g_kn))
    kb4 = kb.reshape(b, s // B_TK, B_TK, B_KV_WIDTH)
    ob = lax.cond(score_bound <= B_UNSHIFTED_SCORE_MAX,
                  lambda: _flash_b(qt, kb4, vt, online=False),
                  lambda: _flash_b(qt, kb4, vt, online=True))

    x1, h2, aff3 = _post_attn(xf, oas, lses, ob, gates, b, s, w_a_proj, w_b_proj, w_out, g_ffn, w_router)

    cap = CAPACITY_FACTOR * n // N_EXPERTS
    assert cap % MOE_WS == 0 and cap >= MOE_WS
    posl, starts = _route(aff3, cap)
    s_flat = jnp.concatenate([starts[:, :, 0].T, jnp.full((N_EXPERTS, 1), cap, jnp.int32)], axis=1).reshape(-1)
    xs, tm = _moe_gather(s_flat, aff3, posl, h2, cap)
    ye = _moe_ffn(xs, cap, w1, w3, w2)
    out = _final(s_flat, x1, tm, p.reshape(n, PLE_DIM), ye, w_ple, w_ple_gate, g_ple, g_final)
    return out.reshape(b, s, D_MODEL)


def kernel(x_prompt, x_sample, p_prompt, p_sample, g_mix, w_in, g_qn, g_kn, w_a_proj, w_b_proj, w_out, g_ffn,
           w_router, w_exp_gate, w_exp_up, w_exp_down, g_ple, w_ple, w_ple_gate, g_final):
    assert g_mix.shape[0] == 1, "single layer"
    w1 = w_exp_gate[0].astype(BF16)
    w3 = w_exp_up[0].astype(BF16)
    w2 = w_exp_down[0].astype(BF16)
    args = (g_mix[0], w_in[0], g_qn[0], g_kn[0], w_a_proj[0], w_b_proj[0], w_out[0], g_ffn[0], w_router[0],
            w1, w3, w2, g_ple[0], w_ple[0], w_ple_gate[0], g_final)
    y_prompt = _run_group(x_prompt, p_prompt[0], *args)
    y_sample = _run_group(x_sample, p_sample[0], *args)
    return (y_prompt, y_sample)
```

```python
import functools
import math

import jax
import jax.numpy as jnp
from jax import lax
from jax.experimental import pallas as pl
from jax.experimental.pallas import tpu as pltpu

F32 = jnp.float32
BF16 = jnp.bfloat16

D_MODEL = 1024
HEAD_DIM = 64
A_PATTERNS = ((128, 1), (512, 4), (2048, 16))
A_DILATIONS = tuple(d for _, d in A_PATTERNS)
A_GROUPS = len(A_PATTERNS)
A_HEADS_PER_GROUP = 4
A_WIDTH = 768
A_GROUP_WIDTH = A_HEADS_PER_GROUP * HEAD_DIM
A_ROT_DIM = 16
A_RADIUS = 64
ROPE_THETA = 500000.0
B_Q_WIDTH = 512
B_KV_HEADS = 2
B_KV_WIDTH = 128
B_GROUP_HEADS = 4
B_GROUP_WIDTH = B_GROUP_HEADS * HEAD_DIM
AXIAL_THETA = 10000.0
GRID_W = 64
GATE_WIDTH = 2048
N_EXPERTS = 16
CAPACITY_FACTOR = 2
PLE_DIM = 256
EPS = 1e-6
NEG_INF = -1e30
LN2 = math.log(2.0)
Q_SCALE = (HEAD_DIM ** -0.5) / LN2
B_UNSHIFTED_SCORE_MAX = 50.0

LANES = 128
SUBLANES = 8
VMEM_LIMIT = 48 * 1024 * 1024

ROW_TILE = 512
A_TQ = 512
A_SUB = 128
B_TQ = ROW_TILE
B_TK = 4 * ROW_TILE
POST_SPLIT = 1
FFN_TC = 512
MOE_T = ROW_TILE
MOE_WS_SHIFT = 7
MOE_WS = 1 << MOE_WS_SHIFT
MOE_ALIGN_SHIFT = 4
MOE_ALIGN = 1 << MOE_ALIGN_SHIFT
XS_WIDTH = D_MODEL + LANES
TM_POS = N_EXPERTS
TM_GATE_LO = 2 * N_EXPERTS


def _cparams(sem):
    return pltpu.CompilerParams(dimension_semantics=sem, vmem_limit_bytes=VMEM_LIMIT)


def _full_spec(a):
    return pl.BlockSpec(a.shape, lambda *_: (0,) * a.ndim)


def _rope_tables(s):
    pos = jnp.arange(s)
    posf = pos.astype(F32)
    inv_a = jnp.power(ROPE_THETA, -jnp.arange(0, A_ROT_DIM, 2, dtype=F32) / A_ROT_DIM)
    ang = posf[:, None] * inv_a[None, :]
    ca, sa = jnp.cos(ang), jnp.sin(ang)
    z8 = jnp.zeros_like(sa)
    rest = HEAD_DIM - A_ROT_DIM
    cos_a = jnp.concatenate([ca, ca, jnp.ones((s, rest), F32)], axis=1)
    sp_a = jnp.concatenate([z8, sa, jnp.zeros((s, rest), F32)], axis=1)
    sm_a = jnp.concatenate([-sa, z8, jnp.zeros((s, rest), F32)], axis=1)

    hb = HEAD_DIM // 2
    inv_b = jnp.power(AXIAL_THETA, -jnp.arange(0, hb, 2, dtype=F32) / hb)
    row = (pos // GRID_W).astype(F32)
    col = (pos % GRID_W).astype(F32)
    ar = row[:, None] * inv_b[None, :]
    ac = col[:, None] * inv_b[None, :]
    cr, sr, cc, sc = jnp.cos(ar), jnp.sin(ar), jnp.cos(ac), jnp.sin(ac)
    z16 = jnp.zeros_like(sr)
    cos_b = jnp.concatenate([cr, cr, cc, cc], axis=1)
    sp_b = jnp.concatenate([z16, sr, z16, sc], axis=1)
    sm_b = jnp.concatenate([-sr, z16, -sc, z16], axis=1)
    heads_per_tile = LANES // HEAD_DIM
    parts = [jnp.tile(t, (1, heads_per_tile)) for t in (cos_a, sp_a, sm_a, cos_b, sp_b, sm_b)]
    return jnp.concatenate(parts, axis=1)


def _rope_chunk(x, tab_ref, base, half):
    cos = tab_ref[:, base:base + LANES]
    sp = tab_ref[:, base + LANES:base + 2 * LANES]
    sm = tab_ref[:, base + 2 * LANES:base + 3 * LANES]
    return x * cos + pltpu.roll(x, half, 1) * sp + pltpu.roll(x, LANES - half, 1) * sm


def _head_mean_sq(acc, blk_ref):
    sq = acc * acc
    hi = sq.astype(BF16)
    lo = (sq - hi.astype(F32)).astype(BF16)
    blk = blk_ref[...]
    return (jnp.dot(hi, blk, preferred_element_type=F32)
            + jnp.dot(lo, blk, preferred_element_type=F32))


def _store_by_class(out_refs, chunk_idx, chunk, stage_ref):
    g, half = divmod(chunk_idx, A_GROUP_WIDTH // LANES)
    dil = A_DILATIONS[g]
    lanes = slice(half * LANES, (half + 1) * LANES)
    if dil == 1:
        out_refs[g][0, :, lanes] = chunk.astype(BF16)
        return
    tm = chunk.shape[0]
    stage_ref[...] = chunk
    for r in range(dil):
        out_refs[g][r, :, lanes] = stage_ref[pl.ds(r, tm // dil, stride=dil), :].astype(BF16)


def _in_proj_kernel(x_ref, tab_ref, gmix_ref, wqa_ref, wka_ref, wva_ref, wqb_ref, wkb_ref, wvb_ref,
                    wg_ref, gq_ref, gk_ref, blkq_ref, blkk_ref,
                    qa0_ref, qa1_ref, qa2_ref, ka0_ref, ka1_ref, ka2_ref, va0_ref, va1_ref, va2_ref,
                    qt_ref, kb_ref, vt_ref, gates_ref, stage_ref):
    x = x_ref[...]
    tm = x.shape[0]
    ms = jnp.mean(x * x, axis=-1, keepdims=True)
    h = (x * lax.rsqrt(ms + EPS) * gmix_ref[...]).astype(BF16)

    acc = jnp.dot(h, wqa_ref[...], preferred_element_type=F32)
    for c in range(A_WIDTH // LANES):
        roped = _rope_chunk(acc[:, c * LANES:(c + 1) * LANES], tab_ref, 0, A_ROT_DIM // 2) * Q_SCALE
        _store_by_class((qa0_ref, qa1_ref, qa2_ref), c, roped, stage_ref)

    acc = jnp.dot(h, wka_ref[...], preferred_element_type=F32)
    for c in range(A_WIDTH // LANES):
        roped = _rope_chunk(acc[:, c * LANES:(c + 1) * LANES], tab_ref, 0, A_ROT_DIM // 2)
        _store_by_class((ka0_ref, ka1_ref, ka2_ref), c, roped, stage_ref)

    acc = jnp.dot(h, wva_ref[...], preferred_element_type=F32)
    for c in range(A_WIDTH // LANES):
        _store_by_class((va0_ref, va1_ref, va2_ref), c, acc[:, c * LANES:(c + 1) * LANES], stage_ref)

    acc = jnp.dot(h, wqb_ref[...], preferred_element_type=F32)
    acc = acc * lax.rsqrt(_head_mean_sq(acc, blkq_ref) + EPS) * gq_ref[...]
    heads_per_chunk = LANES // HEAD_DIM
    for c in range(B_Q_WIDTH // LANES):
        roped = _rope_chunk(acc[:, c * LANES:(c + 1) * LANES], tab_ref, 3 * LANES, HEAD_DIM // 4) * Q_SCALE
        rt = roped.T.astype(BF16)
        for hh in range(heads_per_chunk):
            head = c * heads_per_chunk + hh
            g, hg = divmod(head, B_GROUP_HEADS)
            cols = slice(hg * tm, (hg + 1) * tm)
            qt_ref[g, g * HEAD_DIM:(g + 1) * HEAD_DIM, cols] = rt[hh * HEAD_DIM:(hh + 1) * HEAD_DIM, :]
            qt_ref[g, (1 - g) * HEAD_DIM:(2 - g) * HEAD_DIM, cols] = jnp.zeros((HEAD_DIM, tm), BF16)

    acc = jnp.dot(h, wkb_ref[...], preferred_element_type=F32)
    acc = acc * lax.rsqrt(_head_mean_sq(acc, blkk_ref) + EPS) * gk_ref[...]
    kb_ref[...] = _rope_chunk(acc, tab_ref, 3 * LANES, HEAD_DIM // 4).astype(BF16)

    vt = jnp.dot(h, wvb_ref[...], preferred_element_type=F32).T.astype(BF16)
    for g in range(B_KV_HEADS):
        vt_ref[g] = vt[g * HEAD_DIM:(g + 1) * HEAD_DIM, :]

    gates_ref[...] = jax.nn.sigmoid(jnp.dot(h, wg_ref[...], preferred_element_type=F32)).astype(BF16)


def _in_proj(xf, tab, b, s, gmix, w_in, gq, gk):
    n = xf.shape[0]
    tm = ROW_TILE
    nts = s // tm
    c1, c2, c3 = A_WIDTH, 2 * A_WIDTH, 3 * A_WIDTH
    c4 = c3 + B_Q_WIDTH
    c5 = c4 + B_KV_WIDTH
    c6 = c5 + B_KV_WIDTH
    wb = w_in.astype(BF16)
    ws = [wb[:, :c1], wb[:, c1:c2], wb[:, c2:c3], wb[:, c3:c4], wb[:, c4:c5], wb[:, c5:c6], wb[:, c6:]]

    def head_blockdiag(width):
        hid = jnp.arange(width) // HEAD_DIM
        return jnp.where(hid[:, None] == hid[None, :], 1.0 / HEAD_DIM, 0.0).astype(BF16)

    consts = [gmix.reshape(1, D_MODEL)] + ws + [
        jnp.tile(gq, B_Q_WIDTH // HEAD_DIM).reshape(1, B_Q_WIDTH),
        jnp.tile(gk, B_KV_WIDTH // HEAD_DIM).reshape(1, B_KV_WIDTH),
        head_blockdiag(B_Q_WIDTH), head_blockdiag(B_KV_WIDTH)]

    a_shapes, a_specs = [], []
    for _ in range(3):
        for dil in A_DILATIONS:
            a_shapes.append(jax.ShapeDtypeStruct((b, dil, s // dil, A_GROUP_WIDTH), BF16))
            a_specs.append(pl.BlockSpec((None, dil, tm // dil, A_GROUP_WIDTH),
                                        lambda i: (i // nts, 0, i % nts, 0)))
    kpb = B_TK // tm
    b_shapes = [jax.ShapeDtypeStruct((b, B_KV_HEADS, nts, B_KV_WIDTH, B_GROUP_HEADS * tm), BF16),
                jax.ShapeDtypeStruct((n, B_KV_WIDTH), BF16),
                jax.ShapeDtypeStruct((b, B_KV_HEADS, s // B_TK, HEAD_DIM, B_TK), BF16),
                jax.ShapeDtypeStruct((n, GATE_WIDTH), BF16)]
    b_specs = [pl.BlockSpec((None, B_KV_HEADS, None, B_KV_WIDTH, B_GROUP_HEADS * tm),
                            lambda i: (i // nts, 0, i % nts, 0, 0)),
               pl.BlockSpec((tm, B_KV_WIDTH), lambda i: (i, 0)),
               pl.BlockSpec((None, B_KV_HEADS, None, HEAD_DIM, tm),
                            lambda i: (i // nts, 0, (i % nts) // kpb, 0, i % kpb)),
               pl.BlockSpec((tm, GATE_WIDTH), lambda i: (i, 0))]

    return pl.pallas_call(
        _in_proj_kernel,
        out_shape=a_shapes + b_shapes,
        grid=(n // tm,),
        in_specs=[pl.BlockSpec((tm, D_MODEL), lambda i: (i, 0)),
                  pl.BlockSpec((tm, tab.shape[1]), lambda i: (i % nts, 0))] + [_full_spec(a) for a in consts],
        out_specs=a_specs + b_specs,
        scratch_shapes=[pltpu.VMEM((tm, LANES), F32)],
        compiler_params=_cparams(("parallel",)),
        name="in_proj",
    )(xf, tab, *consts)


def _band_bias():
    qi = jnp.arange(A_SUB)[:, None]
    kj = jnp.arange(A_SUB + 2 * A_RADIUS)[None, :]
    band = jnp.abs(kj - A_RADIUS - qi) <= A_RADIUS
    first = band & (kj >= A_RADIUS)
    last = band & (kj < A_SUB + A_RADIUS)
    return jnp.where(jnp.stack([band, first, last]), 0.0, NEG_INF).astype(F32)


def _attn_a_kernel(bias_ref, q_ref, kp_ref, kc_ref, kn_ref, vp_ref, vc_ref, vn_ref, o_ref, lse_ref):
    tq = q_ref.shape[0]
    nsub = tq // A_SUB
    i = pl.program_id(2)
    last_tile = pl.num_programs(2) - 1
    first_head = lax.broadcasted_iota(jnp.int32, (1, LANES), 1) < HEAD_DIM
    nt_dims = (((1,), (1,)), ((), ()))

    def window(p_ref, c_ref, n_ref, u, cols):
        lo, hi = u * A_SUB - A_RADIUS, (u + 1) * A_SUB + A_RADIUS
        parts = [p_ref[:, cols]] if lo < 0 else []
        parts.append(c_ref[max(lo, 0):min(hi, tq), cols])
        if hi > tq:
            parts.append(n_ref[:, cols])
        return parts[0] if len(parts) == 1 else jnp.concatenate(parts, axis=0)

    for u in range(nsub):
        if u == 0:
            bias = bias_ref[jnp.where(i == 0, 1, 0)]
        elif u == nsub - 1:
            bias = bias_ref[jnp.where(i == last_tile, 2, 0)]
        else:
            bias = bias_ref[0]
        rows = slice(u * A_SUB, (u + 1) * A_SUB)
        for pair in range(A_GROUP_WIDTH // LANES):
            cols = slice(pair * LANES, (pair + 1) * LANES)
            k = window(kp_ref, kc_ref, kn_ref, u, cols)
            v = window(vp_ref, vc_ref, vn_ref, u, cols)
            q = q_ref[rows, cols]
            outs, lses = [], []
            for hh in range(LANES // HEAD_DIM):
                mine = first_head if hh == 0 else jnp.logical_not(first_head)
                qm = jnp.where(mine, q, jnp.zeros_like(q))
                sc = lax.dot_general(qm, k, nt_dims, preferred_element_type=F32) + bias
                m = jnp.max(sc, axis=1, keepdims=True)
                p = jnp.exp2(sc - m)
                den = jnp.sum(p, axis=1, keepdims=True)
                outs.append(jnp.dot(p.astype(BF16), v, preferred_element_type=F32) / den)
                lses.append(m * LN2 + jnp.log(den))
            o_ref[rows, cols] = jnp.where(first_head, outs[0], outs[1])
            lse_ref[rows, cols] = jnp.where(first_head, lses[0], lses[1])


def _attn_a(q, k, v, gi):
    b, dil, cl, _ = q.shape
    tq = min(A_TQ, cl)
    assert tq >= 2 * A_SUB and A_SUB == 2 * A_RADIUS
    hb = tq // A_RADIUS
    n_halo = cl // A_RADIUS
    bias = _band_bias()
    main = pl.BlockSpec((None, None, tq, A_GROUP_WIDTH), lambda bi, r, i: (bi, r, i, 0))
    prev = pl.BlockSpec((None, None, A_RADIUS, A_GROUP_WIDTH),
                        lambda bi, r, i: (bi, r, jnp.maximum(i * hb - 1, 0), 0))
    nxt = pl.BlockSpec((None, None, A_RADIUS, A_GROUP_WIDTH),
                       lambda bi, r, i: (bi, r, jnp.minimum((i + 1) * hb, n_halo - 1), 0))
    return pl.pallas_call(
        _attn_a_kernel,
        out_shape=[jax.ShapeDtypeStruct((b, dil, cl, A_GROUP_WIDTH), F32)] * 2,
        grid=(b, dil, cl // tq),
        in_specs=[_full_spec(bias), main, prev, main, nxt, prev, main, nxt],
        out_specs=[main, main],
        compiler_params=_cparams(("parallel", "parallel", "parallel")),
        name=f"attn_a{gi}",
    )(bias, q, k, k, k, v, v, v)


def _flash_b_kernel(qt_ref, k_ref, vt_ref, o_ref, acc_sc, l_sc, m_sc, p0_sc, p1_sc, *, online):
    nk, tk, _ = k_ref.shape
    width = qt_ref.shape[1]
    tq = width // B_GROUP_HEADS
    acc_sc[...] = jnp.zeros(acc_sc.shape, F32)
    l_sc[...] = jnp.zeros(l_sc.shape, F32)

    def scores(j):
        return jnp.dot(k_ref[j], qt_ref[...], preferred_element_type=F32)

    def column_sums(p):
        return p.reshape(tk // SUBLANES, SUBLANES, width).sum(axis=0)

    def accumulate(j, pb):
        vt = vt_ref[j]
        for h in range(B_GROUP_HEADS):
            cols = slice(h * tq, (h + 1) * tq)
            acc_sc[:, cols] += jnp.dot(vt, pb[:, cols], preferred_element_type=F32)

    if online:
        m_sc[...] = jnp.full(m_sc.shape, NEG_INF, F32)

        def body(j, carry):
            sc = scores(j)
            m_old = m_sc[...]
            m_new = jnp.maximum(m_old, jnp.max(sc, axis=0, keepdims=True))
            alpha = jnp.exp2(m_old - m_new)
            m_sc[...] = m_new
            p = jnp.exp2(sc - m_new)
            l_sc[...] = alpha * l_sc[...] + column_sums(p)
            acc_sc[...] = alpha * acc_sc[...]
            accumulate(j, p.astype(BF16))
            return carry

        lax.fori_loop(0, nk, body, 0)
    else:
        def exponentials(sc, p_out):
            p = jnp.exp2(sc)
            l_sc[...] += column_sums(p)
            p_out[...] = p.astype(BF16)

        def stage(j, p_in, p_out):
            sc = scores(j)
            accumulate(j - 1, p_in[...])
            exponentials(sc, p_out)

        exponentials(scores(0), p0_sc)

        def body(jj, carry):
            stage(2 * jj + 1, p0_sc, p1_sc)
            stage(2 * jj + 2, p1_sc, p0_sc)
            return carry

        lax.fori_loop(0, nk // 2 - 1, body, 0)
        stage(nk - 1, p0_sc, p1_sc)
        accumulate(nk - 1, p1_sc[...])
    o = acc_sc[...] / jnp.sum(l_sc[...], axis=0, keepdims=True)
    o = jnp.concatenate([o[:, h * tq:(h + 1) * tq] for h in range(B_GROUP_HEADS)], axis=0)
    o_ref[...] = o.T.astype(o_ref.dtype)


def _flash_b(qt, kb, vt, online):
    b, _, nq, _, width = qt.shape
    tq = width // B_GROUP_HEADS
    _, nk, tk, _ = kb.shape
    return pl.pallas_call(
        functools.partial(_flash_b_kernel, online=online),
        out_shape=jax.ShapeDtypeStruct((b, B_KV_HEADS, nq * tq, B_GROUP_WIDTH), BF16),
        grid=(b, B_KV_HEADS, nq),
        in_specs=[pl.BlockSpec((None, None, None, B_KV_WIDTH, width), lambda bi, g, i: (bi, g, i, 0, 0)),
                  pl.BlockSpec((None, nk, tk, B_KV_WIDTH), lambda bi, g, i: (bi, 0, 0, 0)),
                  pl.BlockSpec((None, None, nk, HEAD_DIM, tk), lambda bi, g, i: (bi, g, 0, 0, 0))],
        out_specs=pl.BlockSpec((None, None, tq, B_GROUP_WIDTH), lambda bi, g, i: (bi, g, i, 0)),
        scratch_shapes=[pltpu.VMEM((HEAD_DIM, width), F32), pltpu.VMEM((SUBLANES, width), F32),
                        pltpu.VMEM((1, width), F32),
                        pltpu.VMEM((tk, width), BF16), pltpu.VMEM((tk, width), BF16)],
        compiler_params=_cparams(("parallel", "parallel", "parallel")),
        name="flash_b_online" if online else "flash_b",
    )(qt, kb, vt)


def _stage_classes(blk_ref, stage_ref):
    dil, rows, width = blk_ref.shape
    if dil == 1:
        return
    for c in range(width // LANES):
        for r in range(dil):
            stage_ref.at[c][pl.ds(r, rows, stride=dil), :] = blk_ref[r, :, c * LANES:(c + 1) * LANES]


def _token_rows(blk_ref, stage_ref, rows):
    if blk_ref.shape[0] == 1:
        return blk_ref[0, rows, :]
    return jnp.concatenate([stage_ref[c, rows, :] for c in range(stage_ref.shape[0])], axis=1)


def _post_attn_kernel(x_ref, o0_ref, o1_ref, o2_ref, l0_ref, l1_ref, l2_ref, ob0_ref, ob1_ref, gates_ref,
                      wa_ref, wb0_ref, wb1_ref, wout_ref, gffn_ref, wrh_ref, wrl_ref,
                      x1_ref, h2_ref, aff_ref, so1, so2, sl1, sl2):
    tm = x_ref.shape[0]
    for blk, stage in ((l1_ref, sl1), (l2_ref, sl2), (o1_ref, so1), (o2_ref, so2)):
        _stage_classes(blk, stage)
    nt = (((1,), (1,)), ((), ()))
    hr = tm // POST_SPLIT
    for half in range(POST_SPLIT):
        rows = slice(half * hr, (half + 1) * hr)
        l0 = _token_rows(l0_ref, None, rows)
        l1 = _token_rows(l1_ref, sl1, rows)
        l2 = _token_rows(l2_ref, sl2, rows)
        m = jnp.maximum(jnp.maximum(l0, l1), l2)
        e0, e1, e2 = jnp.exp(l0 - m), jnp.exp(l1 - m), jnp.exp(l2 - m)
        oa = (e0 * _token_rows(o0_ref, None, rows) + e1 * _token_rows(o1_ref, so1, rows)
              + e2 * _token_rows(o2_ref, so2, rows)) / (e0 + e1 + e2)
        ya = jnp.dot(oa.astype(BF16), wa_ref[...], preferred_element_type=F32)
        yb = (jnp.dot(ob0_ref[rows, :], wb0_ref[...], preferred_element_type=F32)
              + jnp.dot(ob1_ref[rows, :], wb1_ref[...], preferred_element_type=F32))
        ga = gates_ref[rows, :D_MODEL].astype(F32)
        gb = gates_ref[rows, D_MODEL:].astype(F32)
        z = (ga * ya + gb * yb).astype(BF16)
        x1 = x_ref[rows, :] + jnp.dot(z, wout_ref[...], preferred_element_type=F32)
        x1_ref[rows, :] = x1
        ms = jnp.mean(x1 * x1, axis=-1, keepdims=True)
        h2 = x1 * lax.rsqrt(ms + EPS) * gffn_ref[...]
        hi = h2.astype(BF16)
        lo = (h2 - hi.astype(F32)).astype(BF16)
        h2_ref[rows, :] = hi
        both = lax.dot_general(wrl_ref[...], hi, nt, preferred_element_type=F32)
        logits = (both[:N_EXPERTS] + both[N_EXPERTS:]
                  + lax.dot_general(wrh_ref[...], lo, nt, preferred_element_type=F32))
        logits = logits - jnp.max(logits, axis=0, keepdims=True)
        e = jnp.exp(logits)
        aff = e / jnp.sum(e, axis=0, keepdims=True)
        for j in range(hr // LANES):
            aff_ref[half * (hr // LANES) + j] = aff[:, j * LANES:(j + 1) * LANES]


def _post_attn(xf, oas, lses, ob, gates, b, s, w_a, w_b, w_out, g_ffn, w_router):
    n = xf.shape[0]
    tm = ROW_TILE
    nts = s // tm
    wa = w_a.astype(BF16)
    wb = w_b.astype(BF16)
    wb0, wb1 = wb[:B_GROUP_WIDTH], wb[B_GROUP_WIDTH:]
    wout = w_out.astype(BF16)
    wrt = w_router.T
    wrh = wrt.astype(BF16)
    wrl = jnp.concatenate([wrh, (wrt - wrh.astype(F32)).astype(BF16)], axis=0)
    consts = [wa, wb0, wb1, wout, g_ffn.reshape(1, D_MODEL), wrh, wrl]

    def rows(w):
        return pl.BlockSpec((tm, w), lambda i: (i, 0))

    def class_spec(dil):
        return pl.BlockSpec((None, dil, tm // dil, A_GROUP_WIDTH), lambda i: (i // nts, 0, i % nts, 0))

    def ob_spec(g):
        return pl.BlockSpec((None, None, tm, B_GROUP_WIDTH), lambda i: (i // nts, g, i % nts, 0))

    a_specs = [class_spec(d) for d in A_DILATIONS]
    return pl.pallas_call(
        _post_attn_kernel,
        out_shape=[jax.ShapeDtypeStruct((n, D_MODEL), F32),
                   jax.ShapeDtypeStruct((n, D_MODEL), BF16),
                   jax.ShapeDtypeStruct((n // LANES, N_EXPERTS, LANES), F32)],
        grid=(n // tm,),
        in_specs=[rows(D_MODEL)] + a_specs + a_specs + [ob_spec(0), ob_spec(1), rows(GATE_WIDTH)]
                 + [_full_spec(a) for a in consts],
        out_specs=[rows(D_MODEL), rows(D_MODEL),
                   pl.BlockSpec((tm // LANES, N_EXPERTS, LANES), lambda i: (i, 0, 0))],
        scratch_shapes=[pltpu.VMEM((A_GROUP_WIDTH // LANES, tm, LANES), F32)] * 4,
        compiler_params=_cparams(("parallel",)),
        name="post_attn",
    )(xf, *oas, *lses, ob, ob, gates, *consts)


def _route_kernel(aff_ref, tri_ref, posl_ref, starts_ref, sel_sc, *, cap, idx_bits, tiles_per_block):
    nlt = aff_ref.shape[0]
    capf = float(cap)

    def count(mask):
        part = jnp.sum(jnp.where(mask, 1.0, 0.0), axis=0)
        return jnp.sum(part, axis=1, keepdims=True)[None]

    def value_body(i, tau):
        bits = pltpu.bitcast(aff_ref[...], jnp.int32)
        cand = tau | lax.shift_left(jnp.int32(1), 30 - i)
        return jnp.where(count(bits >= cand) >= capf, cand, tau)

    tau = lax.fori_loop(0, 31, value_body, jnp.zeros((1, N_EXPERTS, 1), jnp.int32))
    bits = pltpu.bitcast(aff_ref[...], jnp.int32)
    need = capf - count(bits > tau)
    tok = (lax.broadcasted_iota(jnp.int32, bits.shape, 0) * LANES
           + lax.broadcasted_iota(jnp.int32, bits.shape, 2))

    def index_body(i, last):
        b = pltpu.bitcast(aff_ref[...], jnp.int32)
        cand = last | lax.shift_left(jnp.int32(1), idx_bits - 1 - i)
        return jnp.where(count((b == tau) & (tok < cand)) < need, cand, last)

    last = lax.fori_loop(0, idx_bits, index_body, jnp.zeros((1, N_EXPERTS, 1), jnp.int32))
    sel_sc[...] = jnp.where((bits > tau) | ((bits == tau) & (tok <= last)), 1.0, 0.0)

    sel = sel_sc[...]
    sb = sel.reshape(nlt * N_EXPERTS, LANES).astype(BF16)
    tri = tri_ref[...]
    within = jnp.dot(sb, tri, preferred_element_type=F32).reshape(nlt, N_EXPERTS, LANES)
    total = jnp.dot(sb, jnp.ones((LANES, LANES), BF16),
                    preferred_element_type=F32).reshape(nlt, N_EXPERTS, LANES)
    upto = total
    step = 1
    while step < nlt:
        upto = upto + jnp.concatenate([jnp.zeros((step, N_EXPERTS, LANES), F32), upto[:nlt - step]], axis=0)
        step *= 2
    before = upto - total
    posl_ref[...] = jnp.where(sel > 0, before + within - sel, -1.0).astype(jnp.int32)
    starts_ref[...] = before.reshape(nlt // tiles_per_block, tiles_per_block, N_EXPERTS, LANES)[:, 0].astype(jnp.int32)


def _route(aff3, cap):
    nlt = aff3.shape[0]
    n = nlt * LANES
    tpb = MOE_T // LANES
    tri = (jnp.arange(LANES)[:, None] <= jnp.arange(LANES)[None, :]).astype(BF16)
    vm = pl.BlockSpec(memory_space=pltpu.VMEM)
    return pl.pallas_call(
        functools.partial(_route_kernel, cap=cap, idx_bits=(n - 1).bit_length(), tiles_per_block=tpb),
        out_shape=[jax.ShapeDtypeStruct((nlt, N_EXPERTS, LANES), jnp.int32),
                   jax.ShapeDtypeStruct((n // MOE_T, N_EXPERTS, LANES), jnp.int32)],
        in_specs=[vm, vm],
        out_specs=[vm, vm],
        scratch_shapes=[pltpu.VMEM((nlt, N_EXPERTS, LANES), F32)],
        compiler_params=pltpu.CompilerParams(vmem_limit_bytes=VMEM_LIMIT),
        name="route",
    )(aff3, tri)


def _block_slots(s_ref, i, nt):
    return [(s_ref[e * (nt + 1) + i], s_ref[e * (nt + 1) + i + 1]) for e in range(N_EXPERTS)]


def _align_down(x):
    return (x >> MOE_ALIGN_SHIFT) << MOE_ALIGN_SHIFT


def _moe_gather_kernel(s_ref, aff_ref, posl_ref, h2_ref, xs_hbm, tm_ref,
                       haug_sc, pall_sc, stage_sc, carry_sc, pending_sc, sem, *, nt, cap):
    i = pl.program_id(0)
    nj = posl_ref.shape[0]
    ws = MOE_WS

    @pl.when(i == 0)
    def _():
        carry_sc[...] = jnp.zeros(carry_sc.shape, BF16)
        pending_sc[0] = 0
        pending_sc[1] = 0
        pad = xs_hbm.shape[1] - cap
        stage_sc[0] = jnp.zeros(stage_sc.shape[1:], BF16)
        fills = [pltpu.make_async_copy(stage_sc.at[0, pl.ds(0, pad)], xs_hbm.at[e, pl.ds(cap, pad)], sem)
                 for e in range(N_EXPERTS)]
        for cp in fills:
            cp.start()
        for cp in fills:
            cp.wait()

    for j in range(nj):
        pj = posl_ref[j]
        gj = jnp.where(pj >= 0, aff_ref[j], 0.0)
        blk = jnp.concatenate([gj, pj.astype(F32), gj, jnp.zeros((LANES - 3 * N_EXPERTS, LANES), F32)], axis=0)
        tm_ref[j * LANES:(j + 1) * LANES, :] = blk.T
    tmv = tm_ref[...]
    lane = lax.broadcasted_iota(jnp.int32, tmv.shape, 1)
    ghi = tmv.astype(BF16)
    glo = (tmv - ghi.astype(F32)).astype(BF16)
    zero = jnp.zeros_like(ghi)
    haug_sc[:, :D_MODEL] = h2_ref[...]
    haug_sc[:, D_MODEL:] = jnp.where(lane < N_EXPERTS, ghi,
                                     jnp.where((lane >= TM_GATE_LO) & (lane < TM_GATE_LO + N_EXPERTS), glo, zero))

    astarts, kks, offcs = [], [], []
    nw = jnp.int32(0)
    for st, en in _block_slots(s_ref, i, nt):
        ast = _align_down(st)
        span = _align_down(en) - ast
        kk = span >> MOE_WS_SHIFT
        astarts.append(ast)
        kks.append(kk)
        offcs.append(span - kk * ws)
        nw = jnp.maximum(nw, kk + 1)

    def wait_window():
        for _ in range(N_EXPERTS):
            pltpu.make_async_copy(stage_sc.at[0, pl.ds(0, ws)], xs_hbm.at[0, pl.ds(0, ws)], sem).wait()

    row = lax.broadcasted_iota(jnp.int32, (ws, LANES), 0)

    def window(k, carry):
        for e in range(N_EXPERTS):
            base = astarts[e] + k * ws
            for j in range(nj):
                hit = (posl_ref[j, e:e + 1, :] - base) == row
                pall_sc[e * ws:(e + 1) * ws, j * LANES:(j + 1) * LANES] = jnp.where(hit, 1.0, 0.0).astype(BF16)
        stage = stage_sc.at[pending_sc[1]]
        stage[...] = jnp.dot(pall_sc[...], haug_sc[...], preferred_element_type=F32).astype(BF16)

        @pl.when(k == 0)
        def _():
            for e in range(N_EXPERTS):
                stage[e * ws:e * ws + MOE_ALIGN, :] += carry_sc[e]

        for e in range(N_EXPERTS):
            @pl.when(k == kks[e])
            def _():
                carry_sc[e] = stage[pl.ds(pl.multiple_of(e * ws + offcs[e], MOE_ALIGN), MOE_ALIGN), :]

        @pl.when(pending_sc[0] == 1)
        def _():
            wait_window()

        for e in range(N_EXPERTS):
            base = pl.multiple_of(astarts[e] + k * ws, MOE_ALIGN)
            pltpu.make_async_copy(stage.at[pl.ds(e * ws, ws)], xs_hbm.at[e, pl.ds(base, ws)], sem).start()
        pending_sc[0] = 1
        pending_sc[1] = 1 - pending_sc[1]
        return carry

    lax.fori_loop(0, nw, window, 0)

    @pl.when((i == nt - 1) & (pending_sc[0] == 1))
    def _():
        wait_window()
        pending_sc[0] = 0


def _moe_gather(s_flat, aff3, posl, h2, cap):
    n = h2.shape[0]
    t = MOE_T
    nt = n // t
    cap_pad = cap + (t // MOE_WS + 1) * MOE_WS
    blk3 = pl.BlockSpec((t // LANES, N_EXPERTS, LANES), lambda i, s: (i, 0, 0))
    grid_spec = pltpu.PrefetchScalarGridSpec(
        num_scalar_prefetch=1,
        grid=(nt,),
        in_specs=[blk3, blk3, pl.BlockSpec((t, D_MODEL), lambda i, s: (i, 0))],
        out_specs=[pl.BlockSpec(memory_space=pl.ANY), pl.BlockSpec((t, LANES), lambda i, s: (i, 0))],
        scratch_shapes=[pltpu.VMEM((t, XS_WIDTH), BF16),
                        pltpu.VMEM((N_EXPERTS * MOE_WS, t), BF16),
                        pltpu.VMEM((2, N_EXPERTS * MOE_WS, XS_WIDTH), BF16),
                        pltpu.VMEM((N_EXPERTS, MOE_ALIGN, XS_WIDTH), BF16),
                        pltpu.SMEM((2,), jnp.int32),
                        pltpu.SemaphoreType.DMA(())])
    return pl.pallas_call(
        functools.partial(_moe_gather_kernel, nt=nt, cap=cap),
        out_shape=[jax.ShapeDtypeStruct((N_EXPERTS, cap_pad, XS_WIDTH), BF16),
                   jax.ShapeDtypeStruct((n, LANES), F32)],
        grid_spec=grid_spec,
        compiler_params=_cparams(("arbitrary",)),
        name="moe_gather",
    )(s_flat, aff3, posl, h2)


def _moe_ffn_kernel(xs_ref, w1_ref, w3_ref, w2_ref, ye_ref):
    xe = xs_ref[:, :D_MODEL]
    route = xs_ref[:, D_MODEL:].astype(F32)
    lane = lax.broadcasted_iota(jnp.int32, route.shape, 1)
    e = pl.program_id(0)
    gate = jnp.sum(jnp.where((lane == e) | (lane == e + TM_GATE_LO), route, 0.0), axis=1, keepdims=True)
    a = jnp.dot(xe, w1_ref[...], preferred_element_type=F32)
    u = jnp.dot(xe, w3_ref[...], preferred_element_type=F32)
    hid = (a * jax.nn.sigmoid(a) * u).astype(BF16)
    ye = jnp.dot(hid, w2_ref[...], preferred_element_type=F32) * gate
    ye_ref[...] = ye.astype(ye_ref.dtype)


def _moe_ffn(xs, cap, w1, w3, w2):
    e = xs.shape[0]
    tc = min(FFN_TC, cap)
    wspec = pl.BlockSpec((None, D_MODEL, D_MODEL), lambda ei, ci: (ei, 0, 0))
    return pl.pallas_call(
        _moe_ffn_kernel,
        out_shape=jax.ShapeDtypeStruct((e, cap, D_MODEL), BF16),
        grid=(e, cap // tc),
        in_specs=[pl.BlockSpec((None, tc, XS_WIDTH), lambda ei, ci: (ei, ci, 0)), wspec, wspec, wspec],
        out_specs=pl.BlockSpec((None, tc, D_MODEL), lambda ei, ci: (ei, ci, 0)),
        compiler_params=_cparams(("parallel", "arbitrary")),
        name="moe_ffn",
    )(xs, w1, w3, w2)


def _final_kernel(s_ref, x1_ref, tm_ref, p_ref, ye_hbm, wple_ref, wpg_ref, gple_ref, gfin_ref, expand_ref, out_ref,
                  buf_sc, pall_sc, y_sc, sem, *, nt, cap):
    i = pl.program_id(0)
    ws = MOE_WS
    t = x1_ref.shape[0]
    slot = i % 2

    def window_rows(blk, k):
        out = []
        for st, _ in _block_slots(s_ref, blk, nt):
            lo = _align_down(st) + k * ws
            out.append((lo, pl.multiple_of(jnp.minimum(lo, cap - ws), MOE_ALIGN)))
        return out

    def copies(rows, buf, which):
        return [pltpu.make_async_copy(ye_hbm.at[e, pl.ds(base, ws)], buf_sc.at[buf, pl.ds(e * ws, ws)], sem.at[which])
                for e, (_, base) in enumerate(rows)]

    @pl.when(i == 0)
    def _():
        for cp in copies(window_rows(0, 0), 0, 0):
            cp.start()

    @pl.when(i + 1 < nt)
    def _():
        for cp in copies(window_rows(i + 1, 0), 1 - slot, 1 - slot):
            cp.start()

    nw = jnp.int32(0)
    for st, en in _block_slots(s_ref, i, nt):
        nw = jnp.maximum(nw, (en - _align_down(st) + ws - 1) >> MOE_WS_SHIFT)
    lane = lax.broadcasted_iota(jnp.int32, (1, LANES), 1)
    col = (lax.broadcasted_iota(jnp.int32, (1, N_EXPERTS * ws), 1) & (ws - 1)).astype(F32)

    def onehot(rows):
        basev = jnp.zeros((1, LANES), F32)
        lov = jnp.full((1, LANES), float(2 ** 30), F32)
        for e, (lo, base) in enumerate(rows):
            basev = jnp.where(lane == TM_POS + e, base.astype(F32), basev)
            lov = jnp.where(lane == TM_POS + e, lo.astype(F32), lov)
        tmv = tm_ref[...]
        off = tmv - basev
        off = jnp.where((tmv >= lov) & (off >= 0.0) & (off < float(ws)), off, -1.0)
        spread = jnp.dot(off.astype(BF16), expand_ref[...], preferred_element_type=F32)
        pall_sc[...] = jnp.where(spread == col, 1.0, 0.0).astype(BF16)

    rows0 = window_rows(i, 0)
    onehot(rows0)
    for cp in copies(rows0, slot, slot):
        cp.wait()
    y_sc[...] = jnp.dot(pall_sc[...], buf_sc[slot], preferred_element_type=F32)

    def window(k, carry):
        rows = window_rows(i, k)
        cps = copies(rows, slot, slot)
        for cp in cps:
            cp.start()
        onehot(rows)
        for cp in cps:
            cp.wait()
        y_sc[...] += jnp.dot(pall_sc[...], buf_sc[slot], preferred_element_type=F32)
        return carry

    lax.fori_loop(1, nw, window, 0)

    hr = t // POST_SPLIT
    for half in range(POST_SPLIT):
        rows = slice(half * hr, (half + 1) * hr)
        x2 = x1_ref[rows, :] + y_sc[rows, :]
        ms = jnp.mean(x2 * x2, axis=-1, keepdims=True)
        hp = (x2 * lax.rsqrt(ms + EPS) * gple_ref[...]).astype(BF16)
        gt = jax.nn.sigmoid(jnp.dot(hp, wpg_ref[...], preferred_element_type=F32))
        emb = jnp.dot(p_ref[rows, :].astype(BF16), wple_ref[...], preferred_element_type=F32)
        x3 = x2 + emb * gt
        ms3 = jnp.mean(x3 * x3, axis=-1, keepdims=True)
        out_ref[rows, :] = x3 * lax.rsqrt(ms3 + EPS) * gfin_ref[...]


def _final(s_flat, x1, tm, pf, ye, w_ple, w_ple_gate, g_ple, g_final):
    n = x1.shape[0]
    t = MOE_T
    nt = n // t
    cap = ye.shape[1]
    src = jnp.arange(LANES)[:, None] - TM_POS
    dst = jnp.arange(N_EXPERTS * MOE_WS)[None, :] >> MOE_WS_SHIFT
    expand = (src == dst).astype(BF16)
    consts = [w_ple.astype(BF16), w_ple_gate.astype(BF16), g_ple.reshape(1, D_MODEL), g_final.reshape(1, D_MODEL),
              expand]

    def rows(w):
        return pl.BlockSpec((t, w), lambda i, s: (i, 0))

    grid_spec = pltpu.PrefetchScalarGridSpec(
        num_scalar_prefetch=1,
        grid=(nt,),
        in_specs=[rows(D_MODEL), rows(LANES), rows(PLE_DIM), pl.BlockSpec(memory_space=pl.ANY)]
                 + [_full_spec(a) for a in consts],
        out_specs=rows(D_MODEL),
        scratch_shapes=[pltpu.VMEM((2, N_EXPERTS * MOE_WS, D_MODEL), BF16),
                        pltpu.VMEM((t, N_EXPERTS * MOE_WS), BF16),
                        pltpu.VMEM((t, D_MODEL), F32),
                        pltpu.SemaphoreType.DMA((2,))])
    return pl.pallas_call(
        functools.partial(_final_kernel, nt=nt, cap=cap),
        out_shape=jax.ShapeDtypeStruct((n, D_MODEL), F32),
        grid_spec=grid_spec,
        compiler_params=_cparams(("arbitrary",)),
        name="final",
    )(s_flat, x1, tm, pf, ye, *consts)


def _run_group(x, p, g_mix, w_in, g_qn, g_kn, w_a_proj, w_b_proj, w_out, g_ffn, w_router,
               w1, w3, w2, g_ple, w_ple, w_ple_gate, g_final):
    b, s, _ = x.shape
    n = b * s
    assert s % B_TK == 0 and s % (A_DILATIONS[-1] * A_RADIUS) == 0
    xf = x.reshape(n, D_MODEL)
    tab = _rope_tables(s)
    outs = _in_proj(xf, tab, b, s, g_mix, w_in, g_qn, g_kn)
    qas, kas, vas = outs[0:3], outs[3:6], outs[6:9]
    qt, kb, vt, gates = outs[9:]

    oas, lses = [], []
    for gi, (window, dil) in enumerate(A_PATTERNS):
        assert window // (2 * dil) == A_RADIUS
        o, lse = _attn_a(qas[gi], kas[gi], vas[gi], gi)
        oas.append(o)
        lses.append(lse)

    score_bound = (HEAD_DIM * Q_SCALE * 1.02) * jnp.max(jnp.abs(g_qn)) * jnp.max(jnp.abs(g_kn))
    kb4 = kb.reshape(b, s // B_TK, B_TK, B_KV_WIDTH)
    ob = lax.cond(score_bound <= B_UNSHIFTED_SCORE_MAX,
                  lambda: _flash_b(qt, kb4, vt, online=False),
                  lambda: _flash_b(qt, kb4, vt, online=True))

    x1, h2, aff3 = _post_attn(xf, oas, lses, ob, gates, b, s, w_a_proj, w_b_proj, w_out, g_ffn, w_router)

    cap = CAPACITY_FACTOR * n // N_EXPERTS
    assert cap % MOE_WS == 0 and cap >= MOE_WS
    posl, starts = _route(aff3, cap)
    s_flat = jnp.concatenate([starts[:, :, 0].T, jnp.full((N_EXPERTS, 1), cap, jnp.int32)], axis=1).reshape(-1)
    xs, tm = _moe_gather(s_flat, aff3, posl, h2, cap)
    ye = _moe_ffn(xs, cap, w1, w3, w2)
    out = _final(s_flat, x1, tm, p.reshape(n, PLE_DIM), ye, w_ple, w_ple_gate, g_ple, g_final)
    return out.reshape(b, s, D_MODEL)


def kernel(x_prompt, x_sample, p_prompt, p_sample, g_mix, w_in, g_qn, g_kn, w_a_proj, w_b_proj, w_out, g_ffn,
           w_router, w_exp_gate, w_exp_up, w_exp_down, g_ple, w_ple, w_ple_gate, g_final):
    assert g_mix.shape[0] == 1, "single layer"
    w1 = w_exp_gate[0].astype(BF16)
    w3 = w_exp_up[0].astype(BF16)
    w2 = w_exp_down[0].astype(BF16)
    args = (g_mix[0], w_in[0], g_qn[0], g_kn[0], w_a_proj[0], w_b_proj[0], w_out[0], g_ffn[0], w_router[0],
            w1, w3, w2, g_ple[0], w_ple[0], w_ple_gate[0], g_final)
    y_prompt = _run_group(x_prompt, p_prompt[0], *args)
    y_sample = _run_group(x_sample, p_sample[0], *args)
    return (y_prompt, y_sample)
```

```python
import functools
import math

import jax
import jax.numpy as jnp
from jax import lax
from jax.experimental import pallas as pl
from jax.experimental.pallas import tpu as pltpu

F32 = jnp.float32
BF16 = jnp.bfloat16

D_MODEL = 1024
HEAD_DIM = 64
A_PATTERNS = ((128, 1), (512, 4), (2048, 16))
A_DILATIONS = tuple(d for _, d in A_PATTERNS)
A_GROUPS = len(A_PATTERNS)
A_HEADS_PER_GROUP = 4
A_WIDTH = 768
A_GROUP_WIDTH = A_HEADS_PER_GROUP * HEAD_DIM
A_ROT_DIM = 16
A_RADIUS = 64
ROPE_THETA = 500000.0
B_Q_WIDTH = 512
B_KV_HEADS = 2
B_KV_WIDTH = 128
B_GROUP_HEADS = 4
B_GROUP_WIDTH = B_GROUP_HEADS * HEAD_DIM
AXIAL_THETA = 10000.0
GRID_W = 64
GATE_WIDTH = 2048
N_EXPERTS = 16
CAPACITY_FACTOR = 2
PLE_DIM = 256
EPS = 1e-6
NEG_INF = -1e30
LN2 = math.log(2.0)
Q_SCALE = (HEAD_DIM ** -0.5) / LN2
B_UNSHIFTED_SCORE_MAX = 50.0

LANES = 128
SUBLANES = 8
VMEM_LIMIT = 48 * 1024 * 1024

ROW_TILE = 512
A_TQ = 1024
A_SUB = 128
B_TQ = ROW_TILE
B_TK = 4 * ROW_TILE
POST_SPLIT = 1
FFN_TC = 1024
MOE_T = ROW_TILE
MOE_WS_SHIFT = 7
MOE_WS = 1 << MOE_WS_SHIFT
MOE_ALIGN_SHIFT = 4
MOE_ALIGN = 1 << MOE_ALIGN_SHIFT
XS_WIDTH = D_MODEL + LANES
TM_POS = N_EXPERTS
TM_GATE_LO = 2 * N_EXPERTS


def _cparams(sem):
    return pltpu.CompilerParams(dimension_semantics=sem, vmem_limit_bytes=VMEM_LIMIT)


def _full_spec(a):
    return pl.BlockSpec(a.shape, lambda *_: (0,) * a.ndim)


def _rope_tables(s):
    pos = jnp.arange(s)
    posf = pos.astype(F32)
    inv_a = jnp.power(ROPE_THETA, -jnp.arange(0, A_ROT_DIM, 2, dtype=F32) / A_ROT_DIM)
    ang = posf[:, None] * inv_a[None, :]
    ca, sa = jnp.cos(ang), jnp.sin(ang)
    z8 = jnp.zeros_like(sa)
    rest = HEAD_DIM - A_ROT_DIM
    cos_a = jnp.concatenate([ca, ca, jnp.ones((s, rest), F32)], axis=1)
    sp_a = jnp.concatenate([z8, sa, jnp.zeros((s, rest), F32)], axis=1)
    sm_a = jnp.concatenate([-sa, z8, jnp.zeros((s, rest), F32)], axis=1)

    hb = HEAD_DIM // 2
    inv_b = jnp.power(AXIAL_THETA, -jnp.arange(0, hb, 2, dtype=F32) / hb)
    row = (pos // GRID_W).astype(F32)
    col = (pos % GRID_W).astype(F32)
    ar = row[:, None] * inv_b[None, :]
    ac = col[:, None] * inv_b[None, :]
    cr, sr, cc, sc = jnp.cos(ar), jnp.sin(ar), jnp.cos(ac), jnp.sin(ac)
    z16 = jnp.zeros_like(sr)
    cos_b = jnp.concatenate([cr, cr, cc, cc], axis=1)
    sp_b = jnp.concatenate([z16, sr, z16, sc], axis=1)
    sm_b = jnp.concatenate([-sr, z16, -sc, z16], axis=1)
    heads_per_tile = LANES // HEAD_DIM
    parts = [jnp.tile(t, (1, heads_per_tile)) for t in (cos_a, sp_a, sm_a, cos_b, sp_b, sm_b)]
    return jnp.concatenate(parts, axis=1)


def _rope_chunk(x, tab_ref, base, half):
    cos = tab_ref[:, base:base + LANES]
    sp = tab_ref[:, base + LANES:base + 2 * LANES]
    sm = tab_ref[:, base + 2 * LANES:base + 3 * LANES]
    return x * cos + pltpu.roll(x, half, 1) * sp + pltpu.roll(x, LANES - half, 1) * sm


def _head_mean_sq(acc, blk_ref):
    sq = acc * acc
    hi = sq.astype(BF16)
    lo = (sq - hi.astype(F32)).astype(BF16)
    blk = blk_ref[...]
    return (jnp.dot(hi, blk, preferred_element_type=F32)
            + jnp.dot(lo, blk, preferred_element_type=F32))


def _store_by_class(out_refs, chunk_idx, chunk, stage_ref):
    g, half = divmod(chunk_idx, A_GROUP_WIDTH // LANES)
    dil = A_DILATIONS[g]
    lanes = slice(half * LANES, (half + 1) * LANES)
    if dil == 1:
        out_refs[g][0, :, lanes] = chunk.astype(BF16)
        return
    tm = chunk.shape[0]
    stage_ref[...] = chunk
    for r in range(dil):
        out_refs[g][r, :, lanes] = stage_ref[pl.ds(r, tm // dil, stride=dil), :].astype(BF16)


def _in_proj_kernel(x_ref, tab_ref, gmix_ref, wqa_ref, wka_ref, wva_ref, wqb_ref, wkb_ref, wvb_ref,
                    wg_ref, gq_ref, gk_ref, blkq_ref, blkk_ref,
                    qa0_ref, qa1_ref, qa2_ref, ka0_ref, ka1_ref, ka2_ref, va0_ref, va1_ref, va2_ref,
                    qt_ref, kb_ref, vt_ref, gates_ref, stage_ref):
    x = x_ref[...]
    tm = x.shape[0]
    ms = jnp.mean(x * x, axis=-1, keepdims=True)
    h = (x * lax.rsqrt(ms + EPS) * gmix_ref[...]).astype(BF16)

    acc = jnp.dot(h, wqa_ref[...], preferred_element_type=F32)
    for c in range(A_WIDTH // LANES):
        roped = _rope_chunk(acc[:, c * LANES:(c + 1) * LANES], tab_ref, 0, A_ROT_DIM // 2) * Q_SCALE
        _store_by_class((qa0_ref, qa1_ref, qa2_ref), c, roped, stage_ref)

    acc = jnp.dot(h, wka_ref[...], preferred_element_type=F32)
    for c in range(A_WIDTH // LANES):
        roped = _rope_chunk(acc[:, c * LANES:(c + 1) * LANES], tab_ref, 0, A_ROT_DIM // 2)
        _store_by_class((ka0_ref, ka1_ref, ka2_ref), c, roped, stage_ref)

    acc = jnp.dot(h, wva_ref[...], preferred_element_type=F32)
    for c in range(A_WIDTH // LANES):
        _store_by_class((va0_ref, va1_ref, va2_ref), c, acc[:, c * LANES:(c + 1) * LANES], stage_ref)

    acc = jnp.dot(h, wqb_ref[...], preferred_element_type=F32)
    acc = acc * lax.rsqrt(_head_mean_sq(acc, blkq_ref) + EPS) * gq_ref[...]
    heads_per_chunk = LANES // HEAD_DIM
    for c in range(B_Q_WIDTH // LANES):
        roped = _rope_chunk(acc[:, c * LANES:(c + 1) * LANES], tab_ref, 3 * LANES, HEAD_DIM // 4) * Q_SCALE
        rt = roped.T.astype(BF16)
        for hh in range(heads_per_chunk):
            head = c * heads_per_chunk + hh
            g, hg = divmod(head, B_GROUP_HEADS)
            cols = slice(hg * tm, (hg + 1) * tm)
            qt_ref[g, g * HEAD_DIM:(g + 1) * HEAD_DIM, cols] = rt[hh * HEAD_DIM:(hh + 1) * HEAD_DIM, :]
            qt_ref[g, (1 - g) * HEAD_DIM:(2 - g) * HEAD_DIM, cols] = jnp.zeros((HEAD_DIM, tm), BF16)

    acc = jnp.dot(h, wkb_ref[...], preferred_element_type=F32)
    acc = acc * lax.rsqrt(_head_mean_sq(acc, blkk_ref) + EPS) * gk_ref[...]
    kb_ref[...] = _rope_chunk(acc, tab_ref, 3 * LANES, HEAD_DIM // 4).astype(BF16)

    vt = jnp.dot(h, wvb_ref[...], preferred_element_type=F32).T.astype(BF16)
    for g in range(B_KV_HEADS):
        vt_ref[g] = vt[g * HEAD_DIM:(g + 1) * HEAD_DIM, :]

    gates_ref[...] = jax.nn.sigmoid(jnp.dot(h, wg_ref[...], preferred_element_type=F32)).astype(BF16)


def _in_proj(xf, tab, b, s, gmix, w_in, gq, gk):
    n = xf.shape[0]
    tm = ROW_TILE
    nts = s // tm
    c1, c2, c3 = A_WIDTH, 2 * A_WIDTH, 3 * A_WIDTH
    c4 = c3 + B_Q_WIDTH
    c5 = c4 + B_KV_WIDTH
    c6 = c5 + B_KV_WIDTH
    wb = w_in.astype(BF16)
    ws = [wb[:, :c1], wb[:, c1:c2], wb[:, c2:c3], wb[:, c3:c4], wb[:, c4:c5], wb[:, c5:c6], wb[:, c6:]]

    def head_blockdiag(width):
        hid = jnp.arange(width) // HEAD_DIM
        return jnp.where(hid[:, None] == hid[None, :], 1.0 / HEAD_DIM, 0.0).astype(BF16)

    consts = [gmix.reshape(1, D_MODEL)] + ws + [
        jnp.tile(gq, B_Q_WIDTH // HEAD_DIM).reshape(1, B_Q_WIDTH),
        jnp.tile(gk, B_KV_WIDTH // HEAD_DIM).reshape(1, B_KV_WIDTH),
        head_blockdiag(B_Q_WIDTH), head_blockdiag(B_KV_WIDTH)]

    a_shapes, a_specs = [], []
    for _ in range(3):
        for dil in A_DILATIONS:
            a_shapes.append(jax.ShapeDtypeStruct((b, dil, s // dil, A_GROUP_WIDTH), BF16))
            a_specs.append(pl.BlockSpec((None, dil, tm // dil, A_GROUP_WIDTH),
                                        lambda i: (i // nts, 0, i % nts, 0)))
    kpb = B_TK // tm
    b_shapes = [jax.ShapeDtypeStruct((b, B_KV_HEADS, nts, B_KV_WIDTH, B_GROUP_HEADS * tm), BF16),
                jax.ShapeDtypeStruct((n, B_KV_WIDTH), BF16),
                jax.ShapeDtypeStruct((b, B_KV_HEADS, s // B_TK, HEAD_DIM, B_TK), BF16),
                jax.ShapeDtypeStruct((n, GATE_WIDTH), BF16)]
    b_specs = [pl.BlockSpec((None, B_KV_HEADS, None, B_KV_WIDTH, B_GROUP_HEADS * tm),
                            lambda i: (i // nts, 0, i % nts, 0, 0)),
               pl.BlockSpec((tm, B_KV_WIDTH), lambda i: (i, 0)),
               pl.BlockSpec((None, B_KV_HEADS, None, HEAD_DIM, tm),
                            lambda i: (i // nts, 0, (i % nts) // kpb, 0, i % kpb)),
               pl.BlockSpec((tm, GATE_WIDTH), lambda i: (i, 0))]

    return pl.pallas_call(
        _in_proj_kernel,
        out_shape=a_shapes + b_shapes,
        grid=(n // tm,),
        in_specs=[pl.BlockSpec((tm, D_MODEL), lambda i: (i, 0)),
                  pl.BlockSpec((tm, tab.shape[1]), lambda i: (i % nts, 0))] + [_full_spec(a) for a in consts],
        out_specs=a_specs + b_specs,
        scratch_shapes=[pltpu.VMEM((tm, LANES), F32)],
        compiler_params=_cparams(("parallel",)),
        name="in_proj",
    )(xf, tab, *consts)


def _band_bias():
    qi = jnp.arange(A_SUB)[:, None]
    kj = jnp.arange(A_SUB + 2 * A_RADIUS)[None, :]
    band = jnp.abs(kj - A_RADIUS - qi) <= A_RADIUS
    first = band & (kj >= A_RADIUS)
    last = band & (kj < A_SUB + A_RADIUS)
    return jnp.where(jnp.stack([band, first, last]), 0.0, NEG_INF).astype(F32)


def _attn_a_kernel(bias_ref, q_ref, kp_ref, kc_ref, kn_ref, vp_ref, vc_ref, vn_ref, o_ref, lse_ref):
    tq = q_ref.shape[0]
    nsub = tq // A_SUB
    i = pl.program_id(2)
    last_tile = pl.num_programs(2) - 1
    first_head = lax.broadcasted_iota(jnp.int32, (1, LANES), 1) < HEAD_DIM
    nt_dims = (((1,), (1,)), ((), ()))

    def window(p_ref, c_ref, n_ref, u, cols):
        lo, hi = u * A_SUB - A_RADIUS, (u + 1) * A_SUB + A_RADIUS
        parts = [p_ref[:, cols]] if lo < 0 else []
        parts.append(c_ref[max(lo, 0):min(hi, tq), cols])
        if hi > tq:
            parts.append(n_ref[:, cols])
        return parts[0] if len(parts) == 1 else jnp.concatenate(parts, axis=0)

    for u in range(nsub):
        if u == 0:
            bias = bias_ref[jnp.where(i == 0, 1, 0)]
        elif u == nsub - 1:
            bias = bias_ref[jnp.where(i == last_tile, 2, 0)]
        else:
            bias = bias_ref[0]
        rows = slice(u * A_SUB, (u + 1) * A_SUB)
        for pair in range(A_GROUP_WIDTH // LANES):
            cols = slice(pair * LANES, (pair + 1) * LANES)
            k = window(kp_ref, kc_ref, kn_ref, u, cols)
            v = window(vp_ref, vc_ref, vn_ref, u, cols)
            q = q_ref[rows, cols]
            outs, lses = [], []
            for hh in range(LANES // HEAD_DIM):
                mine = first_head if hh == 0 else jnp.logical_not(first_head)
                qm = jnp.where(mine, q, jnp.zeros_like(q))
                sc = lax.dot_general(qm, k, nt_dims, preferred_element_type=F32) + bias
                m = jnp.max(sc, axis=1, keepdims=True)
                p = jnp.exp2(sc - m)
                den = jnp.sum(p, axis=1, keepdims=True)
                outs.append(jnp.dot(p.astype(BF16), v, preferred_element_type=F32) / den)
                lses.append(m * LN2 + jnp.log(den))
            o_ref[rows, cols] = jnp.where(first_head, outs[0], outs[1])
            lse_ref[rows, cols] = jnp.where(first_head, lses[0], lses[1])


def _attn_a(q, k, v, gi):
    b, dil, cl, _ = q.shape
    tq = min(A_TQ, cl)
    assert tq >= 2 * A_SUB and A_SUB == 2 * A_RADIUS
    hb = tq // A_RADIUS
    n_halo = cl // A_RADIUS
    bias = _band_bias()
    main = pl.BlockSpec((None, None, tq, A_GROUP_WIDTH), lambda bi, r, i: (bi, r, i, 0))
    prev = pl.BlockSpec((None, None, A_RADIUS, A_GROUP_WIDTH),
                        lambda bi, r, i: (bi, r, jnp.maximum(i * hb - 1, 0), 0))
    nxt = pl.BlockSpec((None, None, A_RADIUS, A_GROUP_WIDTH),
                       lambda bi, r, i: (bi, r, jnp.minimum((i + 1) * hb, n_halo - 1), 0))
    return pl.pallas_call(
        _attn_a_kernel,
        out_shape=[jax.ShapeDtypeStruct((b, dil, cl, A_GROUP_WIDTH), F32)] * 2,
        grid=(b, dil, cl // tq),
        in_specs=[_full_spec(bias), main, prev, main, nxt, prev, main, nxt],
        out_specs=[main, main],
        compiler_params=_cparams(("parallel", "parallel", "parallel")),
        name=f"attn_a{gi}",
    )(bias, q, k, k, k, v, v, v)


def _flash_b_kernel(qt_ref, k_ref, vt_ref, o_ref, acc_sc, l_sc, m_sc, p0_sc, p1_sc, *, online):
    nk, tk, _ = k_ref.shape
    width = qt_ref.shape[1]
    tq = width // B_GROUP_HEADS
    acc_sc[...] = jnp.zeros(acc_sc.shape, F32)
    l_sc[...] = jnp.zeros(l_sc.shape, F32)

    def scores(j):
        return jnp.dot(k_ref[j], qt_ref[...], preferred_element_type=F32)

    def column_sums(p):
        return p.reshape(tk // SUBLANES, SUBLANES, width).sum(axis=0)

    def accumulate(j, pb):
        vt = vt_ref[j]
        for h in range(B_GROUP_HEADS):
            cols = slice(h * tq, (h + 1) * tq)
            acc_sc[:, cols] += jnp.dot(vt, pb[:, cols], preferred_element_type=F32)

    if online:
        m_sc[...] = jnp.full(m_sc.shape, NEG_INF, F32)

        def body(j, carry):
            sc = scores(j)
            m_old = m_sc[...]
            m_new = jnp.maximum(m_old, jnp.max(sc, axis=0, keepdims=True))
            alpha = jnp.exp2(m_old - m_new)
            m_sc[...] = m_new
            p = jnp.exp2(sc - m_new)
            l_sc[...] = alpha * l_sc[...] + column_sums(p)
            acc_sc[...] = alpha * acc_sc[...]
            accumulate(j, p.astype(BF16))
            return carry

        lax.fori_loop(0, nk, body, 0)
    else:
        def exponentials(sc, p_out):
            p = jnp.exp2(sc)
            l_sc[...] += column_sums(p)
            p_out[...] = p.astype(BF16)

        def stage(j, p_in, p_out):
            sc = scores(j)
            accumulate(j - 1, p_in[...])
            exponentials(sc, p_out)

        exponentials(scores(0), p0_sc)

        def body(jj, carry):
            stage(2 * jj + 1, p0_sc, p1_sc)
            stage(2 * jj + 2, p1_sc, p0_sc)
            return carry

        lax.fori_loop(0, nk // 2 - 1, body, 0)
        stage(nk - 1, p0_sc, p1_sc)
        accumulate(nk - 1, p1_sc[...])
    o = acc_sc[...] / jnp.sum(l_sc[...], axis=0, keepdims=True)
    o = jnp.concatenate([o[:, h * tq:(h + 1) * tq] for h in range(B_GROUP_HEADS)], axis=0)
    o_ref[...] = o.T.astype(o_ref.dtype)


def _flash_b(qt, kb, vt, online):
    b, _, nq, _, width = qt.shape
    tq = width // B_GROUP_HEADS
    _, nk, tk, _ = kb.shape
    return pl.pallas_call(
        functools.partial(_flash_b_kernel, online=online),
        out_shape=jax.ShapeDtypeStruct((b, B_KV_HEADS, nq * tq, B_GROUP_WIDTH), BF16),
        grid=(b, B_KV_HEADS, nq),
        in_specs=[pl.BlockSpec((None, None, None, B_KV_WIDTH, width), lambda bi, g, i: (bi, g, i, 0, 0)),
                  pl.BlockSpec((None, nk, tk, B_KV_WIDTH), lambda bi, g, i: (bi, 0, 0, 0)),
                  pl.BlockSpec((None, None, nk, HEAD_DIM, tk), lambda bi, g, i: (bi, g, 0, 0, 0))],
        out_specs=pl.BlockSpec((None, None, tq, B_GROUP_WIDTH), lambda bi, g, i: (bi, g, i, 0)),
        scratch_shapes=[pltpu.VMEM((HEAD_DIM, width), F32), pltpu.VMEM((SUBLANES, width), F32),
                        pltpu.VMEM((1, width), F32),
                        pltpu.VMEM((tk, width), BF16), pltpu.VMEM((tk, width), BF16)],
        compiler_params=_cparams(("parallel", "parallel", "parallel")),
        name="flash_b_online" if online else "flash_b",
    )(qt, kb, vt)


def _stage_classes(blk_ref, stage_ref):
    dil, rows, width = blk_ref.shape
    if dil == 1:
        return
    for c in range(width // LANES):
        for r in range(dil):
            stage_ref.at[c][pl.ds(r, rows, stride=dil), :] = blk_ref[r, :, c * LANES:(c + 1) * LANES]


def _token_rows(blk_ref, stage_ref, rows):
    if blk_ref.shape[0] == 1:
        return blk_ref[0, rows, :]
    return jnp.concatenate([stage_ref[c, rows, :] for c in range(stage_ref.shape[0])], axis=1)


def _post_attn_kernel(x_ref, o0_ref, o1_ref, o2_ref, l0_ref, l1_ref, l2_ref, ob0_ref, ob1_ref, gates_ref,
                      wa_ref, wb0_ref, wb1_ref, wout_ref, gffn_ref, wrh_ref, wrl_ref,
                      x1_ref, h2_ref, aff_ref, so1, so2, sl1, sl2):
    tm = x_ref.shape[0]
    for blk, stage in ((l1_ref, sl1), (l2_ref, sl2), (o1_ref, so1), (o2_ref, so2)):
        _stage_classes(blk, stage)
    nt = (((1,), (1,)), ((), ()))
    hr = tm // POST_SPLIT
    for half in range(POST_SPLIT):
        rows = slice(half * hr, (half + 1) * hr)
        l0 = _token_rows(l0_ref, None, rows)
        l1 = _token_rows(l1_ref, sl1, rows)
        l2 = _token_rows(l2_ref, sl2, rows)
        m = jnp.maximum(jnp.maximum(l0, l1), l2)
        e0, e1, e2 = jnp.exp(l0 - m), jnp.exp(l1 - m), jnp.exp(l2 - m)
        oa = (e0 * _token_rows(o0_ref, None, rows) + e1 * _token_rows(o1_ref, so1, rows)
              + e2 * _token_rows(o2_ref, so2, rows)) / (e0 + e1 + e2)
        ya = jnp.dot(oa.astype(BF16), wa_ref[...], preferred_element_type=F32)
        yb = (jnp.dot(ob0_ref[rows, :], wb0_ref[...], preferred_element_type=F32)
              + jnp.dot(ob1_ref[rows, :], wb1_ref[...], preferred_element_type=F32))
        ga = gates_ref[rows, :D_MODEL].astype(F32)
        gb = gates_ref[rows, D_MODEL:].astype(F32)
        z = (ga * ya + gb * yb).astype(BF16)
        x1 = x_ref[rows, :] + jnp.dot(z, wout_ref[...], preferred_element_type=F32)
        x1_ref[rows, :] = x1
        ms = jnp.mean(x1 * x1, axis=-1, keepdims=True)
        h2 = x1 * lax.rsqrt(ms + EPS) * gffn_ref[...]
        hi = h2.astype(BF16)
        lo = (h2 - hi.astype(F32)).astype(BF16)
        h2_ref[rows, :] = hi
        both = lax.dot_general(wrl_ref[...], hi, nt, preferred_element_type=F32)
        logits = (both[:N_EXPERTS] + both[N_EXPERTS:]
                  + lax.dot_general(wrh_ref[...], lo, nt, preferred_element_type=F32))
        logits = logits - jnp.max(logits, axis=0, keepdims=True)
        e = jnp.exp(logits)
        aff = e / jnp.sum(e, axis=0, keepdims=True)
        for j in range(hr // LANES):
            aff_ref[half * (hr // LANES) + j] = aff[:, j * LANES:(j + 1) * LANES]


def _post_attn(xf, oas, lses, ob, gates, b, s, w_a, w_b, w_out, g_ffn, w_router):
    n = xf.shape[0]
    tm = ROW_TILE
    nts = s // tm
    wa = w_a.astype(BF16)
    wb = w_b.astype(BF16)
    wb0, wb1 = wb[:B_GROUP_WIDTH], wb[B_GROUP_WIDTH:]
    wout = w_out.astype(BF16)
    wrt = w_router.T
    wrh = wrt.astype(BF16)
    wrl = jnp.concatenate([wrh, (wrt - wrh.astype(F32)).astype(BF16)], axis=0)
    consts = [wa, wb0, wb1, wout, g_ffn.reshape(1, D_MODEL), wrh, wrl]

    def rows(w):
        return pl.BlockSpec((tm, w), lambda i: (i, 0))

    def class_spec(dil):
        return pl.BlockSpec((None, dil, tm // dil, A_GROUP_WIDTH), lambda i: (i // nts, 0, i % nts, 0))

    def ob_spec(g):
        return pl.BlockSpec((None, None, tm, B_GROUP_WIDTH), lambda i: (i // nts, g, i % nts, 0))

    a_specs = [class_spec(d) for d in A_DILATIONS]
    return pl.pallas_call(
        _post_attn_kernel,
        out_shape=[jax.ShapeDtypeStruct((n, D_MODEL), F32),
                   jax.ShapeDtypeStruct((n, D_MODEL), BF16),
                   jax.ShapeDtypeStruct((n // LANES, N_EXPERTS, LANES), F32)],
        grid=(n // tm,),
        in_specs=[rows(D_MODEL)] + a_specs + a_specs + [ob_spec(0), ob_spec(1), rows(GATE_WIDTH)]
                 + [_full_spec(a) for a in consts],
        out_specs=[rows(D_MODEL), rows(D_MODEL),
                   pl.BlockSpec((tm // LANES, N_EXPERTS, LANES), lambda i: (i, 0, 0))],
        scratch_shapes=[pltpu.VMEM((A_GROUP_WIDTH // LANES, tm, LANES), F32)] * 4,
        compiler_params=_cparams(("parallel",)),
        name="post_attn",
    )(xf, *oas, *lses, ob, ob, gates, *consts)


def _route_kernel(aff_ref, tri_ref, posl_ref, starts_ref, sel_sc, *, cap, idx_bits, tiles_per_block):
    nlt = aff_ref.shape[0]
    capf = float(cap)

    def count(mask):
        part = jnp.sum(jnp.where(mask, 1.0, 0.0), axis=0)
        return jnp.sum(part, axis=1, keepdims=True)[None]

    def value_body(i, tau):
        bits = pltpu.bitcast(aff_ref[...], jnp.int32)
        cand = tau | lax.shift_left(jnp.int32(1), 30 - i)
        return jnp.where(count(bits >= cand) >= capf, cand, tau)

    tau = lax.fori_loop(0, 31, value_body, jnp.zeros((1, N_EXPERTS, 1), jnp.int32))
    bits = pltpu.bitcast(aff_ref[...], jnp.int32)
    need = capf - count(bits > tau)
    tok = (lax.broadcasted_iota(jnp.int32, bits.shape, 0) * LANES
           + lax.broadcasted_iota(jnp.int32, bits.shape, 2))

    def index_body(i, last):
        b = pltpu.bitcast(aff_ref[...], jnp.int32)
        cand = last | lax.shift_left(jnp.int32(1), idx_bits - 1 - i)
        return jnp.where(count((b == tau) & (tok < cand)) < need, cand, last)

    last = lax.fori_loop(0, idx_bits, index_body, jnp.zeros((1, N_EXPERTS, 1), jnp.int32))
    sel_sc[...] = jnp.where((bits > tau) | ((bits == tau) & (tok <= last)), 1.0, 0.0)

    sel = sel_sc[...]
    sb = sel.reshape(nlt * N_EXPERTS, LANES).astype(BF16)
    tri = tri_ref[...]
    within = jnp.dot(sb, tri, preferred_element_type=F32).reshape(nlt, N_EXPERTS, LANES)
    total = jnp.dot(sb, jnp.ones((LANES, LANES), BF16),
                    preferred_element_type=F32).reshape(nlt, N_EXPERTS, LANES)
    upto = total
    step = 1
    while step < nlt:
        upto = upto + jnp.concatenate([jnp.zeros((step, N_EXPERTS, LANES), F32), upto[:nlt - step]], axis=0)
        step *= 2
    before = upto - total
    posl_ref[...] = jnp.where(sel > 0, before + within - sel, -1.0).astype(jnp.int32)
    starts_ref[...] = before.reshape(nlt // tiles_per_block, tiles_per_block, N_EXPERTS, LANES)[:, 0].astype(jnp.int32)


def _route(aff3, cap):
    nlt = aff3.shape[0]
    n = nlt * LANES
    tpb = MOE_T // LANES
    tri = (jnp.arange(LANES)[:, None] <= jnp.arange(LANES)[None, :]).astype(BF16)
    vm = pl.BlockSpec(memory_space=pltpu.VMEM)
    return pl.pallas_call(
        functools.partial(_route_kernel, cap=cap, idx_bits=(n - 1).bit_length(), tiles_per_block=tpb),
        out_shape=[jax.ShapeDtypeStruct((nlt, N_EXPERTS, LANES), jnp.int32),
                   jax.ShapeDtypeStruct((n // MOE_T, N_EXPERTS, LANES), jnp.int32)],
        in_specs=[vm, vm],
        out_specs=[vm, vm],
        scratch_shapes=[pltpu.VMEM((nlt, N_EXPERTS, LANES), F32)],
        compiler_params=pltpu.CompilerParams(vmem_limit_bytes=VMEM_LIMIT),
        name="route",
    )(aff3, tri)


def _block_slots(s_ref, i, nt):
    return [(s_ref[e * (nt + 1) + i], s_ref[e * (nt + 1) + i + 1]) for e in range(N_EXPERTS)]


def _align_down(x):
    return (x >> MOE_ALIGN_SHIFT) << MOE_ALIGN_SHIFT


def _moe_gather_kernel(s_ref, aff_ref, posl_ref, h2_ref, xs_hbm, tm_ref,
                       haug_sc, pall_sc, stage_sc, carry_sc, pending_sc, sem, *, nt, cap):
    i = pl.program_id(0)
    nj = posl_ref.shape[0]
    ws = MOE_WS

    @pl.when(i == 0)
    def _():
        carry_sc[...] = jnp.zeros(carry_sc.shape, BF16)
        pending_sc[0] = 0
        pending_sc[1] = 0
        pad = xs_hbm.shape[1] - cap
        stage_sc[0] = jnp.zeros(stage_sc.shape[1:], BF16)
        fills = [pltpu.make_async_copy(stage_sc.at[0, pl.ds(0, pad)], xs_hbm.at[e, pl.ds(cap, pad)], sem)
                 for e in range(N_EXPERTS)]
        for cp in fills:
            cp.start()
        for cp in fills:
            cp.wait()

    for j in range(nj):
        pj = posl_ref[j]
        gj = jnp.where(pj >= 0, aff_ref[j], 0.0)
        blk = jnp.concatenate([gj, pj.astype(F32), gj, jnp.zeros((LANES - 3 * N_EXPERTS, LANES), F32)], axis=0)
        tm_ref[j * LANES:(j + 1) * LANES, :] = blk.T
    tmv = tm_ref[...]
    lane = lax.broadcasted_iota(jnp.int32, tmv.shape, 1)
    ghi = tmv.astype(BF16)
    glo = (tmv - ghi.astype(F32)).astype(BF16)
    zero = jnp.zeros_like(ghi)
    haug_sc[:, :D_MODEL] = h2_ref[...]
    haug_sc[:, D_MODEL:] = jnp.where(lane < N_EXPERTS, ghi,
                                     jnp.where((lane >= TM_GATE_LO) & (lane < TM_GATE_LO + N_EXPERTS), glo, zero))

    astarts, kks, offcs = [], [], []
    nw = jnp.int32(0)
    for st, en in _block_slots(s_ref, i, nt):
        ast = _align_down(st)
        span = _align_down(en) - ast
        kk = span >> MOE_WS_SHIFT
        astarts.append(ast)
        kks.append(kk)
        offcs.append(span - kk * ws)
        nw = jnp.maximum(nw, kk + 1)

    def wait_window():
        for _ in range(N_EXPERTS):
            pltpu.make_async_copy(stage_sc.at[0, pl.ds(0, ws)], xs_hbm.at[0, pl.ds(0, ws)], sem).wait()

    row = lax.broadcasted_iota(jnp.int32, (ws, LANES), 0)

    def window(k, carry):
        for e in range(N_EXPERTS):
            base = astarts[e] + k * ws
            for j in range(nj):
                hit = (posl_ref[j, e:e + 1, :] - base) == row
                pall_sc[e * ws:(e + 1) * ws, j * LANES:(j + 1) * LANES] = jnp.where(hit, 1.0, 0.0).astype(BF16)
        stage = stage_sc.at[pending_sc[1]]
        stage[...] = jnp.dot(pall_sc[...], haug_sc[...], preferred_element_type=F32).astype(BF16)

        @pl.when(k == 0)
        def _():
            for e in range(N_EXPERTS):
                stage[e * ws:e * ws + MOE_ALIGN, :] += carry_sc[e]

        for e in range(N_EXPERTS):
            @pl.when(k == kks[e])
            def _():
                carry_sc[e] = stage[pl.ds(pl.multiple_of(e * ws + offcs[e], MOE_ALIGN), MOE_ALIGN), :]

        @pl.when(pending_sc[0] == 1)
        def _():
            wait_window()

        for e in range(N_EXPERTS):
            base = pl.multiple_of(astarts[e] + k * ws, MOE_ALIGN)
            pltpu.make_async_copy(stage.at[pl.ds(e * ws, ws)], xs_hbm.at[e, pl.ds(base, ws)], sem).start()
        pending_sc[0] = 1
        pending_sc[1] = 1 - pending_sc[1]
        return carry

    lax.fori_loop(0, nw, window, 0)

    @pl.when((i == nt - 1) & (pending_sc[0] == 1))
    def _():
        wait_window()
        pending_sc[0] = 0


def _moe_gather(s_flat, aff3, posl, h2, cap):
    n = h2.shape[0]
    t = MOE_T
    nt = n // t
    cap_pad = cap + (t // MOE_WS + 1) * MOE_WS
    blk3 = pl.BlockSpec((t // LANES, N_EXPERTS, LANES), lambda i, s: (i, 0, 0))
    grid_spec = pltpu.PrefetchScalarGridSpec(
        num_scalar_prefetch=1,
        grid=(nt,),
        in_specs=[blk3, blk3, pl.BlockSpec((t, D_MODEL), lambda i, s: (i, 0))],
        out_specs=[pl.BlockSpec(memory_space=pl.ANY), pl.BlockSpec((t, LANES), lambda i, s: (i, 0))],
        scratch_shapes=[pltpu.VMEM((t, XS_WIDTH), BF16),
                        pltpu.VMEM((N_EXPERTS * MOE_WS, t), BF16),
                        pltpu.VMEM((2, N_EXPERTS * MOE_WS, XS_WIDTH), BF16),
                        pltpu.VMEM((N_EXPERTS, MOE_ALIGN, XS_WIDTH), BF16),
                        pltpu.SMEM((2,), jnp.int32),
                        pltpu.SemaphoreType.DMA(())])
    return pl.pallas_call(
        functools.partial(_moe_gather_kernel, nt=nt, cap=cap),
        out_shape=[jax.ShapeDtypeStruct((N_EXPERTS, cap_pad, XS_WIDTH), BF16),
                   jax.ShapeDtypeStruct((n, LANES), F32)],
        grid_spec=grid_spec,
        compiler_params=_cparams(("arbitrary",)),
        name="moe_gather",
    )(s_flat, aff3, posl, h2)


def _moe_ffn_kernel(xs_ref, w1_ref, w3_ref, w2_ref, ye_ref):
    xe = xs_ref[:, :D_MODEL]
    route = xs_ref[:, D_MODEL:].astype(F32)
    lane = lax.broadcasted_iota(jnp.int32, route.shape, 1)
    e = pl.program_id(0)
    gate = jnp.sum(jnp.where((lane == e) | (lane == e + TM_GATE_LO), route, 0.0), axis=1, keepdims=True)
    a = jnp.dot(xe, w1_ref[...], preferred_element_type=F32)
    u = jnp.dot(xe, w3_ref[...], preferred_element_type=F32)
    hid = (a * jax.nn.sigmoid(a) * u).astype(BF16)
    ye = jnp.dot(hid, w2_ref[...], preferred_element_type=F32) * gate
    ye_ref[...] = ye.astype(ye_ref.dtype)


def _moe_ffn(xs, cap, w1, w3, w2):
    e = xs.shape[0]
    tc = min(FFN_TC, cap)
    wspec = pl.BlockSpec((None, D_MODEL, D_MODEL), lambda ei, ci: (ei, 0, 0))
    return pl.pallas_call(
        _moe_ffn_kernel,
        out_shape=jax.ShapeDtypeStruct((e, cap, D_MODEL), BF16),
        grid=(e, cap // tc),
        in_specs=[pl.BlockSpec((None, tc, XS_WIDTH), lambda ei, ci: (ei, ci, 0)), wspec, wspec, wspec],
        out_specs=pl.BlockSpec((None, tc, D_MODEL), lambda ei, ci: (ei, ci, 0)),
        compiler_params=_cparams(("parallel", "arbitrary")),
        name="moe_ffn",
    )(xs, w1, w3, w2)


def _final_kernel(s_ref, x1_ref, tm_ref, p_ref, ye_hbm, wple_ref, wpg_ref, gple_ref, gfin_ref, expand_ref, out_ref,
                  buf_sc, pall_sc, y_sc, sem, *, nt, cap):
    i = pl.program_id(0)
    ws = MOE_WS
    t = x1_ref.shape[0]
    slot = i % 2

    def window_rows(blk, k):
        out = []
        for st, _ in _block_slots(s_ref, blk, nt):
            lo = _align_down(st) + k * ws
            out.append((lo, pl.multiple_of(jnp.minimum(lo, cap - ws), MOE_ALIGN)))
        return out

    def copies(rows, buf, which):
        return [pltpu.make_async_copy(ye_hbm.at[e, pl.ds(base, ws)], buf_sc.at[buf, pl.ds(e * ws, ws)], sem.at[which])
                for e, (_, base) in enumerate(rows)]

    @pl.when(i == 0)
    def _():
        for cp in copies(window_rows(0, 0), 0, 0):
            cp.start()

    @pl.when(i + 1 < nt)
    def _():
        for cp in copies(window_rows(i + 1, 0), 1 - slot, 1 - slot):
            cp.start()

    nw = jnp.int32(0)
    for st, en in _block_slots(s_ref, i, nt):
        nw = jnp.maximum(nw, (en - _align_down(st) + ws - 1) >> MOE_WS_SHIFT)
    lane = lax.broadcasted_iota(jnp.int32, (1, LANES), 1)
    col = (lax.broadcasted_iota(jnp.int32, (1, N_EXPERTS * ws), 1) & (ws - 1)).astype(F32)

    def onehot(rows):
        basev = jnp.zeros((1, LANES), F32)
        lov = jnp.full((1, LANES), float(2 ** 30), F32)
        for e, (lo, base) in enumerate(rows):
            basev = jnp.where(lane == TM_POS + e, base.astype(F32), basev)
            lov = jnp.where(lane == TM_POS + e, lo.astype(F32), lov)
        tmv = tm_ref[...]
        off = tmv - basev
        off = jnp.where((tmv >= lov) & (off >= 0.0) & (off < float(ws)), off, -1.0)
        spread = jnp.dot(off.astype(BF16), expand_ref[...], preferred_element_type=F32)
        pall_sc[...] = jnp.where(spread == col, 1.0, 0.0).astype(BF16)

    rows0 = window_rows(i, 0)
    onehot(rows0)
    for cp in copies(rows0, slot, slot):
        cp.wait()
    y_sc[...] = jnp.dot(pall_sc[...], buf_sc[slot], preferred_element_type=F32)

    def window(k, carry):
        rows = window_rows(i, k)
        cps = copies(rows, slot, slot)
        for cp in cps:
            cp.start()
        onehot(rows)
        for cp in cps:
            cp.wait()
        y_sc[...] += jnp.dot(pall_sc[...], buf_sc[slot], preferred_element_type=F32)
        return carry

    lax.fori_loop(1, nw, window, 0)

    hr = t // POST_SPLIT
    for half in range(POST_SPLIT):
        rows = slice(half * hr, (half + 1) * hr)
        x2 = x1_ref[rows, :] + y_sc[rows, :]
        ms = jnp.mean(x2 * x2, axis=-1, keepdims=True)
        hp = (x2 * lax.rsqrt(ms + EPS) * gple_ref[...]).astype(BF16)
        gt = jax.nn.sigmoid(jnp.dot(hp, wpg_ref[...], preferred_element_type=F32))
        emb = jnp.dot(p_ref[rows, :].astype(BF16), wple_ref[...], preferred_element_type=F32)
        x3 = x2 + emb * gt
        ms3 = jnp.mean(x3 * x3, axis=-1, keepdims=True)
        out_ref[rows, :] = x3 * lax.rsqrt(ms3 + EPS) * gfin_ref[...]


def _final(s_flat, x1, tm, pf, ye, w_ple, w_ple_gate, g_ple, g_final):
    n = x1.shape[0]
    t = MOE_T
    nt = n // t
    cap = ye.shape[1]
    src = jnp.arange(LANES)[:, None] - TM_POS
    dst = jnp.arange(N_EXPERTS * MOE_WS)[None, :] >> MOE_WS_SHIFT
    expand = (src == dst).astype(BF16)
    consts = [w_ple.astype(BF16), w_ple_gate.astype(BF16), g_ple.reshape(1, D_MODEL), g_final.reshape(1, D_MODEL),
              expand]

    def rows(w):
        return pl.BlockSpec((t, w), lambda i, s: (i, 0))

    grid_spec = pltpu.PrefetchScalarGridSpec(
        num_scalar_prefetch=1,
        grid=(nt,),
        in_specs=[rows(D_MODEL), rows(LANES), rows(PLE_DIM), pl.BlockSpec(memory_space=pl.ANY)]
                 + [_full_spec(a) for a in consts],
        out_specs=rows(D_MODEL),
        scratch_shapes=[pltpu.VMEM((2, N_EXPERTS * MOE_WS, D_MODEL), BF16),
                        pltpu.VMEM((t, N_EXPERTS * MOE_WS), BF16),
                        pltpu.VMEM((t, D_MODEL), F32),
                        pltpu.SemaphoreType.DMA((2,))])
    return pl.pallas_call(
        functools.partial(_final_kernel, nt=nt, cap=cap),
        out_shape=jax.ShapeDtypeStruct((n, D_MODEL), F32),
        grid_spec=grid_spec,
        compiler_params=_cparams(("arbitrary",)),
        name="final",
    )(s_flat, x1, tm, pf, ye, *consts)


def _run_group(x, p, g_mix, w_in, g_qn, g_kn, w_a_proj, w_b_proj, w_out, g_ffn, w_router,
               w1, w3, w2, g_ple, w_ple, w_ple_gate, g_final):
    b, s, _ = x.shape
    n = b * s
    assert s % B_TK == 0 and s % (A_DILATIONS[-1] * A_RADIUS) == 0
    xf = x.reshape(n, D_MODEL)
    tab = _rope_tables(s)
    outs = _in_proj(xf, tab, b, s, g_mix, w_in, g_qn, g_kn)
    qas, kas, vas = outs[0:3], outs[3:6], outs[6:9]
    qt, kb, vt, gates = outs[9:]

    oas, lses = [], []
    for gi, (window, dil) in enumerate(A_PATTERNS):
        assert window // (2 * dil) == A_RADIUS
        o, lse = _attn_a(qas[gi], kas[gi], vas[gi], gi)
        oas.append(o)
        lses.append(lse)

    score_bound = (HEAD_DIM * Q_SCALE * 1.02) * jnp.max(jnp.abs(g_qn)) * jnp.max(jnp.abs(g_kn))
    kb4 = kb.reshape(b, s // B_TK, B_TK, B_KV_WIDTH)
    ob = lax.cond(score_bound <= B_UNSHIFTED_SCORE_MAX,
                  lambda: _flash_b(qt, kb4, vt, online=False),
                  lambda: _flash_b(qt, kb4, vt, online=True))

    x1, h2, aff3 = _post_attn(xf, oas, lses, ob, gates, b, s, w_a_proj, w_b_proj, w_out, g_ffn, w_router)

    cap = CAPACITY_FACTOR * n // N_EXPERTS
    assert cap % MOE_WS == 0 and cap >= MOE_WS
    posl, starts = _route(aff3, cap)
    s_flat = jnp.concatenate([starts[:, :, 0].T, jnp.full((N_EXPERTS, 1), cap, jnp.int32)], axis=1).reshape(-1)
    xs, tm = _moe_gather(s_flat, aff3, posl, h2, cap)
    ye = _moe_ffn(xs, cap, w1, w3, w2)
    out = _final(s_flat, x1, tm, p.reshape(n, PLE_DIM), ye, w_ple, w_ple_gate, g_ple, g_final)
    return out.reshape(b, s, D_MODEL)


def kernel(x_prompt, x_sample, p_prompt, p_sample, g_mix, w_in, g_qn, g_kn, w_a_proj, w_b_proj, w_out, g_ffn,
           w_router, w_exp_gate, w_exp_up, w_exp_down, g_ple, w_ple, w_ple_gate, g_final):
    assert g_mix.shape[0] == 1, "single layer"
    w1 = w_exp_gate[0].astype(BF16)
    w3 = w_exp_up[0].astype(BF16)
    w2 = w_exp_down[0].astype(BF16)
    args = (g_mix[0], w_in[0], g_qn[0], g_kn[0], w_a_proj[0], w_b_proj[0], w_out[0], g_ffn[0], w_router[0],
            w1, w3, w2, g_ple[0], w_ple[0], w_ple_gate[0], g_final)
    y_prompt = _run_group(x_prompt, p_prompt[0], *args)
    y_sample = _run_group(x_sample, p_sample[0], *args)
    return (y_prompt, y_sample)
```

```python
import functools
import math

import jax
import jax.numpy as jnp
from jax import lax
from jax.experimental import pallas as pl
from jax.experimental.pallas import tpu as pltpu

F32 = jnp.float32
BF16 = jnp.bfloat16

D_MODEL = 1024
HEAD_DIM = 64
A_PATTERNS = ((128, 1), (512, 4), (2048, 16))
A_DILATIONS = tuple(d for _, d in A_PATTERNS)
A_GROUPS = len(A_PATTERNS)
A_HEADS_PER_GROUP = 4
A_WIDTH = 768
A_GROUP_WIDTH = A_HEADS_PER_GROUP * HEAD_DIM
A_ROT_DIM = 16
A_RADIUS = 64
ROPE_THETA = 500000.0
B_Q_WIDTH = 512
B_KV_HEADS = 2
B_KV_WIDTH = 128
B_GROUP_HEADS = 4
B_GROUP_WIDTH = B_GROUP_HEADS * HEAD_DIM
AXIAL_THETA = 10000.0
GRID_W = 64
GATE_WIDTH = 2048
N_EXPERTS = 16
CAPACITY_FACTOR = 2
PLE_DIM = 256
EPS = 1e-6
NEG_INF = -1e30
LN2 = math.log(2.0)
Q_SCALE = (HEAD_DIM ** -0.5) / LN2
B_UNSHIFTED_SCORE_MAX = 50.0

LANES = 128
SUBLANES = 8
VMEM_LIMIT = 48 * 1024 * 1024

ROW_TILE = 512
A_TQ = 1024
A_SUB = 128
B_TQ = ROW_TILE
B_TK = 4 * ROW_TILE
POST_SPLIT = 1
FFN_TC = 1024
MOE_T = ROW_TILE
MOE_WS_SHIFT = 7
MOE_WS = 1 << MOE_WS_SHIFT
MOE_ALIGN_SHIFT = 4
MOE_ALIGN = 1 << MOE_ALIGN_SHIFT
XS_WIDTH = D_MODEL + LANES
TM_POS = N_EXPERTS
TM_GATE_LO = 2 * N_EXPERTS


def _cparams(sem):
    return pltpu.CompilerParams(dimension_semantics=sem, vmem_limit_bytes=VMEM_LIMIT)


def _full_spec(a):
    return pl.BlockSpec(a.shape, lambda *_: (0,) * a.ndim)


def _rope_tables(s):
    pos = jnp.arange(s)
    posf = pos.astype(F32)
    inv_a = jnp.power(ROPE_THETA, -jnp.arange(0, A_ROT_DIM, 2, dtype=F32) / A_ROT_DIM)
    ang = posf[:, None] * inv_a[None, :]
    ca, sa = jnp.cos(ang), jnp.sin(ang)
    z8 = jnp.zeros_like(sa)
    rest = HEAD_DIM - A_ROT_DIM
    cos_a = jnp.concatenate([ca, ca, jnp.ones((s, rest), F32)], axis=1)
    sp_a = jnp.concatenate([z8, sa, jnp.zeros((s, rest), F32)], axis=1)
    sm_a = jnp.concatenate([-sa, z8, jnp.zeros((s, rest), F32)], axis=1)

    hb = HEAD_DIM // 2
    inv_b = jnp.power(AXIAL_THETA, -jnp.arange(0, hb, 2, dtype=F32) / hb)
    row = (pos // GRID_W).astype(F32)
    col = (pos % GRID_W).astype(F32)
    ar = row[:, None] * inv_b[None, :]
    ac = col[:, None] * inv_b[None, :]
    cr, sr, cc, sc = jnp.cos(ar), jnp.sin(ar), jnp.cos(ac), jnp.sin(ac)
    z16 = jnp.zeros_like(sr)
    cos_b = jnp.concatenate([cr, cr, cc, cc], axis=1)
    sp_b = jnp.concatenate([z16, sr, z16, sc], axis=1)
    sm_b = jnp.concatenate([-sr, z16, -sc, z16], axis=1)
    heads_per_tile = LANES // HEAD_DIM
    parts = [jnp.tile(t, (1, heads_per_tile)) for t in (cos_a, sp_a, sm_a, cos_b, sp_b, sm_b)]
    return jnp.concatenate(parts, axis=1)


def _rope_chunk(x, tab_ref, base, half):
    cos = tab_ref[:, base:base + LANES]
    sp = tab_ref[:, base + LANES:base + 2 * LANES]
    sm = tab_ref[:, base + 2 * LANES:base + 3 * LANES]
    return x * cos + pltpu.roll(x, half, 1) * sp + pltpu.roll(x, LANES - half, 1) * sm


def _head_mean_sq(acc, blk_ref):
    sq = acc * acc
    hi = sq.astype(BF16)
    lo = (sq - hi.astype(F32)).astype(BF16)
    blk = blk_ref[...]
    return (jnp.dot(hi, blk, preferred_element_type=F32)
            + jnp.dot(lo, blk, preferred_element_type=F32))


def _store_by_class(out_refs, chunk_idx, chunk, stage_ref):
    g, half = divmod(chunk_idx, A_GROUP_WIDTH // LANES)
    dil = A_DILATIONS[g]
    lanes = slice(half * LANES, (half + 1) * LANES)
    if dil == 1:
        out_refs[g][0, :, lanes] = chunk.astype(BF16)
        return
    tm = chunk.shape[0]
    stage_ref[...] = chunk
    for r in range(dil):
        out_refs[g][r, :, lanes] = stage_ref[pl.ds(r, tm // dil, stride=dil), :].astype(BF16)


def _in_proj_kernel(x_ref, tab_ref, gmix_ref, wqa_ref, wka_ref, wva_ref, wqb_ref, wkb_ref, wvb_ref,
                    wg_ref, gq_ref, gk_ref, blkq_ref, blkk_ref,
                    qa0_ref, qa1_ref, qa2_ref, ka0_ref, ka1_ref, ka2_ref, va0_ref, va1_ref, va2_ref,
                    qt_ref, kb_ref, vt_ref, gates_ref, stage_ref):
    x = x_ref[...]
    tm = x.shape[0]
    ms = jnp.mean(x * x, axis=-1, keepdims=True)
    h = (x * lax.rsqrt(ms + EPS) * gmix_ref[...]).astype(BF16)

    acc = jnp.dot(h, wqa_ref[...], preferred_element_type=F32)
    for c in range(A_WIDTH // LANES):
        roped = _rope_chunk(acc[:, c * LANES:(c + 1) * LANES], tab_ref, 0, A_ROT_DIM // 2) * Q_SCALE
        _store_by_class((qa0_ref, qa1_ref, qa2_ref), c, roped, stage_ref)

    acc = jnp.dot(h, wka_ref[...], preferred_element_type=F32)
    for c in range(A_WIDTH // LANES):
        roped = _rope_chunk(acc[:, c * LANES:(c + 1) * LANES], tab_ref, 0, A_ROT_DIM // 2)
        _store_by_class((ka0_ref, ka1_ref, ka2_ref), c, roped, stage_ref)

    acc = jnp.dot(h, wva_ref[...], preferred_element_type=F32)
    for c in range(A_WIDTH // LANES):
        _store_by_class((va0_ref, va1_ref, va2_ref), c, acc[:, c * LANES:(c + 1) * LANES], stage_ref)

    acc = jnp.dot(h, wqb_ref[...], preferred_element_type=F32)
    acc = acc * lax.rsqrt(_head_mean_sq(acc, blkq_ref) + EPS) * gq_ref[...]
    heads_per_chunk = LANES // HEAD_DIM
    for c in range(B_Q_WIDTH // LANES):
        roped = _rope_chunk(acc[:, c * LANES:(c + 1) * LANES], tab_ref, 3 * LANES, HEAD_DIM // 4) * Q_SCALE
        rt = roped.T.astype(BF16)
        for hh in range(heads_per_chunk):
            head = c * heads_per_chunk + hh
            g, hg = divmod(head, B_GROUP_HEADS)
            cols = slice(hg * tm, (hg + 1) * tm)
            qt_ref[g, g * HEAD_DIM:(g + 1) * HEAD_DIM, cols] = rt[hh * HEAD_DIM:(hh + 1) * HEAD_DIM, :]
            qt_ref[g, (1 - g) * HEAD_DIM:(2 - g) * HEAD_DIM, cols] = jnp.zeros((HEAD_DIM, tm), BF16)

    acc = jnp.dot(h, wkb_ref[...], preferred_element_type=F32)
    acc = acc * lax.rsqrt(_head_mean_sq(acc, blkk_ref) + EPS) * gk_ref[...]
    kb_ref[...] = _rope_chunk(acc, tab_ref, 3 * LANES, HEAD_DIM // 4).astype(BF16)

    vt = jnp.dot(h, wvb_ref[...], preferred_element_type=F32).T.astype(BF16)
    for g in range(B_KV_HEADS):
        vt_ref[g] = vt[g * HEAD_DIM:(g + 1) * HEAD_DIM, :]

    gates_ref[...] = jax.nn.sigmoid(jnp.dot(h, wg_ref[...], preferred_element_type=F32)).astype(BF16)


def _in_proj(xf, tab, b, s, gmix, w_in, gq, gk):
    n = xf.shape[0]
    tm = ROW_TILE
    nts = s // tm
    c1, c2, c3 = A_WIDTH, 2 * A_WIDTH, 3 * A_WIDTH
    c4 = c3 + B_Q_WIDTH
    c5 = c4 + B_KV_WIDTH
    c6 = c5 + B_KV_WIDTH
    wb = w_in.astype(BF16)
    ws = [wb[:, :c1], wb[:, c1:c2], wb[:, c2:c3], wb[:, c3:c4], wb[:, c4:c5], wb[:, c5:c6], wb[:, c6:]]

    def head_blockdiag(width):
        hid = jnp.arange(width) // HEAD_DIM
        return jnp.where(hid[:, None] == hid[None, :], 1.0 / HEAD_DIM, 0.0).astype(BF16)

    consts = [gmix.reshape(1, D_MODEL)] + ws + [
        jnp.tile(gq, B_Q_WIDTH // HEAD_DIM).reshape(1, B_Q_WIDTH),
        jnp.tile(gk, B_KV_WIDTH // HEAD_DIM).reshape(1, B_KV_WIDTH),
        head_blockdiag(B_Q_WIDTH), head_blockdiag(B_KV_WIDTH)]

    a_shapes, a_specs = [], []
    for _ in range(3):
        for dil in A_DILATIONS:
            a_shapes.append(jax.ShapeDtypeStruct((b, dil, s // dil, A_GROUP_WIDTH), BF16))
            a_specs.append(pl.BlockSpec((None, dil, tm // dil, A_GROUP_WIDTH),
                                        lambda i: (i // nts, 0, i % nts, 0)))
    kpb = B_TK // tm
    b_shapes = [jax.ShapeDtypeStruct((b, B_KV_HEADS, nts, B_KV_WIDTH, B_GROUP_HEADS * tm), BF16),
                jax.ShapeDtypeStruct((n, B_KV_WIDTH), BF16),
                jax.ShapeDtypeStruct((b, B_KV_HEADS, s // B_TK, HEAD_DIM, B_TK), BF16),
                jax.ShapeDtypeStruct((n, GATE_WIDTH), BF16)]
    b_specs = [pl.BlockSpec((None, B_KV_HEADS, None, B_KV_WIDTH, B_GROUP_HEADS * tm),
                            lambda i: (i // nts, 0, i % nts, 0, 0)),
               pl.BlockSpec((tm, B_KV_WIDTH), lambda i: (i, 0)),
               pl.BlockSpec((None, B_KV_HEADS, None, HEAD_DIM, tm),
                            lambda i: (i // nts, 0, (i % nts) // kpb, 0, i % kpb)),
               pl.BlockSpec((tm, GATE_WIDTH), lambda i: (i, 0))]

    return pl.pallas_call(
        _in_proj_kernel,
        out_shape=a_shapes + b_shapes,
        grid=(n // tm,),
        in_specs=[pl.BlockSpec((tm, D_MODEL), lambda i: (i, 0)),
                  pl.BlockSpec((tm, tab.shape[1]), lambda i: (i % nts, 0))] + [_full_spec(a) for a in consts],
        out_specs=a_specs + b_specs,
        scratch_shapes=[pltpu.VMEM((tm, LANES), F32)],
        compiler_params=_cparams(("parallel",)),
        name="in_proj",
    )(xf, tab, *consts)


def _band_bias():
    qi = jnp.arange(A_SUB)[:, None]
    kj = jnp.arange(A_SUB + 2 * A_RADIUS)[None, :]
    band = jnp.abs(kj - A_RADIUS - qi) <= A_RADIUS
    first = band & (kj >= A_RADIUS)
    last = band & (kj < A_SUB + A_RADIUS)
    return jnp.where(jnp.stack([band, first, last]), 0.0, NEG_INF).astype(F32)


def _attn_a_kernel(bias_ref, q_ref, kp_ref, kc_ref, kn_ref, vp_ref, vc_ref, vn_ref, o_ref, lse_ref):
    classes, tq, _ = q_ref.shape
    nsub = tq // A_SUB
    i = pl.program_id(2)
    last_tile = pl.num_programs(2) - 1
    first_head = lax.broadcasted_iota(jnp.int32, (1, LANES), 1) < HEAD_DIM
    nt_dims = (((1,), (1,)), ((), ()))

    def window(p_ref, c_ref, n_ref, c, u, cols):
        lo, hi = u * A_SUB - A_RADIUS, (u + 1) * A_SUB + A_RADIUS
        parts = [p_ref[c, :, cols]] if lo < 0 else []
        parts.append(c_ref[c, max(lo, 0):min(hi, tq), cols])
        if hi > tq:
            parts.append(n_ref[c, :, cols])
        return parts[0] if len(parts) == 1 else jnp.concatenate(parts, axis=0)

    for c in range(classes):
        for u in range(nsub):
            if u == 0:
                bias = bias_ref[jnp.where(i == 0, 1, 0)]
            elif u == nsub - 1:
                bias = bias_ref[jnp.where(i == last_tile, 2, 0)]
            else:
                bias = bias_ref[0]
            rows = slice(u * A_SUB, (u + 1) * A_SUB)
            for pair in range(A_GROUP_WIDTH // LANES):
                cols = slice(pair * LANES, (pair + 1) * LANES)
                k = window(kp_ref, kc_ref, kn_ref, c, u, cols)
                v = window(vp_ref, vc_ref, vn_ref, c, u, cols)
                q = q_ref[c, rows, cols]
                outs, lses = [], []
                for hh in range(LANES // HEAD_DIM):
                    mine = first_head if hh == 0 else jnp.logical_not(first_head)
                    qm = jnp.where(mine, q, jnp.zeros_like(q))
                    sc = lax.dot_general(qm, k, nt_dims, preferred_element_type=F32) + bias
                    m = jnp.max(sc, axis=1, keepdims=True)
                    p = jnp.exp2(sc - m)
                    den = jnp.sum(p, axis=1, keepdims=True)
                    outs.append(jnp.dot(p.astype(BF16), v, preferred_element_type=F32) / den)
                    lses.append(m * LN2 + jnp.log(den))
                o_ref[c, rows, cols] = jnp.where(first_head, outs[0], outs[1])
                lse_ref[c, rows, cols] = jnp.where(first_head, lses[0], lses[1])


def _attn_a(q, k, v, gi):
    b, dil, cl, _ = q.shape
    tq = min(A_TQ, cl)
    cb = min(dil, A_TQ // tq)
    assert tq >= 2 * A_SUB and A_SUB == 2 * A_RADIUS and dil % cb == 0
    hb = tq // A_RADIUS
    n_halo = cl // A_RADIUS
    bias = _band_bias()
    main = pl.BlockSpec((None, cb, tq, A_GROUP_WIDTH), lambda bi, r, i: (bi, r, i, 0))
    prev = pl.BlockSpec((None, cb, A_RADIUS, A_GROUP_WIDTH),
                        lambda bi, r, i: (bi, r, jnp.maximum(i * hb - 1, 0), 0))
    nxt = pl.BlockSpec((None, cb, A_RADIUS, A_GROUP_WIDTH),
                       lambda bi, r, i: (bi, r, jnp.minimum((i + 1) * hb, n_halo - 1), 0))
    return pl.pallas_call(
        _attn_a_kernel,
        out_shape=[jax.ShapeDtypeStruct((b, dil, cl, A_GROUP_WIDTH), F32)] * 2,
        grid=(b, dil // cb, cl // tq),
        in_specs=[_full_spec(bias), main, prev, main, nxt, prev, main, nxt],
        out_specs=[main, main],
        compiler_params=_cparams(("parallel", "parallel", "parallel")),
        name=f"attn_a{gi}",
    )(bias, q, k, k, k, v, v, v)


def _flash_b_kernel(qt_ref, k_ref, vt_ref, o_ref, acc_sc, l_sc, m_sc, p0_sc, p1_sc, *, online):
    nk, tk, _ = k_ref.shape
    width = qt_ref.shape[1]
    tq = width // B_GROUP_HEADS
    acc_sc[...] = jnp.zeros(acc_sc.shape, F32)
    l_sc[...] = jnp.zeros(l_sc.shape, F32)

    def scores(j):
        return jnp.dot(k_ref[j], qt_ref[...], preferred_element_type=F32)

    def column_sums(p):
        return p.reshape(tk // SUBLANES, SUBLANES, width).sum(axis=0)

    def accumulate(j, pb):
        vt = vt_ref[j]
        for h in range(B_GROUP_HEADS):
            cols = slice(h * tq, (h + 1) * tq)
            acc_sc[:, cols] += jnp.dot(vt, pb[:, cols], preferred_element_type=F32)

    if online:
        m_sc[...] = jnp.full(m_sc.shape, NEG_INF, F32)

        def body(j, carry):
            sc = scores(j)
            m_old = m_sc[...]
            m_new = jnp.maximum(m_old, jnp.max(sc, axis=0, keepdims=True))
            alpha = jnp.exp2(m_old - m_new)
            m_sc[...] = m_new
            p = jnp.exp2(sc - m_new)
            l_sc[...] = alpha * l_sc[...] + column_sums(p)
            acc_sc[...] = alpha * acc_sc[...]
            accumulate(j, p.astype(BF16))
            return carry

        lax.fori_loop(0, nk, body, 0)
    else:
        def exponentials(sc, p_out):
            p = jnp.exp2(sc)
            l_sc[...] += column_sums(p)
            p_out[...] = p.astype(BF16)

        def stage(j, p_in, p_out):
            sc = scores(j)
            accumulate(j - 1, p_in[...])
            exponentials(sc, p_out)

        exponentials(scores(0), p0_sc)

        def body(jj, carry):
            stage(2 * jj + 1, p0_sc, p1_sc)
            stage(2 * jj + 2, p1_sc, p0_sc)
            return carry

        lax.fori_loop(0, nk // 2 - 1, body, 0)
        stage(nk - 1, p0_sc, p1_sc)
        accumulate(nk - 1, p1_sc[...])
    o = acc_sc[...] / jnp.sum(l_sc[...], axis=0, keepdims=True)
    o = jnp.concatenate([o[:, h * tq:(h + 1) * tq] for h in range(B_GROUP_HEADS)], axis=0)
    o_ref[...] = o.T.astype(o_ref.dtype)


def _flash_b(qt, kb, vt, online):
    b, _, nq, _, width = qt.shape
    tq = width // B_GROUP_HEADS
    _, nk, tk, _ = kb.shape
    return pl.pallas_call(
        functools.partial(_flash_b_kernel, online=online),
        out_shape=jax.ShapeDtypeStruct((b, B_KV_HEADS, nq * tq, B_GROUP_WIDTH), BF16),
        grid=(b, B_KV_HEADS, nq),
        in_specs=[pl.BlockSpec((None, None, None, B_KV_WIDTH, width), lambda bi, g, i: (bi, g, i, 0, 0)),
                  pl.BlockSpec((None, nk, tk, B_KV_WIDTH), lambda bi, g, i: (bi, 0, 0, 0)),
                  pl.BlockSpec((None, None, nk, HEAD_DIM, tk), lambda bi, g, i: (bi, g, 0, 0, 0))],
        out_specs=pl.BlockSpec((None, None, tq, B_GROUP_WIDTH), lambda bi, g, i: (bi, g, i, 0)),
        scratch_shapes=[pltpu.VMEM((HEAD_DIM, width), F32), pltpu.VMEM((SUBLANES, width), F32),
                        pltpu.VMEM((1, width), F32),
                        pltpu.VMEM((tk, width), BF16), pltpu.VMEM((tk, width), BF16)],
        compiler_params=_cparams(("parallel", "parallel", "parallel")),
        name="flash_b_online" if online else "flash_b",
    )(qt, kb, vt)


def _stage_classes(blk_ref, stage_ref):
    dil, rows, width = blk_ref.shape
    if dil == 1:
        return
    for c in range(width // LANES):
        for r in range(dil):
            stage_ref.at[c][pl.ds(r, rows, stride=dil), :] = blk_ref[r, :, c * LANES:(c + 1) * LANES]


def _token_rows(blk_ref, stage_ref, rows):
    if blk_ref.shape[0] == 1:
        return blk_ref[0, rows, :]
    return jnp.concatenate([stage_ref[c, rows, :] for c in range(stage_ref.shape[0])], axis=1)


def _post_attn_kernel(x_ref, o0_ref, o1_ref, o2_ref, l0_ref, l1_ref, l2_ref, ob0_ref, ob1_ref, gates_ref,
                      wa_ref, wb0_ref, wb1_ref, wout_ref, gffn_ref, wrh_ref, wrl_ref,
                      x1_ref, h2_ref, aff_ref, so1, so2, sl1, sl2):
    tm = x_ref.shape[0]
    for blk, stage in ((l1_ref, sl1), (l2_ref, sl2), (o1_ref, so1), (o2_ref, so2)):
        _stage_classes(blk, stage)
    nt = (((1,), (1,)), ((), ()))
    hr = tm // POST_SPLIT
    for half in range(POST_SPLIT):
        rows = slice(half * hr, (half + 1) * hr)
        l0 = _token_rows(l0_ref, None, rows)
        l1 = _token_rows(l1_ref, sl1, rows)
        l2 = _token_rows(l2_ref, sl2, rows)
        m = jnp.maximum(jnp.maximum(l0, l1), l2)
        e0, e1, e2 = jnp.exp(l0 - m), jnp.exp(l1 - m), jnp.exp(l2 - m)
        oa = (e0 * _token_rows(o0_ref, None, rows) + e1 * _token_rows(o1_ref, so1, rows)
              + e2 * _token_rows(o2_ref, so2, rows)) / (e0 + e1 + e2)
        ya = jnp.dot(oa.astype(BF16), wa_ref[...], preferred_element_type=F32)
        yb = (jnp.dot(ob0_ref[rows, :], wb0_ref[...], preferred_element_type=F32)
              + jnp.dot(ob1_ref[rows, :], wb1_ref[...], preferred_element_type=F32))
        ga = gates_ref[rows, :D_MODEL].astype(F32)
        gb = gates_ref[rows, D_MODEL:].astype(F32)
        z = (ga * ya + gb * yb).astype(BF16)
        x1 = x_ref[rows, :] + jnp.dot(z, wout_ref[...], preferred_element_type=F32)
        x1_ref[rows, :] = x1
        ms = jnp.mean(x1 * x1, axis=-1, keepdims=True)
        h2 = x1 * lax.rsqrt(ms + EPS) * gffn_ref[...]
        hi = h2.astype(BF16)
        lo = (h2 - hi.astype(F32)).astype(BF16)
        h2_ref[rows, :] = hi
        both = lax.dot_general(wrl_ref[...], hi, nt, preferred_element_type=F32)
        logits = (both[:N_EXPERTS] + both[N_EXPERTS:]
                  + lax.dot_general(wrh_ref[...], lo, nt, preferred_element_type=F32))
        logits = logits - jnp.max(logits, axis=0, keepdims=True)
        e = jnp.exp(logits)
        aff = e / jnp.sum(e, axis=0, keepdims=True)
        for j in range(hr // LANES):
            aff_ref[half * (hr // LANES) + j] = aff[:, j * LANES:(j + 1) * LANES]


def _post_attn(xf, oas, lses, ob, gates, b, s, w_a, w_b, w_out, g_ffn, w_router):
    n = xf.shape[0]
    tm = ROW_TILE
    nts = s // tm
    wa = w_a.astype(BF16)
    wb = w_b.astype(BF16)
    wb0, wb1 = wb[:B_GROUP_WIDTH], wb[B_GROUP_WIDTH:]
    wout = w_out.astype(BF16)
    wrt = w_router.T
    wrh = wrt.astype(BF16)
    wrl = jnp.concatenate([wrh, (wrt - wrh.astype(F32)).astype(BF16)], axis=0)
    consts = [wa, wb0, wb1, wout, g_ffn.reshape(1, D_MODEL), wrh, wrl]

    def rows(w):
        return pl.BlockSpec((tm, w), lambda i: (i, 0))

    def class_spec(dil):
        return pl.BlockSpec((None, dil, tm // dil, A_GROUP_WIDTH), lambda i: (i // nts, 0, i % nts, 0))

    def ob_spec(g):
        return pl.BlockSpec((None, None, tm, B_GROUP_WIDTH), lambda i: (i // nts, g, i % nts, 0))

    a_specs = [class_spec(d) for d in A_DILATIONS]
    return pl.pallas_call(
        _post_attn_kernel,
        out_shape=[jax.ShapeDtypeStruct((n, D_MODEL), F32),
                   jax.ShapeDtypeStruct((n, D_MODEL), BF16),
                   jax.ShapeDtypeStruct((n // LANES, N_EXPERTS, LANES), F32)],
        grid=(n // tm,),
        in_specs=[rows(D_MODEL)] + a_specs + a_specs + [ob_spec(0), ob_spec(1), rows(GATE_WIDTH)]
                 + [_full_spec(a) for a in consts],
        out_specs=[rows(D_MODEL), rows(D_MODEL),
                   pl.BlockSpec((tm // LANES, N_EXPERTS, LANES), lambda i: (i, 0, 0))],
        scratch_shapes=[pltpu.VMEM((A_GROUP_WIDTH // LANES, tm, LANES), F32)] * 4,
        compiler_params=_cparams(("parallel",)),
        name="post_attn",
    )(xf, *oas, *lses, ob, ob, gates, *consts)


def _route_kernel(aff_ref, tri_ref, posl_ref, starts_ref, sel_sc, *, cap, idx_bits, tiles_per_block):
    nlt = aff_ref.shape[0]
    capf = float(cap)

    def count(mask):
        part = jnp.sum(jnp.where(mask, 1.0, 0.0), axis=0)
        return jnp.sum(part, axis=1, keepdims=True)[None]

    def value_body(i, tau):
        bits = pltpu.bitcast(aff_ref[...], jnp.int32)
        cand = tau | lax.shift_left(jnp.int32(1), 30 - i)
        return jnp.where(count(bits >= cand) >= capf, cand, tau)

    tau = lax.fori_loop(0, 31, value_body, jnp.zeros((1, N_EXPERTS, 1), jnp.int32))
    bits = pltpu.bitcast(aff_ref[...], jnp.int32)
    need = capf - count(bits > tau)
    tok = (lax.broadcasted_iota(jnp.int32, bits.shape, 0) * LANES
           + lax.broadcasted_iota(jnp.int32, bits.shape, 2))

    def index_body(i, last):
        b = pltpu.bitcast(aff_ref[...], jnp.int32)
        cand = last | lax.shift_left(jnp.int32(1), idx_bits - 1 - i)
        return jnp.where(count((b == tau) & (tok < cand)) < need, cand, last)

    last = lax.fori_loop(0, idx_bits, index_body, jnp.zeros((1, N_EXPERTS, 1), jnp.int32))
    sel_sc[...] = jnp.where((bits > tau) | ((bits == tau) & (tok <= last)), 1.0, 0.0)

    sel = sel_sc[...]
    sb = sel.reshape(nlt * N_EXPERTS, LANES).astype(BF16)
    tri = tri_ref[...]
    within = jnp.dot(sb, tri, preferred_element_type=F32).reshape(nlt, N_EXPERTS, LANES)
    total = jnp.dot(sb, jnp.ones((LANES, LANES), BF16),
                    preferred_element_type=F32).reshape(nlt, N_EXPERTS, LANES)
    upto = total
    step = 1
    while step < nlt:
        upto = upto + jnp.concatenate([jnp.zeros((step, N_EXPERTS, LANES), F32), upto[:nlt - step]], axis=0)
        step *= 2
    before = upto - total
    posl_ref[...] = jnp.where(sel > 0, before + within - sel, -1.0).astype(jnp.int32)
    starts_ref[...] = before.reshape(nlt // tiles_per_block, tiles_per_block, N_EXPERTS, LANES)[:, 0].astype(jnp.int32)


def _route(aff3, cap):
    nlt = aff3.shape[0]
    n = nlt * LANES
    tpb = MOE_T // LANES
    tri = (jnp.arange(LANES)[:, None] <= jnp.arange(LANES)[None, :]).astype(BF16)
    vm = pl.BlockSpec(memory_space=pltpu.VMEM)
    return pl.pallas_call(
        functools.partial(_route_kernel, cap=cap, idx_bits=(n - 1).bit_length(), tiles_per_block=tpb),
        out_shape=[jax.ShapeDtypeStruct((nlt, N_EXPERTS, LANES), jnp.int32),
                   jax.ShapeDtypeStruct((n // MOE_T, N_EXPERTS, LANES), jnp.int32)],
        in_specs=[vm, vm],
        out_specs=[vm, vm],
        scratch_shapes=[pltpu.VMEM((nlt, N_EXPERTS, LANES), F32)],
        compiler_params=pltpu.CompilerParams(vmem_limit_bytes=VMEM_LIMIT),
        name="route",
    )(aff3, tri)


def _block_slots(s_ref, i, nt):
    return [(s_ref[e * (nt + 1) + i], s_ref[e * (nt + 1) + i + 1]) for e in range(N_EXPERTS)]


def _align_down(x):
    return (x >> MOE_ALIGN_SHIFT) << MOE_ALIGN_SHIFT


def _moe_gather_kernel(s_ref, aff_ref, posl_ref, h2_ref, xs_hbm, tm_ref,
                       haug_sc, pall_sc, stage_sc, carry_sc, pending_sc, sem, *, nt, cap):
    i = pl.program_id(0)
    nj = posl_ref.shape[0]
    ws = MOE_WS

    @pl.when(i == 0)
    def _():
        carry_sc[...] = jnp.zeros(carry_sc.shape, BF16)
        pending_sc[0] = 0
        pending_sc[1] = 0
        pad = xs_hbm.shape[1] - cap
        stage_sc[0] = jnp.zeros(stage_sc.shape[1:], BF16)
        fills = [pltpu.make_async_copy(stage_sc.at[0, pl.ds(0, pad)], xs_hbm.at[e, pl.ds(cap, pad)], sem)
                 for e in range(N_EXPERTS)]
        for cp in fills:
            cp.start()
        for cp in fills:
            cp.wait()

    for j in range(nj):
        pj = posl_ref[j]
        gj = jnp.where(pj >= 0, aff_ref[j], 0.0)
        blk = jnp.concatenate([gj, pj.astype(F32), gj, jnp.zeros((LANES - 3 * N_EXPERTS, LANES), F32)], axis=0)
        tm_ref[j * LANES:(j + 1) * LANES, :] = blk.T
    tmv = tm_ref[...]
    lane = lax.broadcasted_iota(jnp.int32, tmv.shape, 1)
    ghi = tmv.astype(BF16)
    glo = (tmv - ghi.astype(F32)).astype(BF16)
    zero = jnp.zeros_like(ghi)
    haug_sc[:, :D_MODEL] = h2_ref[...]
    haug_sc[:, D_MODEL:] = jnp.where(lane < N_EXPERTS, ghi,
                                     jnp.where((lane >= TM_GATE_LO) & (lane < TM_GATE_LO + N_EXPERTS), glo, zero))

    astarts, kks, offcs = [], [], []
    nw = jnp.int32(0)
    for st, en in _block_slots(s_ref, i, nt):
        ast = _align_down(st)
        span = _align_down(en) - ast
        kk = span >> MOE_WS_SHIFT
        astarts.append(ast)
        kks.append(kk)
        offcs.append(span - kk * ws)
        nw = jnp.maximum(nw, kk + 1)

    def wait_window():
        for _ in range(N_EXPERTS):
            pltpu.make_async_copy(stage_sc.at[0, pl.ds(0, ws)], xs_hbm.at[0, pl.ds(0, ws)], sem).wait()

    row = lax.broadcasted_iota(jnp.int32, (ws, LANES), 0)

    def window(k, carry):
        for e in range(N_EXPERTS):
            base = astarts[e] + k * ws
            for j in range(nj):
                hit = (posl_ref[j, e:e + 1, :] - base) == row
                pall_sc[e * ws:(e + 1) * ws, j * LANES:(j + 1) * LANES] = jnp.where(hit, 1.0, 0.0).astype(BF16)
        stage = stage_sc.at[pending_sc[1]]
        stage[...] = jnp.dot(pall_sc[...], haug_sc[...], preferred_element_type=F32).astype(BF16)

        @pl.when(k == 0)
        def _():
            for e in range(N_EXPERTS):
                stage[e * ws:e * ws + MOE_ALIGN, :] += carry_sc[e]

        for e in range(N_EXPERTS):
            @pl.when(k == kks[e])
            def _():
                carry_sc[e] = stage[pl.ds(pl.multiple_of(e * ws + offcs[e], MOE_ALIGN), MOE_ALIGN), :]

        @pl.when(pending_sc[0] == 1)
        def _():
            wait_window()

        for e in range(N_EXPERTS):
            base = pl.multiple_of(astarts[e] + k * ws, MOE_ALIGN)
            pltpu.make_async_copy(stage.at[pl.ds(e * ws, ws)], xs_hbm.at[e, pl.ds(base, ws)], sem).start()
        pending_sc[0] = 1
        pending_sc[1] = 1 - pending_sc[1]
        return carry

    lax.fori_loop(0, nw, window, 0)

    @pl.when((i == nt - 1) & (pending_sc[0] == 1))
    def _():
        wait_window()
        pending_sc[0] = 0


def _moe_gather(s_flat, aff3, posl, h2, cap):
    n = h2.shape[0]
    t = MOE_T
    nt = n // t
    cap_pad = cap + (t // MOE_WS + 1) * MOE_WS
    blk3 = pl.BlockSpec((t // LANES, N_EXPERTS, LANES), lambda i, s: (i, 0, 0))
    grid_spec = pltpu.PrefetchScalarGridSpec(
        num_scalar_prefetch=1,
        grid=(nt,),
        in_specs=[blk3, blk3, pl.BlockSpec((t, D_MODEL), lambda i, s: (i, 0))],
        out_specs=[pl.BlockSpec(memory_space=pl.ANY), pl.BlockSpec((t, LANES), lambda i, s: (i, 0))],
        scratch_shapes=[pltpu.VMEM((t, XS_WIDTH), BF16),
                        pltpu.VMEM((N_EXPERTS * MOE_WS, t), BF16),
                        pltpu.VMEM((2, N_EXPERTS * MOE_WS, XS_WIDTH), BF16),
                        pltpu.VMEM((N_EXPERTS, MOE_ALIGN, XS_WIDTH), BF16),
                        pltpu.SMEM((2,), jnp.int32),
                        pltpu.SemaphoreType.DMA(())])
    return pl.pallas_call(
        functools.partial(_moe_gather_kernel, nt=nt, cap=cap),
        out_shape=[jax.ShapeDtypeStruct((N_EXPERTS, cap_pad, XS_WIDTH), BF16),
                   jax.ShapeDtypeStruct((n, LANES), F32)],
        grid_spec=grid_spec,
        compiler_params=_cparams(("arbitrary",)),
        name="moe_gather",
    )(s_flat, aff3, posl, h2)


def _moe_ffn_kernel(xs_ref, w1_ref, w3_ref, w2_ref, ye_ref):
    xe = xs_ref[:, :D_MODEL]
    route = xs_ref[:, D_MODEL:].astype(F32)
    lane = lax.broadcasted_iota(jnp.int32, route.shape, 1)
    e = pl.program_id(0)
    gate = jnp.sum(jnp.where((lane == e) | (lane == e + TM_GATE_LO), route, 0.0), axis=1, keepdims=True)
    a = jnp.dot(xe, w1_ref[...], preferred_element_type=F32)
    u = jnp.dot(xe, w3_ref[...], preferred_element_type=F32)
    hid = (a * jax.nn.sigmoid(a) * u).astype(BF16)
    ye = jnp.dot(hid, w2_ref[...], preferred_element_type=F32) * gate
    ye_ref[...] = ye.astype(ye_ref.dtype)


def _moe_ffn(xs, cap, w1, w3, w2):
    e = xs.shape[0]
    tc = min(FFN_TC, cap)
    wspec = pl.BlockSpec((None, D_MODEL, D_MODEL), lambda ei, ci: (ei, 0, 0))
    return pl.pallas_call(
        _moe_ffn_kernel,
        out_shape=jax.ShapeDtypeStruct((e, cap, D_MODEL), BF16),
        grid=(e, cap // tc),
        in_specs=[pl.BlockSpec((None, tc, XS_WIDTH), lambda ei, ci: (ei, ci, 0)), wspec, wspec, wspec],
        out_specs=pl.BlockSpec((None, tc, D_MODEL), lambda ei, ci: (ei, ci, 0)),
        compiler_params=_cparams(("parallel", "arbitrary")),
        name="moe_ffn",
    )(xs, w1, w3, w2)


def _final_kernel(s_ref, x1_ref, tm_ref, p_ref, ye_hbm, wple_ref, wpg_ref, gple_ref, gfin_ref, expand_ref, out_ref,
                  buf_sc, pall_sc, y_sc, sem, *, nt, cap):
    i = pl.program_id(0)
    ws = MOE_WS
    t = x1_ref.shape[0]
    slot = i % 2

    def window_rows(blk, k):
        out = []
        for st, _ in _block_slots(s_ref, blk, nt):
            lo = _align_down(st) + k * ws
            out.append((lo, pl.multiple_of(jnp.minimum(lo, cap - ws), MOE_ALIGN)))
        return out

    def copies(rows, buf, which):
        return [pltpu.make_async_copy(ye_hbm.at[e, pl.ds(base, ws)], buf_sc.at[buf, pl.ds(e * ws, ws)], sem.at[which])
                for e, (_, base) in enumerate(rows)]

    @pl.when(i == 0)
    def _():
        for cp in copies(window_rows(0, 0), 0, 0):
            cp.start()

    @pl.when(i + 1 < nt)
    def _():
        for cp in copies(window_rows(i + 1, 0), 1 - slot, 1 - slot):
            cp.start()

    nw = jnp.int32(0)
    for st, en in _block_slots(s_ref, i, nt):
        nw = jnp.maximum(nw, (en - _align_down(st) + ws - 1) >> MOE_WS_SHIFT)
    lane = lax.broadcasted_iota(jnp.int32, (1, LANES), 1)
    col = (lax.broadcasted_iota(jnp.int32, (1, N_EXPERTS * ws), 1) & (ws - 1)).astype(F32)

    def onehot(rows):
        basev = jnp.zeros((1, LANES), F32)
        lov = jnp.full((1, LANES), float(2 ** 30), F32)
        for e, (lo, base) in enumerate(rows):
            basev = jnp.where(lane == TM_POS + e, base.astype(F32), basev)
            lov = jnp.where(lane == TM_POS + e, lo.astype(F32), lov)
        tmv = tm_ref[...]
        off = tmv - basev
        off = jnp.where((tmv >= lov) & (off >= 0.0) & (off < float(ws)), off, -1.0)
        spread = jnp.dot(off.astype(BF16), expand_ref[...], preferred_element_type=F32)
        pall_sc[...] = jnp.where(spread == col, 1.0, 0.0).astype(BF16)

    rows0 = window_rows(i, 0)
    onehot(rows0)
    for cp in copies(rows0, slot, slot):
        cp.wait()
    y_sc[...] = jnp.dot(pall_sc[...], buf_sc[slot], preferred_element_type=F32)

    def window(k, carry):
        rows = window_rows(i, k)
        cps = copies(rows, slot, slot)
        for cp in cps:
            cp.start()
        onehot(rows)
        for cp in cps:
            cp.wait()
        y_sc[...] += jnp.dot(pall_sc[...], buf_sc[slot], preferred_element_type=F32)
        return carry

    lax.fori_loop(1, nw, window, 0)

    hr = t // POST_SPLIT
    for half in range(POST_SPLIT):
        rows = slice(half * hr, (half + 1) * hr)
        x2 = x1_ref[rows, :] + y_sc[rows, :]
        ms = jnp.mean(x2 * x2, axis=-1, keepdims=True)
        hp = (x2 * lax.rsqrt(ms + EPS) * gple_ref[...]).astype(BF16)
        gt = jax.nn.sigmoid(jnp.dot(hp, wpg_ref[...], preferred_element_type=F32))
        emb = jnp.dot(p_ref[rows, :].astype(BF16), wple_ref[...], preferred_element_type=F32)
        x3 = x2 + emb * gt
        ms3 = jnp.mean(x3 * x3, axis=-1, keepdims=True)
        out_ref[rows, :] = x3 * lax.rsqrt(ms3 + EPS) * gfin_ref[...]


def _final(s_flat, x1, tm, pf, ye, w_ple, w_ple_gate, g_ple, g_final):
    n = x1.shape[0]
    t = MOE_T
    nt = n // t
    cap = ye.shape[1]
    src = jnp.arange(LANES)[:, None] - TM_POS
    dst = jnp.arange(N_EXPERTS * MOE_WS)[None, :] >> MOE_WS_SHIFT
    expand = (src == dst).astype(BF16)
    consts = [w_ple.astype(BF16), w_ple_gate.astype(BF16), g_ple.reshape(1, D_MODEL), g_final.reshape(1, D_MODEL),
              expand]

    def rows(w):
        return pl.BlockSpec((t, w), lambda i, s: (i, 0))

    grid_spec = pltpu.PrefetchScalarGridSpec(
        num_scalar_prefetch=1,
        grid=(nt,),
        in_specs=[rows(D_MODEL), rows(LANES), rows(PLE_DIM), pl.BlockSpec(memory_space=pl.ANY)]
                 + [_full_spec(a) for a in consts],
        out_specs=rows(D_MODEL),
        scratch_shapes=[pltpu.VMEM((2, N_EXPERTS * MOE_WS, D_MODEL), BF16),
                        pltpu.VMEM((t, N_EXPERTS * MOE_WS), BF16),
                        pltpu.VMEM((t, D_MODEL), F32),
                        pltpu.SemaphoreType.DMA((2,))])
    return pl.pallas_call(
        functools.partial(_final_kernel, nt=nt, cap=cap),
        out_shape=jax.ShapeDtypeStruct((n, D_MODEL), F32),
        grid_spec=grid_spec,
        compiler_params=_cparams(("arbitrary",)),
        name="final",
    )(s_flat, x1, tm, pf, ye, *consts)


def _run_group(x, p, g_mix, w_in, g_qn, g_kn, w_a_proj, w_b_proj, w_out, g_ffn, w_router,
               w1, w3, w2, g_ple, w_ple, w_ple_gate, g_final):
    b, s, _ = x.shape
    n = b * s
    assert s % B_TK == 0 and s % (A_DILATIONS[-1] * A_RADIUS) == 0
    xf = x.reshape(n, D_MODEL)
    tab = _rope_tables(s)
    outs = _in_proj(xf, tab, b, s, g_mix, w_in, g_qn, g_kn)
    qas, kas, vas = outs[0:3], outs[3:6], outs[6:9]
    qt, kb, vt, gates = outs[9:]

    oas, lses = [], []
    for gi, (window, dil) in enumerate(A_PATTERNS):
        assert window // (2 * dil) == A_RADIUS
        o, lse = _attn_a(qas[gi], kas[gi], vas[gi], gi)
        oas.append(o)
        lses.append(lse)

    score_bound = (HEAD_DIM * Q_SCALE * 1.02) * jnp.max(jnp.abs(g_qn)) * jnp.max(jnp.abs(g_kn))
    kb4 = kb.reshape(b, s // B_TK, B_TK, B_KV_WIDTH)
    ob = lax.cond(score_bound <= B_UNSHIFTED_SCORE_MAX,
                  lambda: _flash_b(qt, kb4, vt, online=False),
                  lambda: _flash_b(qt, kb4, vt, online=True))

    x1, h2, aff3 = _post_attn(xf, oas, lses, ob, gates, b, s, w_a_proj, w_b_proj, w_out, g_ffn, w_router)

    cap = CAPACITY_FACTOR * n // N_EXPERTS
    assert cap % MOE_WS == 0 and cap >= MOE_WS
    posl, starts = _route(aff3, cap)
    s_flat = jnp.concatenate([starts[:, :, 0].T, jnp.full((N_EXPERTS, 1), cap, jnp.int32)], axis=1).reshape(-1)
    xs, tm = _moe_gather(s_flat, aff3, posl, h2, cap)
    ye = _moe_ffn(xs, cap, w1, w3, w2)
    out = _final(s_flat, x1, tm, p.reshape(n, PLE_DIM), ye, w_ple, w_ple_gate, g_ple, g_final)
    return out.reshape(b, s, D_MODEL)


def kernel(x_prompt, x_sample, p_prompt, p_sample, g_mix, w_in, g_qn, g_kn, w_a_proj, w_b_proj, w_out, g_ffn,
           w_router, w_exp_gate, w_exp_up, w_exp_down, g_ple, w_ple, w_ple_gate, g_final):
    assert g_mix.shape[0] == 1, "single layer"
    w1 = w_exp_gate[0].astype(BF16)
    w3 = w_exp_up[0].astype(BF16)
    w2 = w_exp_down[0].astype(BF16)
    args = (g_mix[0], w_in[0], g_qn[0], g_kn[0], w_a_proj[0], w_b_proj[0], w_out[0], g_ffn[0], w_router[0],
            w1, w3, w2, g_ple[0], w_ple[0], w_ple_gate[0], g_final)
    y_prompt = _run_group(x_prompt, p_prompt[0], *args)
    y_sample = _run_group(x_sample, p_sample[0], *args)
    return (y_prompt, y_sample)
```

```python
import functools
import math

import jax
import jax.numpy as jnp
from jax import lax
from jax.experimental import pallas as pl
from jax.experimental.pallas import tpu as pltpu

F32 = jnp.float32
BF16 = jnp.bfloat16

D_MODEL = 1024
HEAD_DIM = 64
A_PATTERNS = ((128, 1), (512, 4), (2048, 16))
A_DILATIONS = tuple(d for _, d in A_PATTERNS)
A_GROUPS = len(A_PATTERNS)
A_HEADS_PER_GROUP = 4
A_WIDTH = 768
A_GROUP_WIDTH = A_HEADS_PER_GROUP * HEAD_DIM
A_ROT_DIM = 16
A_RADIUS = 64
ROPE_THETA = 500000.0
B_Q_WIDTH = 512
B_KV_HEADS = 2
B_KV_WIDTH = 128
B_GROUP_HEADS = 4
B_GROUP_WIDTH = B_GROUP_HEADS * HEAD_DIM
AXIAL_THETA = 10000.0
GRID_W = 64
GATE_WIDTH = 2048
N_EXPERTS = 16
CAPACITY_FACTOR = 2
PLE_DIM = 256
EPS = 1e-6
NEG_INF = -1e30
LN2 = math.log(2.0)
Q_SCALE = (HEAD_DIM ** -0.5) / LN2
B_UNSHIFTED_SCORE_MAX = 50.0

LANES = 128
SUBLANES = 8
VMEM_LIMIT = 48 * 1024 * 1024

ROW_TILE = 512
A_TQ = 1024
A_SUB = 128
B_TQ = ROW_TILE
B_TK = 4 * ROW_TILE
FFN_TC = 1024
MOE_T = ROW_TILE
MOE_WS_SHIFT = 7
MOE_WS = 1 << MOE_WS_SHIFT
MOE_ALIGN_SHIFT = 4
MOE_ALIGN = 1 << MOE_ALIGN_SHIFT
XS_WIDTH = D_MODEL + LANES
TM_POS = N_EXPERTS
TM_GATE_LO = 2 * N_EXPERTS


def _cparams(sem):
    return pltpu.CompilerParams(dimension_semantics=sem, vmem_limit_bytes=VMEM_LIMIT)


def _full_spec(a):
    return pl.BlockSpec(a.shape, lambda *_: (0,) * a.ndim)


def _rope_tables(s):
    pos = jnp.arange(s)
    posf = pos.astype(F32)
    inv_a = jnp.power(ROPE_THETA, -jnp.arange(0, A_ROT_DIM, 2, dtype=F32) / A_ROT_DIM)
    ang = posf[:, None] * inv_a[None, :]
    ca, sa = jnp.cos(ang), jnp.sin(ang)
    z8 = jnp.zeros_like(sa)
    rest = HEAD_DIM - A_ROT_DIM
    cos_a = jnp.concatenate([ca, ca, jnp.ones((s, rest), F32)], axis=1)
    sp_a = jnp.concatenate([z8, sa, jnp.zeros((s, rest), F32)], axis=1)
    sm_a = jnp.concatenate([-sa, z8, jnp.zeros((s, rest), F32)], axis=1)

    hb = HEAD_DIM // 2
    inv_b = jnp.power(AXIAL_THETA, -jnp.arange(0, hb, 2, dtype=F32) / hb)
    row = (pos // GRID_W).astype(F32)
    col = (pos % GRID_W).astype(F32)
    ar = row[:, None] * inv_b[None, :]
    ac = col[:, None] * inv_b[None, :]
    cr, sr, cc, sc = jnp.cos(ar), jnp.sin(ar), jnp.cos(ac), jnp.sin(ac)
    z16 = jnp.zeros_like(sr)
    cos_b = jnp.concatenate([cr, cr, cc, cc], axis=1)
    sp_b = jnp.concatenate([z16, sr, z16, sc], axis=1)
    sm_b = jnp.concatenate([-sr, z16, -sc, z16], axis=1)
    heads_per_tile = LANES // HEAD_DIM
    parts = [jnp.tile(t, (1, heads_per_tile)) for t in (cos_a, sp_a, sm_a, cos_b, sp_b, sm_b)]
    return jnp.concatenate(parts, axis=1)


def _rope_chunk(x, tab_ref, base, half):
    cos = tab_ref[:, base:base + LANES]
    sp = tab_ref[:, base + LANES:base + 2 * LANES]
    sm = tab_ref[:, base + 2 * LANES:base + 3 * LANES]
    return x * cos + pltpu.roll(x, half, 1) * sp + pltpu.roll(x, LANES - half, 1) * sm


def _head_mean_sq(acc, blk_ref):
    sq = acc * acc
    hi = sq.astype(BF16)
    lo = (sq - hi.astype(F32)).astype(BF16)
    blk = blk_ref[...]
    return (jnp.dot(hi, blk, preferred_element_type=F32)
            + jnp.dot(lo, blk, preferred_element_type=F32))


def _store_by_class(out_refs, chunk_idx, chunk, stage_ref):
    g, half = divmod(chunk_idx, A_GROUP_WIDTH // LANES)
    dil = A_DILATIONS[g]
    lanes = slice(half * LANES, (half + 1) * LANES)
    if dil == 1:
        out_refs[g][0, :, lanes] = chunk.astype(BF16)
        return
    tm = chunk.shape[0]
    stage_ref[...] = chunk
    for r in range(dil):
        out_refs[g][r, :, lanes] = stage_ref[pl.ds(r, tm // dil, stride=dil), :].astype(BF16)


def _in_proj_kernel(x_ref, tab_ref, gmix_ref, wqa_ref, wka_ref, wva_ref, wqb_ref, wkb_ref, wvb_ref,
                    wg_ref, gq_ref, gk_ref, blkq_ref, blkk_ref,
                    qa0_ref, qa1_ref, qa2_ref, ka0_ref, ka1_ref, ka2_ref, va0_ref, va1_ref, va2_ref,
                    qt_ref, kb_ref, vt_ref, gates_ref, stage_ref):
    x = x_ref[...]
    tm = x.shape[0]
    ms = jnp.mean(x * x, axis=-1, keepdims=True)
    h = (x * lax.rsqrt(ms + EPS) * gmix_ref[...]).astype(BF16)

    acc = jnp.dot(h, wqa_ref[...], preferred_element_type=F32)
    for c in range(A_WIDTH // LANES):
        roped = _rope_chunk(acc[:, c * LANES:(c + 1) * LANES], tab_ref, 0, A_ROT_DIM // 2) * Q_SCALE
        _store_by_class((qa0_ref, qa1_ref, qa2_ref), c, roped, stage_ref)

    acc = jnp.dot(h, wka_ref[...], preferred_element_type=F32)
    for c in range(A_WIDTH // LANES):
        roped = _rope_chunk(acc[:, c * LANES:(c + 1) * LANES], tab_ref, 0, A_ROT_DIM // 2)
        _store_by_class((ka0_ref, ka1_ref, ka2_ref), c, roped, stage_ref)

    acc = jnp.dot(h, wva_ref[...], preferred_element_type=F32)
    for c in range(A_WIDTH // LANES):
        _store_by_class((va0_ref, va1_ref, va2_ref), c, acc[:, c * LANES:(c + 1) * LANES], stage_ref)

    acc = jnp.dot(h, wqb_ref[...], preferred_element_type=F32)
    acc = acc * lax.rsqrt(_head_mean_sq(acc, blkq_ref) + EPS) * gq_ref[...]
    heads_per_chunk = LANES // HEAD_DIM
    for c in range(B_Q_WIDTH // LANES):
        roped = _rope_chunk(acc[:, c * LANES:(c + 1) * LANES], tab_ref, 3 * LANES, HEAD_DIM // 4) * Q_SCALE
        rt = roped.T.astype(BF16)
        for hh in range(heads_per_chunk):
            head = c * heads_per_chunk + hh
            g, hg = divmod(head, B_GROUP_HEADS)
            cols = slice(hg * tm, (hg + 1) * tm)
            qt_ref[g, g * HEAD_DIM:(g + 1) * HEAD_DIM, cols] = rt[hh * HEAD_DIM:(hh + 1) * HEAD_DIM, :]
            qt_ref[g, (1 - g) * HEAD_DIM:(2 - g) * HEAD_DIM, cols] = jnp.zeros((HEAD_DIM, tm), BF16)

    acc = jnp.dot(h, wkb_ref[...], preferred_element_type=F32)
    acc = acc * lax.rsqrt(_head_mean_sq(acc, blkk_ref) + EPS) * gk_ref[...]
    kb_ref[...] = _rope_chunk(acc, tab_ref, 3 * LANES, HEAD_DIM // 4).astype(BF16)

    vt = jnp.dot(h, wvb_ref[...], preferred_element_type=F32).T.astype(BF16)
    for g in range(B_KV_HEADS):
        vt_ref[g] = vt[g * HEAD_DIM:(g + 1) * HEAD_DIM, :]

    gates_ref[...] = jax.nn.sigmoid(jnp.dot(h, wg_ref[...], preferred_element_type=F32)).astype(BF16)


def _in_proj(xf, tab, b, s, gmix, w_in, gq, gk):
    n = xf.shape[0]
    tm = ROW_TILE
    nts = s // tm
    c1, c2, c3 = A_WIDTH, 2 * A_WIDTH, 3 * A_WIDTH
    c4 = c3 + B_Q_WIDTH
    c5 = c4 + B_KV_WIDTH
    c6 = c5 + B_KV_WIDTH
    wb = w_in.astype(BF16)
    ws = [wb[:, :c1], wb[:, c1:c2], wb[:, c2:c3], wb[:, c3:c4], wb[:, c4:c5], wb[:, c5:c6], wb[:, c6:]]

    def head_blockdiag(width):
        hid = jnp.arange(width) // HEAD_DIM
        return jnp.where(hid[:, None] == hid[None, :], 1.0 / HEAD_DIM, 0.0).astype(BF16)

    consts = [gmix.reshape(1, D_MODEL)] + ws + [
        jnp.tile(gq, B_Q_WIDTH // HEAD_DIM).reshape(1, B_Q_WIDTH),
        jnp.tile(gk, B_KV_WIDTH // HEAD_DIM).reshape(1, B_KV_WIDTH),
        head_blockdiag(B_Q_WIDTH), head_blockdiag(B_KV_WIDTH)]

    a_shapes, a_specs = [], []
    for _ in range(3):
        for dil in A_DILATIONS:
            a_shapes.append(jax.ShapeDtypeStruct((b, dil, s // dil, A_GROUP_WIDTH), BF16))
            a_specs.append(pl.BlockSpec((None, dil, tm // dil, A_GROUP_WIDTH),
                                        lambda i: (i // nts, 0, i % nts, 0)))
    kpb = B_TK // tm
    b_shapes = [jax.ShapeDtypeStruct((b, B_KV_HEADS, nts, B_KV_WIDTH, B_GROUP_HEADS * tm), BF16),
                jax.ShapeDtypeStruct((n, B_KV_WIDTH), BF16),
                jax.ShapeDtypeStruct((b, B_KV_HEADS, s // B_TK, HEAD_DIM, B_TK), BF16),
                jax.ShapeDtypeStruct((n, GATE_WIDTH), BF16)]
    b_specs = [pl.BlockSpec((None, B_KV_HEADS, None, B_KV_WIDTH, B_GROUP_HEADS * tm),
                            lambda i: (i // nts, 0, i % nts, 0, 0)),
               pl.BlockSpec((tm, B_KV_WIDTH), lambda i: (i, 0)),
               pl.BlockSpec((None, B_KV_HEADS, None, HEAD_DIM, tm),
                            lambda i: (i // nts, 0, (i % nts) // kpb, 0, i % kpb)),
               pl.BlockSpec((tm, GATE_WIDTH), lambda i: (i, 0))]

    return pl.pallas_call(
        _in_proj_kernel,
        out_shape=a_shapes + b_shapes,
        grid=(n // tm,),
        in_specs=[pl.BlockSpec((tm, D_MODEL), lambda i: (i, 0)),
                  pl.BlockSpec((tm, tab.shape[1]), lambda i: (i % nts, 0))] + [_full_spec(a) for a in consts],
        out_specs=a_specs + b_specs,
        scratch_shapes=[pltpu.VMEM((tm, LANES), F32)],
        compiler_params=_cparams(("parallel",)),
        name="in_proj",
    )(xf, tab, *consts)


def _band_bias():
    qi = jnp.arange(A_SUB)[:, None]
    kj = jnp.arange(A_SUB + 2 * A_RADIUS)[None, :]
    band = jnp.abs(kj - A_RADIUS - qi) <= A_RADIUS
    first = band & (kj >= A_RADIUS)
    last = band & (kj < A_SUB + A_RADIUS)
    return jnp.where(jnp.stack([band, first, last]), 0.0, NEG_INF).astype(F32)


def _attn_a_kernel(bias_ref, q_ref, kp_ref, kc_ref, kn_ref, vp_ref, vc_ref, vn_ref, o_ref, lse_ref):
    classes, tq, _ = q_ref.shape
    nsub = tq // A_SUB
    i = pl.program_id(2)
    last_tile = pl.num_programs(2) - 1
    first_head = lax.broadcasted_iota(jnp.int32, (1, LANES), 1) < HEAD_DIM
    nt_dims = (((1,), (1,)), ((), ()))

    def window(p_ref, c_ref, n_ref, c, u, cols):
        lo, hi = u * A_SUB - A_RADIUS, (u + 1) * A_SUB + A_RADIUS
        parts = [p_ref[c, :, cols]] if lo < 0 else []
        parts.append(c_ref[c, max(lo, 0):min(hi, tq), cols])
        if hi > tq:
            parts.append(n_ref[c, :, cols])
        return parts[0] if len(parts) == 1 else jnp.concatenate(parts, axis=0)

    for c in range(classes):
        for u in range(nsub):
            if u == 0:
                bias = bias_ref[jnp.where(i == 0, 1, 0)]
            elif u == nsub - 1:
                bias = bias_ref[jnp.where(i == last_tile, 2, 0)]
            else:
                bias = bias_ref[0]
            rows = slice(u * A_SUB, (u + 1) * A_SUB)
            for pair in range(A_GROUP_WIDTH // LANES):
                cols = slice(pair * LANES, (pair + 1) * LANES)
                k = window(kp_ref, kc_ref, kn_ref, c, u, cols)
                v = window(vp_ref, vc_ref, vn_ref, c, u, cols)
                q = q_ref[c, rows, cols]
                outs, lses = [], []
                for hh in range(LANES // HEAD_DIM):
                    mine = first_head if hh == 0 else jnp.logical_not(first_head)
                    qm = jnp.where(mine, q, jnp.zeros_like(q))
                    sc = lax.dot_general(qm, k, nt_dims, preferred_element_type=F32) + bias
                    m = jnp.max(sc, axis=1, keepdims=True)
                    p = jnp.exp2(sc - m)
                    den = jnp.sum(p, axis=1, keepdims=True)
                    outs.append(jnp.dot(p.astype(BF16), v, preferred_element_type=F32) / den)
                    lses.append(m * LN2 + jnp.log(den))
                o_ref[c, rows, cols] = jnp.where(first_head, outs[0], outs[1])
                lse_ref[c, rows, cols] = jnp.where(first_head, lses[0], lses[1])


def _attn_a(q, k, v, gi):
    b, dil, cl, _ = q.shape
    tq = min(A_TQ, cl)
    cb = min(dil, A_TQ // tq)
    assert tq >= 2 * A_SUB and A_SUB == 2 * A_RADIUS and dil % cb == 0
    hb = tq // A_RADIUS
    n_halo = cl // A_RADIUS
    bias = _band_bias()
    main = pl.BlockSpec((None, cb, tq, A_GROUP_WIDTH), lambda bi, r, i: (bi, r, i, 0))
    prev = pl.BlockSpec((None, cb, A_RADIUS, A_GROUP_WIDTH),
                        lambda bi, r, i: (bi, r, jnp.maximum(i * hb - 1, 0), 0))
    nxt = pl.BlockSpec((None, cb, A_RADIUS, A_GROUP_WIDTH),
                       lambda bi, r, i: (bi, r, jnp.minimum((i + 1) * hb, n_halo - 1), 0))
    return pl.pallas_call(
        _attn_a_kernel,
        out_shape=[jax.ShapeDtypeStruct((b, dil, cl, A_GROUP_WIDTH), F32)] * 2,
        grid=(b, dil // cb, cl // tq),
        in_specs=[_full_spec(bias), main, prev, main, nxt, prev, main, nxt],
        out_specs=[main, main],
        compiler_params=_cparams(("parallel", "parallel", "parallel")),
        name=f"attn_a{gi}",
    )(bias, q, k, k, k, v, v, v)


def _flash_b_kernel(qt_ref, k_ref, vt_ref, o_ref, acc_sc, l_sc, m_sc, p0_sc, p1_sc, *, online):
    nk, tk, _ = k_ref.shape
    width = qt_ref.shape[1]
    tq = width // B_GROUP_HEADS
    acc_sc[...] = jnp.zeros(acc_sc.shape, F32)
    l_sc[...] = jnp.zeros(l_sc.shape, F32)

    def scores(j):
        return jnp.dot(k_ref[j], qt_ref[...], preferred_element_type=F32)

    def column_sums(p):
        return p.reshape(tk // SUBLANES, SUBLANES, width).sum(axis=0)

    def accumulate(j, pb):
        vt = vt_ref[j]
        for h in range(B_GROUP_HEADS):
            cols = slice(h * tq, (h + 1) * tq)
            acc_sc[:, cols] += jnp.dot(vt, pb[:, cols], preferred_element_type=F32)

    if online:
        m_sc[...] = jnp.full(m_sc.shape, NEG_INF, F32)

        def body(j, carry):
            sc = scores(j)
            m_old = m_sc[...]
            m_new = jnp.maximum(m_old, jnp.max(sc, axis=0, keepdims=True))
            alpha = jnp.exp2(m_old - m_new)
            m_sc[...] = m_new
            p = jnp.exp2(sc - m_new)
            l_sc[...] = alpha * l_sc[...] + column_sums(p)
            acc_sc[...] = alpha * acc_sc[...]
            accumulate(j, p.astype(BF16))
            return carry

        lax.fori_loop(0, nk, body, 0)
    else:
        def exponentials(sc, p_out):
            p = jnp.exp2(sc)
            l_sc[...] += column_sums(p)
            p_out[...] = p.astype(BF16)

        def stage(j, p_in, p_out):
            sc = scores(j)
            accumulate(j - 1, p_in[...])
            exponentials(sc, p_out)

        exponentials(scores(0), p0_sc)

        def body(jj, carry):
            stage(2 * jj + 1, p0_sc, p1_sc)
            stage(2 * jj + 2, p1_sc, p0_sc)
            return carry

        lax.fori_loop(0, nk // 2 - 1, body, 0)
        stage(nk - 1, p0_sc, p1_sc)
        accumulate(nk - 1, p1_sc[...])
    o = acc_sc[...] / jnp.sum(l_sc[...], axis=0, keepdims=True)
    o = jnp.concatenate([o[:, h * tq:(h + 1) * tq] for h in range(B_GROUP_HEADS)], axis=0)
    o_ref[...] = o.T.astype(o_ref.dtype)


def _flash_b(qt, kb, vt, online):
    b, _, nq, _, width = qt.shape
    tq = width // B_GROUP_HEADS
    _, nk, tk, _ = kb.shape
    return pl.pallas_call(
        functools.partial(_flash_b_kernel, online=online),
        out_shape=jax.ShapeDtypeStruct((b, B_KV_HEADS, nq * tq, B_GROUP_WIDTH), BF16),
        grid=(b, B_KV_HEADS, nq),
        in_specs=[pl.BlockSpec((None, None, None, B_KV_WIDTH, width), lambda bi, g, i: (bi, g, i, 0, 0)),
                  pl.BlockSpec((None, nk, tk, B_KV_WIDTH), lambda bi, g, i: (bi, 0, 0, 0)),
                  pl.BlockSpec((None, None, nk, HEAD_DIM, tk), lambda bi, g, i: (bi, g, 0, 0, 0))],
        out_specs=pl.BlockSpec((None, None, tq, B_GROUP_WIDTH), lambda bi, g, i: (bi, g, i, 0)),
        scratch_shapes=[pltpu.VMEM((HEAD_DIM, width), F32), pltpu.VMEM((SUBLANES, width), F32),
                        pltpu.VMEM((1, width), F32),
                        pltpu.VMEM((tk, width), BF16), pltpu.VMEM((tk, width), BF16)],
        compiler_params=_cparams(("parallel", "parallel", "parallel")),
        name="flash_b_online" if online else "flash_b",
    )(qt, kb, vt)


def _stage_classes(blk_ref, stage_ref):
    dil, rows, width = blk_ref.shape
    if dil == 1:
        return
    for c in range(width // LANES):
        for r in range(dil):
            stage_ref.at[c][pl.ds(r, rows, stride=dil), :] = blk_ref[r, :, c * LANES:(c + 1) * LANES]


def _token_rows(blk_ref, stage_ref):
    if blk_ref.shape[0] == 1:
        return blk_ref[0]
    return jnp.concatenate([stage_ref[c] for c in range(stage_ref.shape[0])], axis=1)


def _post_attn_kernel(x_ref, o0_ref, o1_ref, o2_ref, l0_ref, l1_ref, l2_ref, ob0_ref, ob1_ref, gates_ref,
                      wa_ref, wb0_ref, wb1_ref, wout_ref, gffn_ref, wrh_ref, wrl_ref,
                      x1_ref, h2_ref, aff_ref, so1, so2, sl1, sl2):
    for blk, stage in ((l1_ref, sl1), (l2_ref, sl2), (o1_ref, so1), (o2_ref, so2)):
        _stage_classes(blk, stage)
    nt = (((1,), (1,)), ((), ()))
    l0 = _token_rows(l0_ref, None)
    l1 = _token_rows(l1_ref, sl1)
    l2 = _token_rows(l2_ref, sl2)
    m = jnp.maximum(jnp.maximum(l0, l1), l2)
    e0, e1, e2 = jnp.exp(l0 - m), jnp.exp(l1 - m), jnp.exp(l2 - m)
    oa = (e0 * _token_rows(o0_ref, None) + e1 * _token_rows(o1_ref, so1)
          + e2 * _token_rows(o2_ref, so2)) / (e0 + e1 + e2)
    ya = jnp.dot(oa.astype(BF16), wa_ref[...], preferred_element_type=F32)
    yb = (jnp.dot(ob0_ref[...], wb0_ref[...], preferred_element_type=F32)
          + jnp.dot(ob1_ref[...], wb1_ref[...], preferred_element_type=F32))
    ga = gates_ref[:, :D_MODEL].astype(F32)
    gb = gates_ref[:, D_MODEL:].astype(F32)
    z = (ga * ya + gb * yb).astype(BF16)
    x1 = x_ref[...] + jnp.dot(z, wout_ref[...], preferred_element_type=F32)
    x1_ref[...] = x1
    ms = jnp.mean(x1 * x1, axis=-1, keepdims=True)
    h2 = x1 * lax.rsqrt(ms + EPS) * gffn_ref[...]
    hi = h2.astype(BF16)
    lo = (h2 - hi.astype(F32)).astype(BF16)
    h2_ref[...] = hi
    both = lax.dot_general(wrl_ref[...], hi, nt, preferred_element_type=F32)
    logits = (both[:N_EXPERTS] + both[N_EXPERTS:]
              + lax.dot_general(wrh_ref[...], lo, nt, preferred_element_type=F32))
    logits = logits - jnp.max(logits, axis=0, keepdims=True)
    e = jnp.exp(logits)
    aff = e / jnp.sum(e, axis=0, keepdims=True)
    for j in range(aff_ref.shape[0]):
        aff_ref[j] = aff[:, j * LANES:(j + 1) * LANES]


def _post_attn(xf, oas, lses, ob, gates, b, s, w_a, w_b, w_out, g_ffn, w_router):
    n = xf.shape[0]
    tm = ROW_TILE
    nts = s // tm
    wa = w_a.astype(BF16)
    wb = w_b.astype(BF16)
    wb0, wb1 = wb[:B_GROUP_WIDTH], wb[B_GROUP_WIDTH:]
    wout = w_out.astype(BF16)
    wrt = w_router.T
    wrh = wrt.astype(BF16)
    wrl = jnp.concatenate([wrh, (wrt - wrh.astype(F32)).astype(BF16)], axis=0)
    consts = [wa, wb0, wb1, wout, g_ffn.reshape(1, D_MODEL), wrh, wrl]

    def rows(w):
        return pl.BlockSpec((tm, w), lambda i: (i, 0))

    def class_spec(dil):
        return pl.BlockSpec((None, dil, tm // dil, A_GROUP_WIDTH), lambda i: (i // nts, 0, i % nts, 0))

    def ob_spec(g):
        return pl.BlockSpec((None, None, tm, B_GROUP_WIDTH), lambda i: (i // nts, g, i % nts, 0))

    a_specs = [class_spec(d) for d in A_DILATIONS]
    return pl.pallas_call(
        _post_attn_kernel,
        out_shape=[jax.ShapeDtypeStruct((n, D_MODEL), F32),
                   jax.ShapeDtypeStruct((n, D_MODEL), BF16),
                   jax.ShapeDtypeStruct((n // LANES, N_EXPERTS, LANES), F32)],
        grid=(n // tm,),
        in_specs=[rows(D_MODEL)] + a_specs + a_specs + [ob_spec(0), ob_spec(1), rows(GATE_WIDTH)]
                 + [_full_spec(a) for a in consts],
        out_specs=[rows(D_MODEL), rows(D_MODEL),
                   pl.BlockSpec((tm // LANES, N_EXPERTS, LANES), lambda i: (i, 0, 0))],
        scratch_shapes=[pltpu.VMEM((A_GROUP_WIDTH // LANES, tm, LANES), F32)] * 4,
        compiler_params=_cparams(("parallel",)),
        name="post_attn",
    )(xf, *oas, *lses, ob, ob, gates, *consts)


def _route_kernel(aff_ref, tri_ref, posl_ref, starts_ref, sel_sc, *, cap, idx_bits, tiles_per_block):
    nlt = aff_ref.shape[0]
    capf = float(cap)

    def count(mask):
        part = jnp.sum(jnp.where(mask, 1.0, 0.0), axis=0)
        return jnp.sum(part, axis=1, keepdims=True)[None]

    def value_body(i, tau):
        bits = pltpu.bitcast(aff_ref[...], jnp.int32)
        cand = tau | lax.shift_left(jnp.int32(1), 30 - i)
        return jnp.where(count(bits >= cand) >= capf, cand, tau)

    tau = lax.fori_loop(0, 31, value_body, jnp.zeros((1, N_EXPERTS, 1), jnp.int32))
    bits = pltpu.bitcast(aff_ref[...], jnp.int32)
    need = capf - count(bits > tau)
    tok = (lax.broadcasted_iota(jnp.int32, bits.shape, 0) * LANES
           + lax.broadcasted_iota(jnp.int32, bits.shape, 2))

    def index_body(i, last):
        b = pltpu.bitcast(aff_ref[...], jnp.int32)
        cand = last | lax.shift_left(jnp.int32(1), idx_bits - 1 - i)
        return jnp.where(count((b == tau) & (tok < cand)) < need, cand, last)

    last = lax.fori_loop(0, idx_bits, index_body, jnp.zeros((1, N_EXPERTS, 1), jnp.int32))
    sel_sc[...] = jnp.where((bits > tau) | ((bits == tau) & (tok <= last)), 1.0, 0.0)

    sel = sel_sc[...]
    sb = sel.reshape(nlt * N_EXPERTS, LANES).astype(BF16)
    tri = tri_ref[...]
    within = jnp.dot(sb, tri, preferred_element_type=F32).reshape(nlt, N_EXPERTS, LANES)
    total = jnp.dot(sb, jnp.ones((LANES, LANES), BF16),
                    preferred_element_type=F32).reshape(nlt, N_EXPERTS, LANES)
    upto = total
    step = 1
    while step < nlt:
        upto = upto + jnp.concatenate([jnp.zeros((step, N_EXPERTS, LANES), F32), upto[:nlt - step]], axis=0)
        step *= 2
    before = upto - total
    posl_ref[...] = jnp.where(sel > 0, before + within - sel, -1.0).astype(jnp.int32)
    starts_ref[...] = before.reshape(nlt // tiles_per_block, tiles_per_block, N_EXPERTS, LANES)[:, 0].astype(jnp.int32)


def _route(aff3, cap):
    nlt = aff3.shape[0]
    n = nlt * LANES
    tpb = MOE_T // LANES
    tri = (jnp.arange(LANES)[:, None] <= jnp.arange(LANES)[None, :]).astype(BF16)
    vm = pl.BlockSpec(memory_space=pltpu.VMEM)
    return pl.pallas_call(
        functools.partial(_route_kernel, cap=cap, idx_bits=(n - 1).bit_length(), tiles_per_block=tpb),
        out_shape=[jax.ShapeDtypeStruct((nlt, N_EXPERTS, LANES), jnp.int32),
                   jax.ShapeDtypeStruct((n // MOE_T, N_EXPERTS, LANES), jnp.int32)],
        in_specs=[vm, vm],
        out_specs=[vm, vm],
        scratch_shapes=[pltpu.VMEM((nlt, N_EXPERTS, LANES), F32)],
        compiler_params=pltpu.CompilerParams(vmem_limit_bytes=VMEM_LIMIT),
        name="route",
    )(aff3, tri)


def _block_slots(s_ref, i, nt):
    return [(s_ref[e * (nt + 1) + i], s_ref[e * (nt + 1) + i + 1]) for e in range(N_EXPERTS)]


def _align_down(x):
    return (x >> MOE_ALIGN_SHIFT) << MOE_ALIGN_SHIFT


def _moe_gather_kernel(s_ref, aff_ref, posl_ref, h2_ref, xs_hbm, tm_ref,
                       haug_sc, pall_sc, stage_sc, carry_sc, pending_sc, sem, *, nt, cap):
    i = pl.program_id(0)
    nj = posl_ref.shape[0]
    ws = MOE_WS

    @pl.when(i == 0)
    def _():
        carry_sc[...] = jnp.zeros(carry_sc.shape, BF16)
        pending_sc[0] = 0
        pending_sc[1] = 0
        pad = xs_hbm.shape[1] - cap
        stage_sc[0] = jnp.zeros(stage_sc.shape[1:], BF16)
        fills = [pltpu.make_async_copy(stage_sc.at[0, pl.ds(0, pad)], xs_hbm.at[e, pl.ds(cap, pad)], sem)
                 for e in range(N_EXPERTS)]
        for cp in fills:
            cp.start()
        for cp in fills:
            cp.wait()

    for j in range(nj):
        pj = posl_ref[j]
        gj = jnp.where(pj >= 0, aff_ref[j], 0.0)
        blk = jnp.concatenate([gj, pj.astype(F32), gj, jnp.zeros((LANES - 3 * N_EXPERTS, LANES), F32)], axis=0)
        tm_ref[j * LANES:(j + 1) * LANES, :] = blk.T
    tmv = tm_ref[...]
    lane = lax.broadcasted_iota(jnp.int32, tmv.shape, 1)
    ghi = tmv.astype(BF16)
    glo = (tmv - ghi.astype(F32)).astype(BF16)
    zero = jnp.zeros_like(ghi)
    haug_sc[:, :D_MODEL] = h2_ref[...]
    haug_sc[:, D_MODEL:] = jnp.where(lane < N_EXPERTS, ghi,
                                     jnp.where((lane >= TM_GATE_LO) & (lane < TM_GATE_LO + N_EXPERTS), glo, zero))

    astarts, kks, offcs = [], [], []
    nw = jnp.int32(0)
    for st, en in _block_slots(s_ref, i, nt):
        ast = _align_down(st)
        span = _align_down(en) - ast
        kk = span >> MOE_WS_SHIFT
        astarts.append(ast)
        kks.append(kk)
        offcs.append(span - kk * ws)
        nw = jnp.maximum(nw, kk + 1)

    def wait_window():
        for _ in range(N_EXPERTS):
            pltpu.make_async_copy(stage_sc.at[0, pl.ds(0, ws)], xs_hbm.at[0, pl.ds(0, ws)], sem).wait()

    row = lax.broadcasted_iota(jnp.int32, (ws, LANES), 0)

    def window(k, carry):
        for e in range(N_EXPERTS):
            base = astarts[e] + k * ws
            for j in range(nj):
                hit = (posl_ref[j, e:e + 1, :] - base) == row
                pall_sc[e * ws:(e + 1) * ws, j * LANES:(j + 1) * LANES] = jnp.where(hit, 1.0, 0.0).astype(BF16)
        stage = stage_sc.at[pending_sc[1]]
        stage[...] = jnp.dot(pall_sc[...], haug_sc[...], preferred_element_type=F32).astype(BF16)

        @pl.when(k == 0)
        def _():
            for e in range(N_EXPERTS):
                stage[e * ws:e * ws + MOE_ALIGN, :] += carry_sc[e]

        for e in range(N_EXPERTS):
            @pl.when(k == kks[e])
            def _():
                carry_sc[e] = stage[pl.ds(pl.multiple_of(e * ws + offcs[e], MOE_ALIGN), MOE_ALIGN), :]

        @pl.when(pending_sc[0] == 1)
        def _():
            wait_window()

        for e in range(N_EXPERTS):
            base = pl.multiple_of(astarts[e] + k * ws, MOE_ALIGN)
            pltpu.make_async_copy(stage.at[pl.ds(e * ws, ws)], xs_hbm.at[e, pl.ds(base, ws)], sem).start()
        pending_sc[0] = 1
        pending_sc[1] = 1 - pending_sc[1]
        return carry

    lax.fori_loop(0, nw, window, 0)

    @pl.when((i == nt - 1) & (pending_sc[0] == 1))
    def _():
        wait_window()
        pending_sc[0] = 0


def _moe_gather(s_flat, aff3, posl, h2, cap):
    n = h2.shape[0]
    t = MOE_T
    nt = n // t
    cap_pad = cap + (t // MOE_WS + 1) * MOE_WS
    blk3 = pl.BlockSpec((t // LANES, N_EXPERTS, LANES), lambda i, s: (i, 0, 0))
    grid_spec = pltpu.PrefetchScalarGridSpec(
        num_scalar_prefetch=1,
        grid=(nt,),
        in_specs=[blk3, blk3, pl.BlockSpec((t, D_MODEL), lambda i, s: (i, 0))],
        out_specs=[pl.BlockSpec(memory_space=pl.ANY), pl.BlockSpec((t, LANES), lambda i, s: (i, 0))],
        scratch_shapes=[pltpu.VMEM((t, XS_WIDTH), BF16),
                        pltpu.VMEM((N_EXPERTS * MOE_WS, t), BF16),
                        pltpu.VMEM((2, N_EXPERTS * MOE_WS, XS_WIDTH), BF16),
                        pltpu.VMEM((N_EXPERTS, MOE_ALIGN, XS_WIDTH), BF16),
                        pltpu.SMEM((2,), jnp.int32),
                        pltpu.SemaphoreType.DMA(())])
    return pl.pallas_call(
        functools.partial(_moe_gather_kernel, nt=nt, cap=cap),
        out_shape=[jax.ShapeDtypeStruct((N_EXPERTS, cap_pad, XS_WIDTH), BF16),
                   jax.ShapeDtypeStruct((n, LANES), F32)],
        grid_spec=grid_spec,
        compiler_params=_cparams(("arbitrary",)),
        name="moe_gather",
    )(s_flat, aff3, posl, h2)


def _moe_ffn_kernel(xs_ref, w1_ref, w3_ref, w2_ref, ye_ref):
    xe = xs_ref[:, :D_MODEL]
    route = xs_ref[:, D_MODEL:].astype(F32)
    lane = lax.broadcasted_iota(jnp.int32, route.shape, 1)
    e = pl.program_id(0)
    gate = jnp.sum(jnp.where((lane == e) | (lane == e + TM_GATE_LO), route, 0.0), axis=1, keepdims=True)
    a = jnp.dot(xe, w1_ref[...], preferred_element_type=F32)
    u = jnp.dot(xe, w3_ref[...], preferred_element_type=F32)
    hid = (a * jax.nn.sigmoid(a) * u).astype(BF16)
    ye = jnp.dot(hid, w2_ref[...], preferred_element_type=F32) * gate
    ye_ref[...] = ye.astype(ye_ref.dtype)


def _moe_ffn(xs, cap, w1, w3, w2):
    e = xs.shape[0]
    tc = min(FFN_TC, cap)
    wspec = pl.BlockSpec((None, D_MODEL, D_MODEL), lambda ei, ci: (ei, 0, 0))
    return pl.pallas_call(
        _moe_ffn_kernel,
        out_shape=jax.ShapeDtypeStruct((e, cap, D_MODEL), BF16),
        grid=(e, cap // tc),
        in_specs=[pl.BlockSpec((None, tc, XS_WIDTH), lambda ei, ci: (ei, ci, 0)), wspec, wspec, wspec],
        out_specs=pl.BlockSpec((None, tc, D_MODEL), lambda ei, ci: (ei, ci, 0)),
        compiler_params=_cparams(("parallel", "arbitrary")),
        name="moe_ffn",
    )(xs, w1, w3, w2)


def _final_kernel(s_ref, x1_ref, tm_ref, p_ref, ye_hbm, wple_ref, wpg_ref, gple_ref, gfin_ref, expand_ref, out_ref,
                  buf_sc, pall_sc, y_sc, sem, *, nt, cap):
    i = pl.program_id(0)
    ws = MOE_WS
    t = x1_ref.shape[0]
    slot = i % 2

    def window_rows(blk, k):
        out = []
        for st, _ in _block_slots(s_ref, blk, nt):
            lo = _align_down(st) + k * ws
            out.append((lo, pl.multiple_of(jnp.minimum(lo, cap - ws), MOE_ALIGN)))
        return out

    def copies(rows, buf, which):
        return [pltpu.make_async_copy(ye_hbm.at[e, pl.ds(base, ws)], buf_sc.at[buf, pl.ds(e * ws, ws)], sem.at[which])
                for e, (_, base) in enumerate(rows)]

    @pl.when(i == 0)
    def _():
        for cp in copies(window_rows(0, 0), 0, 0):
            cp.start()

    @pl.when(i + 1 < nt)
    def _():
        for cp in copies(window_rows(i + 1, 0), 1 - slot, 1 - slot):
            cp.start()

    nw = jnp.int32(0)
    for st, en in _block_slots(s_ref, i, nt):
        nw = jnp.maximum(nw, (en - _align_down(st) + ws - 1) >> MOE_WS_SHIFT)
    lane = lax.broadcasted_iota(jnp.int32, (1, LANES), 1)
    col = (lax.broadcasted_iota(jnp.int32, (1, N_EXPERTS * ws), 1) & (ws - 1)).astype(F32)

    def onehot(rows):
        basev = jnp.zeros((1, LANES), F32)
        lov = jnp.full((1, LANES), float(2 ** 30), F32)
        for e, (lo, base) in enumerate(rows):
            basev = jnp.where(lane == TM_POS + e, base.astype(F32), basev)
            lov = jnp.where(lane == TM_POS + e, lo.astype(F32), lov)
        tmv = tm_ref[...]
        off = tmv - basev
        off = jnp.where((tmv >= lov) & (off >= 0.0) & (off < float(ws)), off, -1.0)
        spread = jnp.dot(off.astype(BF16), expand_ref[...], preferred_element_type=F32)
        pall_sc[...] = jnp.where(spread == col, 1.0, 0.0).astype(BF16)

    rows0 = window_rows(i, 0)
    onehot(rows0)
    for cp in copies(rows0, slot, slot):
        cp.wait()
    y_sc[...] = jnp.dot(pall_sc[...], buf_sc[slot], preferred_element_type=F32)

    def window(k, carry):
        rows = window_rows(i, k)
        cps = copies(rows, slot, slot)
        for cp in cps:
            cp.start()
        onehot(rows)
        for cp in cps:
            cp.wait()
        y_sc[...] += jnp.dot(pall_sc[...], buf_sc[slot], preferred_element_type=F32)
        return carry

    lax.fori_loop(1, nw, window, 0)

    x2 = x1_ref[...] + y_sc[...]
    ms = jnp.mean(x2 * x2, axis=-1, keepdims=True)
    hp = (x2 * lax.rsqrt(ms + EPS) * gple_ref[...]).astype(BF16)
    gt = jax.nn.sigmoid(jnp.dot(hp, wpg_ref[...], preferred_element_type=F32))
    emb = jnp.dot(p_ref[...].astype(BF16), wple_ref[...], preferred_element_type=F32)
    x3 = x2 + emb * gt
    ms3 = jnp.mean(x3 * x3, axis=-1, keepdims=True)
    out_ref[...] = x3 * lax.rsqrt(ms3 + EPS) * gfin_ref[...]


def _final(s_flat, x1, tm, pf, ye, w_ple, w_ple_gate, g_ple, g_final):
    n = x1.shape[0]
    t = MOE_T
    nt = n // t
    cap = ye.shape[1]
    src = jnp.arange(LANES)[:, None] - TM_POS
    dst = jnp.arange(N_EXPERTS * MOE_WS)[None, :] >> MOE_WS_SHIFT
    expand = (src == dst).astype(BF16)
    consts = [w_ple.astype(BF16), w_ple_gate.astype(BF16), g_ple.reshape(1, D_MODEL), g_final.reshape(1, D_MODEL),
              expand]

    def rows(w):
        return pl.BlockSpec((t, w), lambda i, s: (i, 0))

    grid_spec = pltpu.PrefetchScalarGridSpec(
        num_scalar_prefetch=1,
        grid=(nt,),
        in_specs=[rows(D_MODEL), rows(LANES), rows(PLE_DIM), pl.BlockSpec(memory_space=pl.ANY)]
                 + [_full_spec(a) for a in consts],
        out_specs=rows(D_MODEL),
        scratch_shapes=[pltpu.VMEM((2, N_EXPERTS * MOE_WS, D_MODEL), BF16),
                        pltpu.VMEM((t, N_EXPERTS * MOE_WS), BF16),
                        pltpu.VMEM((t, D_MODEL), F32),
                        pltpu.SemaphoreType.DMA((2,))])
    return pl.pallas_call(
        functools.partial(_final_kernel, nt=nt, cap=cap),
        out_shape=jax.ShapeDtypeStruct((n, D_MODEL), F32),
        grid_spec=grid_spec,
        compiler_params=_cparams(("arbitrary",)),
        name="final",
    )(s_flat, x1, tm, pf, ye, *consts)


def _run_group(x, p, g_mix, w_in, g_qn, g_kn, w_a_proj, w_b_proj, w_out, g_ffn, w_router,
               w1, w3, w2, g_ple, w_ple, w_ple_gate, g_final):
    b, s, _ = x.shape
    n = b * s
    assert s % B_TK == 0 and s % (A_DILATIONS[-1] * A_RADIUS) == 0
    xf = x.reshape(n, D_MODEL)
    tab = _rope_tables(s)
    outs = _in_proj(xf, tab, b, s, g_mix, w_in, g_qn, g_kn)
    qas, kas, vas = outs[0:3], outs[3:6], outs[6:9]
    qt, kb, vt, gates = outs[9:]

    oas, lses = [], []
    for gi, (window, dil) in enumerate(A_PATTERNS):
        assert window // (2 * dil) == A_RADIUS
        o, lse = _attn_a(qas[gi], kas[gi], vas[gi], gi)
        oas.append(o)
        lses.append(lse)

    score_bound = (HEAD_DIM * Q_SCALE * 1.02) * jnp.max(jnp.abs(g_qn)) * jnp.max(jnp.abs(g_kn))
    kb4 = kb.reshape(b, s // B_TK, B_TK, B_KV_WIDTH)
    ob = lax.cond(score_bound <= B_UNSHIFTED_SCORE_MAX,
                  lambda: _flash_b(qt, kb4, vt, online=False),
                  lambda: _flash_b(qt, kb4, vt, online=True))

    x1, h2, aff3 = _post_attn(xf, oas, lses, ob, gates, b, s, w_a_proj, w_b_proj, w_out, g_ffn, w_router)

    cap = CAPACITY_FACTOR * n // N_EXPERTS
    assert cap % MOE_WS == 0 and cap >= MOE_WS
    posl, starts = _route(aff3, cap)
    s_flat = jnp.concatenate([starts[:, :, 0].T, jnp.full((N_EXPERTS, 1), cap, jnp.int32)], axis=1).reshape(-1)
    xs, tm = _moe_gather(s_flat, aff3, posl, h2, cap)
    ye = _moe_ffn(xs, cap, w1, w3, w2)
    out = _final(s_flat, x1, tm, p.reshape(n, PLE_DIM), ye, w_ple, w_ple_gate, g_ple, g_final)
    return out.reshape(b, s, D_MODEL)


def kernel(x_prompt, x_sample, p_prompt, p_sample, g_mix, w_in, g_qn, g_kn, w_a_proj, w_b_proj, w_out, g_ffn,
           w_router, w_exp_gate, w_exp_up, w_exp_down, g_ple, w_ple, w_ple_gate, g_final):
    assert g_mix.shape[0] == 1, "single layer"
    w1 = w_exp_gate[0].astype(BF16)
    w3 = w_exp_up[0].astype(BF16)
    w2 = w_exp_down[0].astype(BF16)
    args = (g_mix[0], w_in[0], g_qn[0], g_kn[0], w_a_proj[0], w_b_proj[0], w_out[0], g_ffn[0], w_router[0],
            w1, w3, w2, g_ple[0], w_ple[0], w_ple_gate[0], g_final)
    y_prompt = _run_group(x_prompt, p_prompt[0], *args)
    y_sample = _run_group(x_sample, p_sample[0], *args)
    return (y_prompt, y_sample)
```

```python
import functools
import math

import jax
import jax.numpy as jnp
from jax import lax
from jax.experimental import pallas as pl
from jax.experimental.pallas import tpu as pltpu

F32 = jnp.float32
BF16 = jnp.bfloat16

D_MODEL = 1024
HEAD_DIM = 64
A_PATTERNS = ((128, 1), (512, 4), (2048, 16))
A_DILATIONS = tuple(d for _, d in A_PATTERNS)
A_GROUPS = len(A_PATTERNS)
A_HEADS_PER_GROUP = 4
A_WIDTH = 768
A_GROUP_WIDTH = A_HEADS_PER_GROUP * HEAD_DIM
A_ROT_DIM = 16
A_RADIUS = 64
ROPE_THETA = 500000.0
B_Q_WIDTH = 512
B_KV_HEADS = 2
B_KV_WIDTH = 128
B_GROUP_HEADS = 4
B_GROUP_WIDTH = B_GROUP_HEADS * HEAD_DIM
AXIAL_THETA = 10000.0
GRID_W = 64
GATE_WIDTH = 2048
N_EXPERTS = 16
CAPACITY_FACTOR = 2
PLE_DIM = 256
EPS = 1e-6
NEG_INF = -1e30
LN2 = math.log(2.0)
Q_SCALE = (HEAD_DIM ** -0.5) / LN2
B_UNSHIFTED_SCORE_MAX = 50.0

LANES = 128
SUBLANES = 8
VMEM_LIMIT = 48 * 1024 * 1024

ROW_TILE = 512
A_TQ = 1024
A_SUB = 128
B_TQ = ROW_TILE
B_TK = 4 * ROW_TILE
FFN_TC = 1024
MOE_T = ROW_TILE
MOE_WS_SHIFT = 7
MOE_WS = 1 << MOE_WS_SHIFT
MOE_ALIGN_SHIFT = 4
MOE_ALIGN = 1 << MOE_ALIGN_SHIFT
XS_WIDTH = D_MODEL + LANES
TM_POS = N_EXPERTS
TM_GATE_LO = 2 * N_EXPERTS


def _cparams(sem):
    return pltpu.CompilerParams(dimension_semantics=sem, vmem_limit_bytes=VMEM_LIMIT)


def _full_spec(a):
    return pl.BlockSpec(a.shape, lambda *_: (0,) * a.ndim)


def _rope_tables(s):
    pos = jnp.arange(s)
    posf = pos.astype(F32)
    inv_a = jnp.power(ROPE_THETA, -jnp.arange(0, A_ROT_DIM, 2, dtype=F32) / A_ROT_DIM)
    ang = posf[:, None] * inv_a[None, :]
    ca, sa = jnp.cos(ang), jnp.sin(ang)
    z8 = jnp.zeros_like(sa)
    rest = HEAD_DIM - A_ROT_DIM
    cos_a = jnp.concatenate([ca, ca, jnp.ones((s, rest), F32)], axis=1)
    sp_a = jnp.concatenate([z8, sa, jnp.zeros((s, rest), F32)], axis=1)
    sm_a = jnp.concatenate([-sa, z8, jnp.zeros((s, rest), F32)], axis=1)

    hb = HEAD_DIM // 2
    inv_b = jnp.power(AXIAL_THETA, -jnp.arange(0, hb, 2, dtype=F32) / hb)
    row = (pos // GRID_W).astype(F32)
    col = (pos % GRID_W).astype(F32)
    ar = row[:, None] * inv_b[None, :]
    ac = col[:, None] * inv_b[None, :]
    cr, sr, cc, sc = jnp.cos(ar), jnp.sin(ar), jnp.cos(ac), jnp.sin(ac)
    z16 = jnp.zeros_like(sr)
    cos_b = jnp.concatenate([cr, cr, cc, cc], axis=1)
    sp_b = jnp.concatenate([z16, sr, z16, sc], axis=1)
    sm_b = jnp.concatenate([-sr, z16, -sc, z16], axis=1)
    heads_per_tile = LANES // HEAD_DIM
    parts = [jnp.tile(t, (1, heads_per_tile)) for t in (cos_a, sp_a, sm_a, cos_b, sp_b, sm_b)]
    return jnp.concatenate(parts, axis=1)


def _rope_chunk(x, tab_ref, base, half):
    cos = tab_ref[:, base:base + LANES]
    sp = tab_ref[:, base + LANES:base + 2 * LANES]
    sm = tab_ref[:, base + 2 * LANES:base + 3 * LANES]
    return x * cos + pltpu.roll(x, half, 1) * sp + pltpu.roll(x, LANES - half, 1) * sm


def _head_mean_sq(acc, blk_ref):
    sq = acc * acc
    hi = sq.astype(BF16)
    lo = (sq - hi.astype(F32)).astype(BF16)
    blk = blk_ref[...]
    return (jnp.dot(hi, blk, preferred_element_type=F32)
            + jnp.dot(lo, blk, preferred_element_type=F32))


def _store_by_class(out_refs, chunk_idx, chunk, stage_ref):
    g, half = divmod(chunk_idx, A_GROUP_WIDTH // LANES)
    dil = A_DILATIONS[g]
    lanes = slice(half * LANES, (half + 1) * LANES)
    if dil == 1:
        out_refs[g][0, :, lanes] = chunk.astype(BF16)
        return
    tm = chunk.shape[0]
    stage_ref[...] = chunk
    for r in range(dil):
        out_refs[g][r, :, lanes] = stage_ref[pl.ds(r, tm // dil, stride=dil), :].astype(BF16)


def _in_proj_kernel(x_ref, tab_ref, gmix_ref, wqa_ref, wka_ref, wva_ref, wqb_ref, wkvb_ref,
                    wg_ref, gq_ref, gk_ref, blkq_ref, blkk_ref,
                    qa0_ref, qa1_ref, qa2_ref, ka0_ref, ka1_ref, ka2_ref, va0_ref, va1_ref, va2_ref,
                    qt_ref, kb_ref, vt_ref, gates_ref, stage_ref):
    x = x_ref[...]
    tm = x.shape[0]
    ms = jnp.mean(x * x, axis=-1, keepdims=True)
    h = (x * lax.rsqrt(ms + EPS) * gmix_ref[...]).astype(BF16)

    acc = jnp.dot(h, wqa_ref[...], preferred_element_type=F32)
    for c in range(A_WIDTH // LANES):
        roped = _rope_chunk(acc[:, c * LANES:(c + 1) * LANES], tab_ref, 0, A_ROT_DIM // 2) * Q_SCALE
        _store_by_class((qa0_ref, qa1_ref, qa2_ref), c, roped, stage_ref)

    acc = jnp.dot(h, wka_ref[...], preferred_element_type=F32)
    for c in range(A_WIDTH // LANES):
        roped = _rope_chunk(acc[:, c * LANES:(c + 1) * LANES], tab_ref, 0, A_ROT_DIM // 2)
        _store_by_class((ka0_ref, ka1_ref, ka2_ref), c, roped, stage_ref)

    acc = jnp.dot(h, wva_ref[...], preferred_element_type=F32)
    for c in range(A_WIDTH // LANES):
        _store_by_class((va0_ref, va1_ref, va2_ref), c, acc[:, c * LANES:(c + 1) * LANES], stage_ref)

    acc = jnp.dot(h, wqb_ref[...], preferred_element_type=F32)
    acc = acc * lax.rsqrt(_head_mean_sq(acc, blkq_ref) + EPS) * gq_ref[...]
    heads_per_chunk = LANES // HEAD_DIM
    for c in range(B_Q_WIDTH // LANES):
        roped = _rope_chunk(acc[:, c * LANES:(c + 1) * LANES], tab_ref, 3 * LANES, HEAD_DIM // 4) * Q_SCALE
        rt = roped.T.astype(BF16)
        for hh in range(heads_per_chunk):
            head = c * heads_per_chunk + hh
            g, hg = divmod(head, B_GROUP_HEADS)
            cols = slice(hg * tm, (hg + 1) * tm)
            qt_ref[g, g * HEAD_DIM:(g + 1) * HEAD_DIM, cols] = rt[hh * HEAD_DIM:(hh + 1) * HEAD_DIM, :]
            qt_ref[g, (1 - g) * HEAD_DIM:(2 - g) * HEAD_DIM, cols] = jnp.zeros((HEAD_DIM, tm), BF16)

    kv = jnp.dot(h, wkvb_ref[...], preferred_element_type=F32)
    acc = kv[:, :B_KV_WIDTH]
    acc = acc * lax.rsqrt(_head_mean_sq(acc, blkk_ref) + EPS) * gk_ref[...]
    kb_ref[...] = _rope_chunk(acc, tab_ref, 3 * LANES, HEAD_DIM // 4).astype(BF16)

    vt = kv[:, B_KV_WIDTH:].T.astype(BF16)
    for g in range(B_KV_HEADS):
        vt_ref[g] = vt[g * HEAD_DIM:(g + 1) * HEAD_DIM, :]

    gates_ref[...] = jax.nn.sigmoid(jnp.dot(h, wg_ref[...], preferred_element_type=F32)).astype(BF16)


def _in_proj(xf, tab, b, s, gmix, w_in, gq, gk):
    n = xf.shape[0]
    tm = ROW_TILE
    nts = s // tm
    c1, c2, c3 = A_WIDTH, 2 * A_WIDTH, 3 * A_WIDTH
    c4 = c3 + B_Q_WIDTH
    c5 = c4 + B_KV_WIDTH
    c6 = c5 + B_KV_WIDTH
    wb = w_in.astype(BF16)
    ws = [wb[:, :c1], wb[:, c1:c2], wb[:, c2:c3], wb[:, c3:c4], wb[:, c4:c6], wb[:, c6:]]

    def head_blockdiag(width):
        hid = jnp.arange(width) // HEAD_DIM
        return jnp.where(hid[:, None] == hid[None, :], 1.0 / HEAD_DIM, 0.0).astype(BF16)

    consts = [gmix.reshape(1, D_MODEL)] + ws + [
        jnp.tile(gq, B_Q_WIDTH // HEAD_DIM).reshape(1, B_Q_WIDTH),
        jnp.tile(gk, B_KV_WIDTH // HEAD_DIM).reshape(1, B_KV_WIDTH),
        head_blockdiag(B_Q_WIDTH), head_blockdiag(B_KV_WIDTH)]

    a_shapes, a_specs = [], []
    for _ in range(3):
        for dil in A_DILATIONS:
            a_shapes.append(jax.ShapeDtypeStruct((b, dil, s // dil, A_GROUP_WIDTH), BF16))
            a_specs.append(pl.BlockSpec((None, dil, tm // dil, A_GROUP_WIDTH),
                                        lambda i: (i // nts, 0, i % nts, 0)))
    kpb = B_TK // tm
    b_shapes = [jax.ShapeDtypeStruct((b, B_KV_HEADS, nts, B_KV_WIDTH, B_GROUP_HEADS * tm), BF16),
                jax.ShapeDtypeStruct((n, B_KV_WIDTH), BF16),
                jax.ShapeDtypeStruct((b, B_KV_HEADS, s // B_TK, HEAD_DIM, B_TK), BF16),
                jax.ShapeDtypeStruct((n, GATE_WIDTH), BF16)]
    b_specs = [pl.BlockSpec((None, B_KV_HEADS, None, B_KV_WIDTH, B_GROUP_HEADS * tm),
                            lambda i: (i // nts, 0, i % nts, 0, 0)),
               pl.BlockSpec((tm, B_KV_WIDTH), lambda i: (i, 0)),
               pl.BlockSpec((None, B_KV_HEADS, None, HEAD_DIM, tm),
                            lambda i: (i // nts, 0, (i % nts) // kpb, 0, i % kpb)),
               pl.BlockSpec((tm, GATE_WIDTH), lambda i: (i, 0))]

    return pl.pallas_call(
        _in_proj_kernel,
        out_shape=a_shapes + b_shapes,
        grid=(n // tm,),
        in_specs=[pl.BlockSpec((tm, D_MODEL), lambda i: (i, 0)),
                  pl.BlockSpec((tm, tab.shape[1]), lambda i: (i % nts, 0))] + [_full_spec(a) for a in consts],
        out_specs=a_specs + b_specs,
        scratch_shapes=[pltpu.VMEM((tm, LANES), F32)],
        compiler_params=_cparams(("parallel",)),
        name="in_proj",
    )(xf, tab, *consts)


def _band_bias():
    qi = jnp.arange(A_SUB)[:, None]
    kj = jnp.arange(A_SUB + 2 * A_RADIUS)[None, :]
    band = jnp.abs(kj - A_RADIUS - qi) <= A_RADIUS
    first = band & (kj >= A_RADIUS)
    last = band & (kj < A_SUB + A_RADIUS)
    return jnp.where(jnp.stack([band, first, last]), 0.0, NEG_INF).astype(F32)


def _attn_a_kernel(bias_ref, q_ref, kp_ref, kc_ref, kn_ref, vp_ref, vc_ref, vn_ref, o_ref, lse_ref):
    classes, tq, _ = q_ref.shape
    nsub = tq // A_SUB
    i = pl.program_id(2)
    last_tile = pl.num_programs(2) - 1
    first_head = lax.broadcasted_iota(jnp.int32, (1, LANES), 1) < HEAD_DIM
    nt_dims = (((1,), (1,)), ((), ()))

    def window(p_ref, c_ref, n_ref, c, u, cols):
        lo, hi = u * A_SUB - A_RADIUS, (u + 1) * A_SUB + A_RADIUS
        parts = [p_ref[c, :, cols]] if lo < 0 else []
        parts.append(c_ref[c, max(lo, 0):min(hi, tq), cols])
        if hi > tq:
            parts.append(n_ref[c, :, cols])
        return parts[0] if len(parts) == 1 else jnp.concatenate(parts, axis=0)

    for c in range(classes):
        for u in range(nsub):
            if u == 0:
                bias = bias_ref[jnp.where(i == 0, 1, 0)]
            elif u == nsub - 1:
                bias = bias_ref[jnp.where(i == last_tile, 2, 0)]
            else:
                bias = bias_ref[0]
            rows = slice(u * A_SUB, (u + 1) * A_SUB)
            for pair in range(A_GROUP_WIDTH // LANES):
                cols = slice(pair * LANES, (pair + 1) * LANES)
                k = window(kp_ref, kc_ref, kn_ref, c, u, cols)
                v = window(vp_ref, vc_ref, vn_ref, c, u, cols)
                q = q_ref[c, rows, cols]
                outs, lses = [], []
                for hh in range(LANES // HEAD_DIM):
                    mine = first_head if hh == 0 else jnp.logical_not(first_head)
                    qm = jnp.where(mine, q, jnp.zeros_like(q))
                    sc = lax.dot_general(qm, k, nt_dims, preferred_element_type=F32) + bias
                    m = jnp.max(sc, axis=1, keepdims=True)
                    p = jnp.exp2(sc - m)
                    den = jnp.sum(p, axis=1, keepdims=True)
                    outs.append(jnp.dot(p.astype(BF16), v, preferred_element_type=F32) / den)
                    lses.append(m * LN2 + jnp.log(den))
                o_ref[c, rows, cols] = jnp.where(first_head, outs[0], outs[1])
                lse_ref[c, rows, cols] = jnp.where(first_head, lses[0], lses[1])


def _attn_a(q, k, v, gi):
    b, dil, cl, _ = q.shape
    tq = min(A_TQ, cl)
    cb = min(dil, A_TQ // tq)
    assert tq >= 2 * A_SUB and A_SUB == 2 * A_RADIUS and dil % cb == 0
    hb = tq // A_RADIUS
    n_halo = cl // A_RADIUS
    bias = _band_bias()
    main = pl.BlockSpec((None, cb, tq, A_GROUP_WIDTH), lambda bi, r, i: (bi, r, i, 0))
    prev = pl.BlockSpec((None, cb, A_RADIUS, A_GROUP_WIDTH),
                        lambda bi, r, i: (bi, r, jnp.maximum(i * hb - 1, 0), 0))
    nxt = pl.BlockSpec((None, cb, A_RADIUS, A_GROUP_WIDTH),
                       lambda bi, r, i: (bi, r, jnp.minimum((i + 1) * hb, n_halo - 1), 0))
    return pl.pallas_call(
        _attn_a_kernel,
        out_shape=[jax.ShapeDtypeStruct((b, dil, cl, A_GROUP_WIDTH), F32)] * 2,
        grid=(b, dil // cb, cl // tq),
        in_specs=[_full_spec(bias), main, prev, main, nxt, prev, main, nxt],
        out_specs=[main, main],
        compiler_params=_cparams(("parallel", "parallel", "parallel")),
        name=f"attn_a{gi}",
    )(bias, q, k, k, k, v, v, v)


def _flash_b_kernel(qt_ref, k_ref, vt_ref, o_ref, acc_sc, l_sc, m_sc, p0_sc, p1_sc, *, online):
    nk, tk, _ = k_ref.shape
    width = qt_ref.shape[1]
    tq = width // B_GROUP_HEADS
    acc_sc[...] = jnp.zeros(acc_sc.shape, F32)
    l_sc[...] = jnp.zeros(l_sc.shape, F32)

    def scores(j):
        return jnp.dot(k_ref[j], qt_ref[...], preferred_element_type=F32)

    def column_sums(p):
        return p.reshape(tk // SUBLANES, SUBLANES, width).sum(axis=0)

    def accumulate(j, pb):
        vt = vt_ref[j]
        for h in range(B_GROUP_HEADS):
            cols = slice(h * tq, (h + 1) * tq)
            acc_sc[:, cols] += jnp.dot(vt, pb[:, cols], preferred_element_type=F32)

    if online:
        m_sc[...] = jnp.full(m_sc.shape, NEG_INF, F32)

        def body(j, carry):
            sc = scores(j)
            m_old = m_sc[...]
            m_new = jnp.maximum(m_old, jnp.max(sc, axis=0, keepdims=True))
            alpha = jnp.exp2(m_old - m_new)
            m_sc[...] = m_new
            p = jnp.exp2(sc - m_new)
            l_sc[...] = alpha * l_sc[...] + column_sums(p)
            acc_sc[...] = alpha * acc_sc[...]
            accumulate(j, p.astype(BF16))
            return carry

        lax.fori_loop(0, nk, body, 0)
    else:
        def exponentials(sc, p_out):
            p = jnp.exp2(sc)
            l_sc[...] += column_sums(p)
            p_out[...] = p.astype(BF16)

        def stage(j, p_in, p_out):
            sc = scores(j)
            accumulate(j - 1, p_in[...])
            exponentials(sc, p_out)

        exponentials(scores(0), p0_sc)

        def body(jj, carry):
            stage(2 * jj + 1, p0_sc, p1_sc)
            stage(2 * jj + 2, p1_sc, p0_sc)
            return carry

        lax.fori_loop(0, nk // 2 - 1, body, 0)
        stage(nk - 1, p0_sc, p1_sc)
        accumulate(nk - 1, p1_sc[...])
    o = acc_sc[...] / jnp.sum(l_sc[...], axis=0, keepdims=True)
    o = jnp.concatenate([o[:, h * tq:(h + 1) * tq] for h in range(B_GROUP_HEADS)], axis=0)
    o_ref[...] = o.T.astype(o_ref.dtype)


def _flash_b(qt, kb, vt, online):
    b, _, nq, _, width = qt.shape
    tq = width // B_GROUP_HEADS
    _, nk, tk, _ = kb.shape
    return pl.pallas_call(
        functools.partial(_flash_b_kernel, online=online),
        out_shape=jax.ShapeDtypeStruct((b, B_KV_HEADS, nq * tq, B_GROUP_WIDTH), BF16),
        grid=(b, B_KV_HEADS, nq),
        in_specs=[pl.BlockSpec((None, None, None, B_KV_WIDTH, width), lambda bi, g, i: (bi, g, i, 0, 0)),
                  pl.BlockSpec((None, nk, tk, B_KV_WIDTH), lambda bi, g, i: (bi, 0, 0, 0)),
                  pl.BlockSpec((None, None, nk, HEAD_DIM, tk), lambda bi, g, i: (bi, g, 0, 0, 0))],
        out_specs=pl.BlockSpec((None, None, tq, B_GROUP_WIDTH), lambda bi, g, i: (bi, g, i, 0)),
        scratch_shapes=[pltpu.VMEM((HEAD_DIM, width), F32), pltpu.VMEM((SUBLANES, width), F32),
                        pltpu.VMEM((1, width), F32),
                        pltpu.VMEM((tk, width), BF16), pltpu.VMEM((tk, width), BF16)],
        compiler_params=_cparams(("parallel", "parallel", "parallel")),
        name="flash_b_online" if online else "flash_b",
    )(qt, kb, vt)


def _stage_classes(blk_ref, stage_ref):
    dil, rows, width = blk_ref.shape
    if dil == 1:
        return
    for c in range(width // LANES):
        for r in range(dil):
            stage_ref.at[c][pl.ds(r, rows, stride=dil), :] = blk_ref[r, :, c * LANES:(c + 1) * LANES]


def _token_rows(blk_ref, stage_ref):
    if blk_ref.shape[0] == 1:
        return blk_ref[0]
    return jnp.concatenate([stage_ref[c] for c in range(stage_ref.shape[0])], axis=1)


def _post_attn_kernel(x_ref, o0_ref, o1_ref, o2_ref, l0_ref, l1_ref, l2_ref, ob0_ref, ob1_ref, gates_ref,
                      wa_ref, wb0_ref, wb1_ref, wout_ref, gffn_ref, wrh_ref, wrl_ref,
                      x1_ref, h2_ref, aff_ref, so1, so2, sl1, sl2):
    for blk, stage in ((l1_ref, sl1), (l2_ref, sl2), (o1_ref, so1), (o2_ref, so2)):
        _stage_classes(blk, stage)
    nt = (((1,), (1,)), ((), ()))
    l0 = _token_rows(l0_ref, None)
    l1 = _token_rows(l1_ref, sl1)
    l2 = _token_rows(l2_ref, sl2)
    m = jnp.maximum(jnp.maximum(l0, l1), l2)
    e0, e1, e2 = jnp.exp(l0 - m), jnp.exp(l1 - m), jnp.exp(l2 - m)
    oa = (e0 * _token_rows(o0_ref, None) + e1 * _token_rows(o1_ref, so1)
          + e2 * _token_rows(o2_ref, so2)) / (e0 + e1 + e2)
    ya = jnp.dot(oa.astype(BF16), wa_ref[...], preferred_element_type=F32)
    yb = (jnp.dot(ob0_ref[...], wb0_ref[...], preferred_element_type=F32)
          + jnp.dot(ob1_ref[...], wb1_ref[...], preferred_element_type=F32))
    ga = gates_ref[:, :D_MODEL].astype(F32)
    gb = gates_ref[:, D_MODEL:].astype(F32)
    z = (ga * ya + gb * yb).astype(BF16)
    x1 = x_ref[...] + jnp.dot(z, wout_ref[...], preferred_element_type=F32)
    x1_ref[...] = x1
    ms = jnp.mean(x1 * x1, axis=-1, keepdims=True)
    h2 = x1 * lax.rsqrt(ms + EPS) * gffn_ref[...]
    hi = h2.astype(BF16)
    lo = (h2 - hi.astype(F32)).astype(BF16)
    h2_ref[...] = hi
    both = lax.dot_general(wrl_ref[...], hi, nt, preferred_element_type=F32)
    logits = (both[:N_EXPERTS] + both[N_EXPERTS:]
              + lax.dot_general(wrh_ref[...], lo, nt, preferred_element_type=F32))
    logits = logits - jnp.max(logits, axis=0, keepdims=True)
    e = jnp.exp(logits)
    aff = e / jnp.sum(e, axis=0, keepdims=True)
    for j in range(aff_ref.shape[0]):
        aff_ref[j] = aff[:, j * LANES:(j + 1) * LANES]


def _post_attn(xf, oas, lses, ob, gates, b, s, w_a, w_b, w_out, g_ffn, w_router):
    n = xf.shape[0]
    tm = ROW_TILE
    nts = s // tm
    wa = w_a.astype(BF16)
    wb = w_b.astype(BF16)
    wb0, wb1 = wb[:B_GROUP_WIDTH], wb[B_GROUP_WIDTH:]
    wout = w_out.astype(BF16)
    wrt = w_router.T
    wrh = wrt.astype(BF16)
    wrl = jnp.concatenate([wrh, (wrt - wrh.astype(F32)).astype(BF16)], axis=0)
    consts = [wa, wb0, wb1, wout, g_ffn.reshape(1, D_MODEL), wrh, wrl]

    def rows(w):
        return pl.BlockSpec((tm, w), lambda i: (i, 0))

    def class_spec(dil):
        return pl.BlockSpec((None, dil, tm // dil, A_GROUP_WIDTH), lambda i: (i // nts, 0, i % nts, 0))

    def ob_spec(g):
        return pl.BlockSpec((None, None, tm, B_GROUP_WIDTH), lambda i: (i // nts, g, i % nts, 0))

    a_specs = [class_spec(d) for d in A_DILATIONS]
    return pl.pallas_call(
        _post_attn_kernel,
        out_shape=[jax.ShapeDtypeStruct((n, D_MODEL), F32),
                   jax.ShapeDtypeStruct((n, D_MODEL), BF16),
                   jax.ShapeDtypeStruct((n // LANES, N_EXPERTS, LANES), F32)],
        grid=(n // tm,),
        in_specs=[rows(D_MODEL)] + a_specs + a_specs + [ob_spec(0), ob_spec(1), rows(GATE_WIDTH)]
                 + [_full_spec(a) for a in consts],
        out_specs=[rows(D_MODEL), rows(D_MODEL),
                   pl.BlockSpec((tm // LANES, N_EXPERTS, LANES), lambda i: (i, 0, 0))],
        scratch_shapes=[pltpu.VMEM((A_GROUP_WIDTH // LANES, tm, LANES), F32)] * 4,
        compiler_params=_cparams(("parallel",)),
        name="post_attn",
    )(xf, *oas, *lses, ob, ob, gates, *consts)


def _route_kernel(aff_ref, tri_ref, posl_ref, starts_ref, sel_sc, *, cap, idx_bits, tiles_per_block):
    nlt = aff_ref.shape[0]
    capf = float(cap)

    def count(mask):
        part = jnp.sum(jnp.where(mask, 1.0, 0.0), axis=0)
        return jnp.sum(part, axis=1, keepdims=True)[None]

    def value_body(i, tau):
        bits = pltpu.bitcast(aff_ref[...], jnp.int32)
        cand = tau | lax.shift_left(jnp.int32(1), 30 - i)
        return jnp.where(count(bits >= cand) >= capf, cand, tau)

    tau = lax.fori_loop(0, 31, value_body, jnp.zeros((1, N_EXPERTS, 1), jnp.int32))
    bits = pltpu.bitcast(aff_ref[...], jnp.int32)
    need = capf - count(bits > tau)
    tok = (lax.broadcasted_iota(jnp.int32, bits.shape, 0) * LANES
           + lax.broadcasted_iota(jnp.int32, bits.shape, 2))

    def index_body(i, last):
        b = pltpu.bitcast(aff_ref[...], jnp.int32)
        cand = last | lax.shift_left(jnp.int32(1), idx_bits - 1 - i)
        return jnp.where(count((b == tau) & (tok < cand)) < need, cand, last)

    last = lax.fori_loop(0, idx_bits, index_body, jnp.zeros((1, N_EXPERTS, 1), jnp.int32))
    sel_sc[...] = jnp.where((bits > tau) | ((bits == tau) & (tok <= last)), 1.0, 0.0)

    sel = sel_sc[...]
    sb = sel.reshape(nlt * N_EXPERTS, LANES).astype(BF16)
    tri = tri_ref[...]
    within = jnp.dot(sb, tri, preferred_element_type=F32).reshape(nlt, N_EXPERTS, LANES)
    total = jnp.dot(sb, jnp.ones((LANES, LANES), BF16),
                    preferred_element_type=F32).reshape(nlt, N_EXPERTS, LANES)
    upto = total
    step = 1
    while step < nlt:
        upto = upto + jnp.concatenate([jnp.zeros((step, N_EXPERTS, LANES), F32), upto[:nlt - step]], axis=0)
        step *= 2
    before = upto - total
    posl_ref[...] = jnp.where(sel > 0, before + within - sel, -1.0).astype(jnp.int32)
    starts_ref[...] = before.reshape(nlt // tiles_per_block, tiles_per_block, N_EXPERTS, LANES)[:, 0].astype(jnp.int32)


def _route(aff3, cap):
    nlt = aff3.shape[0]
    n = nlt * LANES
    tpb = MOE_T // LANES
    tri = (jnp.arange(LANES)[:, None] <= jnp.arange(LANES)[None, :]).astype(BF16)
    vm = pl.BlockSpec(memory_space=pltpu.VMEM)
    return pl.pallas_call(
        functools.partial(_route_kernel, cap=cap, idx_bits=(n - 1).bit_length(), tiles_per_block=tpb),
        out_shape=[jax.ShapeDtypeStruct((nlt, N_EXPERTS, LANES), jnp.int32),
                   jax.ShapeDtypeStruct((n // MOE_T, N_EXPERTS, LANES), jnp.int32)],
        in_specs=[vm, vm],
        out_specs=[vm, vm],
        scratch_shapes=[pltpu.VMEM((nlt, N_EXPERTS, LANES), F32)],
        compiler_params=pltpu.CompilerParams(vmem_limit_bytes=VMEM_LIMIT),
        name="route",
    )(aff3, tri)


def _block_slots(s_ref, i, nt):
    return [(s_ref[e * (nt + 1) + i], s_ref[e * (nt + 1) + i + 1]) for e in range(N_EXPERTS)]


def _align_down(x):
    return (x >> MOE_ALIGN_SHIFT) << MOE_ALIGN_SHIFT


def _moe_gather_kernel(s_ref, aff_ref, posl_ref, h2_ref, xs_hbm, tm_ref,
                       haug_sc, pall_sc, stage_sc, carry_sc, pending_sc, sem, *, nt, cap):
    i = pl.program_id(0)
    nj = posl_ref.shape[0]
    ws = MOE_WS

    @pl.when(i == 0)
    def _():
        carry_sc[...] = jnp.zeros(carry_sc.shape, BF16)
        pending_sc[0] = 0
        pending_sc[1] = 0
        pad = xs_hbm.shape[1] - cap
        stage_sc[0] = jnp.zeros(stage_sc.shape[1:], BF16)
        fills = [pltpu.make_async_copy(stage_sc.at[0, pl.ds(0, pad)], xs_hbm.at[e, pl.ds(cap, pad)], sem)
                 for e in range(N_EXPERTS)]
        for cp in fills:
            cp.start()
        for cp in fills:
            cp.wait()

    for j in range(nj):
        pj = posl_ref[j]
        gj = jnp.where(pj >= 0, aff_ref[j], 0.0)
        blk = jnp.concatenate([gj, pj.astype(F32), gj, jnp.zeros((LANES - 3 * N_EXPERTS, LANES), F32)], axis=0)
        tm_ref[j * LANES:(j + 1) * LANES, :] = blk.T
    tmv = tm_ref[...]
    lane = lax.broadcasted_iota(jnp.int32, tmv.shape, 1)
    ghi = tmv.astype(BF16)
    glo = (tmv - ghi.astype(F32)).astype(BF16)
    zero = jnp.zeros_like(ghi)
    haug_sc[:, :D_MODEL] = h2_ref[...]
    haug_sc[:, D_MODEL:] = jnp.where(lane < N_EXPERTS, ghi,
                                     jnp.where((lane >= TM_GATE_LO) & (lane < TM_GATE_LO + N_EXPERTS), glo, zero))

    astarts, kks, offcs = [], [], []
    nw = jnp.int32(0)
    for st, en in _block_slots(s_ref, i, nt):
        ast = _align_down(st)
        span = _align_down(en) - ast
        kk = span >> MOE_WS_SHIFT
        astarts.append(ast)
        kks.append(kk)
        offcs.append(span - kk * ws)
        nw = jnp.maximum(nw, kk + 1)

    def wait_window():
        for _ in range(N_EXPERTS):
            pltpu.make_async_copy(stage_sc.at[0, pl.ds(0, ws)], xs_hbm.at[0, pl.ds(0, ws)], sem).wait()

    row = lax.broadcasted_iota(jnp.int32, (ws, LANES), 0)

    def window(k, carry):
        for e in range(N_EXPERTS):
            base = astarts[e] + k * ws
            for j in range(nj):
                hit = (posl_ref[j, e:e + 1, :] - base) == row
                pall_sc[e * ws:(e + 1) * ws, j * LANES:(j + 1) * LANES] = jnp.where(hit, 1.0, 0.0).astype(BF16)
        stage = stage_sc.at[pending_sc[1]]
        stage[...] = jnp.dot(pall_sc[...], haug_sc[...], preferred_element_type=F32).astype(BF16)

        @pl.when(k == 0)
        def _():
            for e in range(N_EXPERTS):
                stage[e * ws:e * ws + MOE_ALIGN, :] += carry_sc[e]

        for e in range(N_EXPERTS):
            @pl.when(k == kks[e])
            def _():
                carry_sc[e] = stage[pl.ds(pl.multiple_of(e * ws + offcs[e], MOE_ALIGN), MOE_ALIGN), :]

        @pl.when(pending_sc[0] == 1)
        def _():
            wait_window()

        for e in range(N_EXPERTS):
            base = pl.multiple_of(astarts[e] + k * ws, MOE_ALIGN)
            pltpu.make_async_copy(stage.at[pl.ds(e * ws, ws)], xs_hbm.at[e, pl.ds(base, ws)], sem).start()
        pending_sc[0] = 1
        pending_sc[1] = 1 - pending_sc[1]
        return carry

    lax.fori_loop(0, nw, window, 0)

    @pl.when((i == nt - 1) & (pending_sc[0] == 1))
    def _():
        wait_window()
        pending_sc[0] = 0


def _moe_gather(s_flat, aff3, posl, h2, cap):
    n = h2.shape[0]
    t = MOE_T
    nt = n // t
    cap_pad = cap + (t // MOE_WS + 1) * MOE_WS
    blk3 = pl.BlockSpec((t // LANES, N_EXPERTS, LANES), lambda i, s: (i, 0, 0))
    grid_spec = pltpu.PrefetchScalarGridSpec(
        num_scalar_prefetch=1,
        grid=(nt,),
        in_specs=[blk3, blk3, pl.BlockSpec((t, D_MODEL), lambda i, s: (i, 0))],
        out_specs=[pl.BlockSpec(memory_space=pl.ANY), pl.BlockSpec((t, LANES), lambda i, s: (i, 0))],
        scratch_shapes=[pltpu.VMEM((t, XS_WIDTH), BF16),
                        pltpu.VMEM((N_EXPERTS * MOE_WS, t), BF16),
                        pltpu.VMEM((2, N_EXPERTS * MOE_WS, XS_WIDTH), BF16),
                        pltpu.VMEM((N_EXPERTS, MOE_ALIGN, XS_WIDTH), BF16),
                        pltpu.SMEM((2,), jnp.int32),
                        pltpu.SemaphoreType.DMA(())])
    return pl.pallas_call(
        functools.partial(_moe_gather_kernel, nt=nt, cap=cap),
        out_shape=[jax.ShapeDtypeStruct((N_EXPERTS, cap_pad, XS_WIDTH), BF16),
                   jax.ShapeDtypeStruct((n, LANES), F32)],
        grid_spec=grid_spec,
        compiler_params=_cparams(("arbitrary",)),
        name="moe_gather",
    )(s_flat, aff3, posl, h2)


def _moe_ffn_kernel(xs_ref, w1_ref, w3_ref, w2_ref, ye_ref):
    xe = xs_ref[:, :D_MODEL]
    route = xs_ref[:, D_MODEL:].astype(F32)
    lane = lax.broadcasted_iota(jnp.int32, route.shape, 1)
    e = pl.program_id(0)
    gate = jnp.sum(jnp.where((lane == e) | (lane == e + TM_GATE_LO), route, 0.0), axis=1, keepdims=True)
    a = jnp.dot(xe, w1_ref[...], preferred_element_type=F32)
    u = jnp.dot(xe, w3_ref[...], preferred_element_type=F32)
    hid = (a * jax.nn.sigmoid(a) * u).astype(BF16)
    ye = jnp.dot(hid, w2_ref[...], preferred_element_type=F32) * gate
    ye_ref[...] = ye.astype(ye_ref.dtype)


def _moe_ffn(xs, cap, w1, w3, w2):
    e = xs.shape[0]
    tc = min(FFN_TC, cap)
    wspec = pl.BlockSpec((None, D_MODEL, D_MODEL), lambda ei, ci: (ei, 0, 0))
    return pl.pallas_call(
        _moe_ffn_kernel,
        out_shape=jax.ShapeDtypeStruct((e, cap, D_MODEL), BF16),
        grid=(e, cap // tc),
        in_specs=[pl.BlockSpec((None, tc, XS_WIDTH), lambda ei, ci: (ei, ci, 0)), wspec, wspec, wspec],
        out_specs=pl.BlockSpec((None, tc, D_MODEL), lambda ei, ci: (ei, ci, 0)),
        compiler_params=_cparams(("parallel", "arbitrary")),
        name="moe_ffn",
    )(xs, w1, w3, w2)


def _final_kernel(s_ref, x1_ref, tm_ref, p_ref, ye_hbm, wple_ref, wpg_ref, gple_ref, gfin_ref, expand_ref, out_ref,
                  buf_sc, pall_sc, y_sc, sem, *, nt, cap):
    i = pl.program_id(0)
    ws = MOE_WS
    t = x1_ref.shape[0]
    slot = i % 2

    def window_rows(blk, k):
        out = []
        for st, _ in _block_slots(s_ref, blk, nt):
            lo = _align_down(st) + k * ws
            out.append((lo, pl.multiple_of(jnp.minimum(lo, cap - ws), MOE_ALIGN)))
        return out

    def copies(rows, buf, which):
        return [pltpu.make_async_copy(ye_hbm.at[e, pl.ds(base, ws)], buf_sc.at[buf, pl.ds(e * ws, ws)], sem.at[which])
                for e, (_, base) in enumerate(rows)]

    @pl.when(i == 0)
    def _():
        for cp in copies(window_rows(0, 0), 0, 0):
            cp.start()

    @pl.when(i + 1 < nt)
    def _():
        for cp in copies(window_rows(i + 1, 0), 1 - slot, 1 - slot):
            cp.start()

    nw = jnp.int32(0)
    for st, en in _block_slots(s_ref, i, nt):
        nw = jnp.maximum(nw, (en - _align_down(st) + ws - 1) >> MOE_WS_SHIFT)
    lane = lax.broadcasted_iota(jnp.int32, (1, LANES), 1)
    col = (lax.broadcasted_iota(jnp.int32, (1, N_EXPERTS * ws), 1) & (ws - 1)).astype(F32)

    def onehot(rows):
        basev = jnp.zeros((1, LANES), F32)
        lov = jnp.full((1, LANES), float(2 ** 30), F32)
        for e, (lo, base) in enumerate(rows):
            basev = jnp.where(lane == TM_POS + e, base.astype(F32), basev)
            lov = jnp.where(lane == TM_POS + e, lo.astype(F32), lov)
        tmv = tm_ref[...]
        off = tmv - basev
        off = jnp.where((tmv >= lov) & (off >= 0.0) & (off < float(ws)), off, -1.0)
        spread = jnp.dot(off.astype(BF16), expand_ref[...], preferred_element_type=F32)
        pall_sc[...] = jnp.where(spread == col, 1.0, 0.0).astype(BF16)

    rows0 = window_rows(i, 0)
    onehot(rows0)
    for cp in copies(rows0, slot, slot):
        cp.wait()
    y_sc[...] = jnp.dot(pall_sc[...], buf_sc[slot], preferred_element_type=F32)

    def window(k, carry):
        rows = window_rows(i, k)
        cps = copies(rows, slot, slot)
        for cp in cps:
            cp.start()
        onehot(rows)
        for cp in cps:
            cp.wait()
        y_sc[...] += jnp.dot(pall_sc[...], buf_sc[slot], preferred_element_type=F32)
        return carry

    lax.fori_loop(1, nw, window, 0)

    x2 = x1_ref[...] + y_sc[...]
    ms = jnp.mean(x2 * x2, axis=-1, keepdims=True)
    hp = (x2 * lax.rsqrt(ms + EPS) * gple_ref[...]).astype(BF16)
    gt = jax.nn.sigmoid(jnp.dot(hp, wpg_ref[...], preferred_element_type=F32))
    emb = jnp.dot(p_ref[...].astype(BF16), wple_ref[...], preferred_element_type=F32)
    x3 = x2 + emb * gt
    ms3 = jnp.mean(x3 * x3, axis=-1, keepdims=True)
    out_ref[...] = x3 * lax.rsqrt(ms3 + EPS) * gfin_ref[...]


def _final(s_flat, x1, tm, pf, ye, w_ple, w_ple_gate, g_ple, g_final):
    n = x1.shape[0]
    t = MOE_T
    nt = n // t
    cap = ye.shape[1]
    src = jnp.arange(LANES)[:, None] - TM_POS
    dst = jnp.arange(N_EXPERTS * MOE_WS)[None, :] >> MOE_WS_SHIFT
    expand = (src == dst).astype(BF16)
    consts = [w_ple.astype(BF16), w_ple_gate.astype(BF16), g_ple.reshape(1, D_MODEL), g_final.reshape(1, D_MODEL),
              expand]

    def rows(w):
        return pl.BlockSpec((t, w), lambda i, s: (i, 0))

    grid_spec = pltpu.PrefetchScalarGridSpec(
        num_scalar_prefetch=1,
        grid=(nt,),
        in_specs=[rows(D_MODEL), rows(LANES), rows(PLE_DIM), pl.BlockSpec(memory_space=pl.ANY)]
                 + [_full_spec(a) for a in consts],
        out_specs=rows(D_MODEL),
        scratch_shapes=[pltpu.VMEM((2, N_EXPERTS * MOE_WS, D_MODEL), BF16),
                        pltpu.VMEM((t, N_EXPERTS * MOE_WS), BF16),
                        pltpu.VMEM((t, D_MODEL), F32),
                        pltpu.SemaphoreType.DMA((2,))])
    return pl.pallas_call(
        functools.partial(_final_kernel, nt=nt, cap=cap),
        out_shape=jax.ShapeDtypeStruct((n, D_MODEL), F32),
        grid_spec=grid_spec,
        compiler_params=_cparams(("arbitrary",)),
        name="final",
    )(s_flat, x1, tm, pf, ye, *consts)


def _run_group(x, p, g_mix, w_in, g_qn, g_kn, w_a_proj, w_b_proj, w_out, g_ffn, w_router,
               w1, w3, w2, g_ple, w_ple, w_ple_gate, g_final):
    b, s, _ = x.shape
    n = b * s
    assert s % B_TK == 0 and s % (A_DILATIONS[-1] * A_RADIUS) == 0
    xf = x.reshape(n, D_MODEL)
    tab = _rope_tables(s)
    outs = _in_proj(xf, tab, b, s, g_mix, w_in, g_qn, g_kn)
    qas, kas, vas = outs[0:3], outs[3:6], outs[6:9]
    qt, kb, vt, gates = outs[9:]

    oas, lses = [], []
    for gi, (window, dil) in enumerate(A_PATTERNS):
        assert window // (2 * dil) == A_RADIUS
        o, lse = _attn_a(qas[gi], kas[gi], vas[gi], gi)
        oas.append(o)
        lses.append(lse)

    score_bound = (HEAD_DIM * Q_SCALE * 1.02) * jnp.max(jnp.abs(g_qn)) * jnp.max(jnp.abs(g_kn))
    kb4 = kb.reshape(b, s // B_TK, B_TK, B_KV_WIDTH)
    ob = lax.cond(score_bound <= B_UNSHIFTED_SCORE_MAX,
                  lambda: _flash_b(qt, kb4, vt, online=False),
                  lambda: _flash_b(qt, kb4, vt, online=True))

    x1, h2, aff3 = _post_attn(xf, oas, lses, ob, gates, b, s, w_a_proj, w_b_proj, w_out, g_ffn, w_router)

    cap = CAPACITY_FACTOR * n // N_EXPERTS
    assert cap % MOE_WS == 0 and cap >= MOE_WS
    posl, starts = _route(aff3, cap)
    s_flat = jnp.concatenate([starts[:, :, 0].T, jnp.full((N_EXPERTS, 1), cap, jnp.int32)], axis=1).reshape(-1)
    xs, tm = _moe_gather(s_flat, aff3, posl, h2, cap)
    ye = _moe_ffn(xs, cap, w1, w3, w2)
    out = _final(s_flat, x1, tm, p.reshape(n, PLE_DIM), ye, w_ple, w_ple_gate, g_ple, g_final)
    return out.reshape(b, s, D_MODEL)


def kernel(x_prompt, x_sample, p_prompt, p_sample, g_mix, w_in, g_qn, g_kn, w_a_proj, w_b_proj, w_out, g_ffn,
           w_router, w_exp_gate, w_exp_up, w_exp_down, g_ple, w_ple, w_ple_gate, g_final):
    assert g_mix.shape[0] == 1, "single layer"
    w1 = w_exp_gate[0].astype(BF16)
    w3 = w_exp_up[0].astype(BF16)
    w2 = w_exp_down[0].astype(BF16)
    args = (g_mix[0], w_in[0], g_qn[0], g_kn[0], w_a_proj[0], w_b_proj[0], w_out[0], g_ffn[0], w_router[0],
            w1, w3, w2, g_ple[0], w_ple[0], w_ple_gate[0], g_final)
    y_prompt = _run_group(x_prompt, p_prompt[0], *args)
    y_sample = _run_group(x_sample, p_sample[0], *args)
    return (y_prompt, y_sample)
```

```python
import functools
import math

import jax
import jax.numpy as jnp
from jax import lax
from jax.experimental import pallas as pl
from jax.experimental.pallas import tpu as pltpu

F32 = jnp.float32
BF16 = jnp.bfloat16

D_MODEL = 1024
HEAD_DIM = 64
A_PATTERNS = ((128, 1), (512, 4), (2048, 16))
A_DILATIONS = tuple(d for _, d in A_PATTERNS)
A_GROUPS = len(A_PATTERNS)
A_HEADS_PER_GROUP = 4
A_WIDTH = 768
A_GROUP_WIDTH = A_HEADS_PER_GROUP * HEAD_DIM
A_ROT_DIM = 16
A_RADIUS = 64
ROPE_THETA = 500000.0
B_Q_WIDTH = 512
B_KV_HEADS = 2
B_KV_WIDTH = 128
B_GROUP_HEADS = 4
B_GROUP_WIDTH = B_GROUP_HEADS * HEAD_DIM
AXIAL_THETA = 10000.0
GRID_W = 64
GATE_WIDTH = 2048
N_EXPERTS = 16
CAPACITY_FACTOR = 2
PLE_DIM = 256
EPS = 1e-6
NEG_INF = -1e30
LN2 = math.log(2.0)
Q_SCALE = (HEAD_DIM ** -0.5) / LN2
B_UNSHIFTED_SCORE_MAX = 50.0

LANES = 128
SUBLANES = 8
MXU_WIDTH = 256
VMEM_LIMIT = 48 * 1024 * 1024

ROW_TILE = 512
A_TQ = 1024
A_SUB = 128
B_TQ = ROW_TILE
B_TK = 4 * ROW_TILE
FFN_TC = 1024
MOE_T = ROW_TILE
MOE_WS_SHIFT = 7
MOE_WS = 1 << MOE_WS_SHIFT
MOE_ALIGN_SHIFT = 4
MOE_ALIGN = 1 << MOE_ALIGN_SHIFT
XS_WIDTH = D_MODEL + LANES
TM_POS = N_EXPERTS
TM_GATE_LO = 2 * N_EXPERTS


def _cparams(sem):
    return pltpu.CompilerParams(dimension_semantics=sem, vmem_limit_bytes=VMEM_LIMIT)


def _full_spec(a):
    return pl.BlockSpec(a.shape, lambda *_: (0,) * a.ndim)


def _rope_tables(s):
    pos = jnp.arange(s)
    posf = pos.astype(F32)
    inv_a = jnp.power(ROPE_THETA, -jnp.arange(0, A_ROT_DIM, 2, dtype=F32) / A_ROT_DIM)
    ang = posf[:, None] * inv_a[None, :]
    ca, sa = jnp.cos(ang), jnp.sin(ang)
    z8 = jnp.zeros_like(sa)
    rest = HEAD_DIM - A_ROT_DIM
    cos_a = jnp.concatenate([ca, ca, jnp.ones((s, rest), F32)], axis=1)
    sp_a = jnp.concatenate([z8, sa, jnp.zeros((s, rest), F32)], axis=1)
    sm_a = jnp.concatenate([-sa, z8, jnp.zeros((s, rest), F32)], axis=1)

    hb = HEAD_DIM // 2
    inv_b = jnp.power(AXIAL_THETA, -jnp.arange(0, hb, 2, dtype=F32) / hb)
    row = (pos // GRID_W).astype(F32)
    col = (pos % GRID_W).astype(F32)
    ar = row[:, None] * inv_b[None, :]
    ac = col[:, None] * inv_b[None, :]
    cr, sr, cc, sc = jnp.cos(ar), jnp.sin(ar), jnp.cos(ac), jnp.sin(ac)
    z16 = jnp.zeros_like(sr)
    cos_b = jnp.concatenate([cr, cr, cc, cc], axis=1)
    sp_b = jnp.concatenate([z16, sr, z16, sc], axis=1)
    sm_b = jnp.concatenate([-sr, z16, -sc, z16], axis=1)
    heads_per_tile = LANES // HEAD_DIM
    parts = [jnp.tile(t, (1, heads_per_tile)) for t in (cos_a, sp_a, sm_a, cos_b, sp_b, sm_b)]
    return jnp.concatenate(parts, axis=1)


def _rope_chunk(x, tab_ref, base, half):
    cos = tab_ref[:, base:base + LANES]
    sp = tab_ref[:, base + LANES:base + 2 * LANES]
    sm = tab_ref[:, base + 2 * LANES:base + 3 * LANES]
    return x * cos + pltpu.roll(x, half, 1) * sp + pltpu.roll(x, LANES - half, 1) * sm


def _head_mean_sq(acc, blk_ref):
    sq = acc * acc
    hi = sq.astype(BF16)
    lo = (sq - hi.astype(F32)).astype(BF16)
    blk = blk_ref[...]
    bw = blk.shape[0]
    parts = []
    for c in range(acc.shape[1] // bw):
        sl = slice(c * bw, (c + 1) * bw)
        parts.append(jnp.dot(hi[:, sl], blk, preferred_element_type=F32)
                     + jnp.dot(lo[:, sl], blk, preferred_element_type=F32))
    return parts[0] if len(parts) == 1 else jnp.concatenate(parts, axis=1)


def _store_by_class(out_refs, chunk_idx, chunk, stage_ref):
    g, half = divmod(chunk_idx, A_GROUP_WIDTH // LANES)
    dil = A_DILATIONS[g]
    lanes = slice(half * LANES, (half + 1) * LANES)
    if dil == 1:
        out_refs[g][0, :, lanes] = chunk.astype(BF16)
        return
    tm = chunk.shape[0]
    stage_ref[...] = chunk
    for r in range(dil):
        out_refs[g][r, :, lanes] = stage_ref[pl.ds(r, tm // dil, stride=dil), :].astype(BF16)


def _in_proj_kernel(x_ref, tab_ref, gmix_ref, wqa_ref, wka_ref, wva_ref, wqb_ref, wkvb_ref,
                    wg_ref, gq_ref, gk_ref, blkq_ref, blkk_ref,
                    qa0_ref, qa1_ref, qa2_ref, ka0_ref, ka1_ref, ka2_ref, va0_ref, va1_ref, va2_ref,
                    qt_ref, kb_ref, vt_ref, gates_ref, stage_ref):
    x = x_ref[...]
    tm = x.shape[0]
    ms = jnp.mean(x * x, axis=-1, keepdims=True)
    h = (x * lax.rsqrt(ms + EPS) * gmix_ref[...]).astype(BF16)

    acc = jnp.dot(h, wqa_ref[...], preferred_element_type=F32)
    for c in range(A_WIDTH // LANES):
        roped = _rope_chunk(acc[:, c * LANES:(c + 1) * LANES], tab_ref, 0, A_ROT_DIM // 2) * Q_SCALE
        _store_by_class((qa0_ref, qa1_ref, qa2_ref), c, roped, stage_ref)

    acc = jnp.dot(h, wka_ref[...], preferred_element_type=F32)
    for c in range(A_WIDTH // LANES):
        roped = _rope_chunk(acc[:, c * LANES:(c + 1) * LANES], tab_ref, 0, A_ROT_DIM // 2)
        _store_by_class((ka0_ref, ka1_ref, ka2_ref), c, roped, stage_ref)

    acc = jnp.dot(h, wva_ref[...], preferred_element_type=F32)
    for c in range(A_WIDTH // LANES):
        _store_by_class((va0_ref, va1_ref, va2_ref), c, acc[:, c * LANES:(c + 1) * LANES], stage_ref)

    acc = jnp.dot(h, wqb_ref[...], preferred_element_type=F32)
    acc = acc * lax.rsqrt(_head_mean_sq(acc, blkq_ref) + EPS) * gq_ref[...]
    heads_per_chunk = LANES // HEAD_DIM
    for c in range(B_Q_WIDTH // LANES):
        roped = _rope_chunk(acc[:, c * LANES:(c + 1) * LANES], tab_ref, 3 * LANES, HEAD_DIM // 4) * Q_SCALE
        rt = roped.T.astype(BF16)
        for hh in range(heads_per_chunk):
            head = c * heads_per_chunk + hh
            g, hg = divmod(head, B_GROUP_HEADS)
            cols = slice(hg * tm, (hg + 1) * tm)
            qt_ref[g, g * HEAD_DIM:(g + 1) * HEAD_DIM, cols] = rt[hh * HEAD_DIM:(hh + 1) * HEAD_DIM, :]
            qt_ref[g, (1 - g) * HEAD_DIM:(2 - g) * HEAD_DIM, cols] = jnp.zeros((HEAD_DIM, tm), BF16)

    kv = jnp.dot(h, wkvb_ref[...], preferred_element_type=F32)
    acc = kv[:, :B_KV_WIDTH]
    acc = acc * lax.rsqrt(_head_mean_sq(acc, blkk_ref) + EPS) * gk_ref[...]
    kb_ref[...] = _rope_chunk(acc, tab_ref, 3 * LANES, HEAD_DIM // 4).astype(BF16)

    vt = kv[:, B_KV_WIDTH:].T.astype(BF16)
    for g in range(B_KV_HEADS):
        vt_ref[g] = vt[g * HEAD_DIM:(g + 1) * HEAD_DIM, :]

    gates_ref[...] = jax.nn.sigmoid(jnp.dot(h, wg_ref[...], preferred_element_type=F32)).astype(BF16)


def _in_proj(xf, tab, b, s, gmix, w_in, gq, gk):
    n = xf.shape[0]
    tm = ROW_TILE
    nts = s // tm
    c1, c2, c3 = A_WIDTH, 2 * A_WIDTH, 3 * A_WIDTH
    c4 = c3 + B_Q_WIDTH
    c5 = c4 + B_KV_WIDTH
    c6 = c5 + B_KV_WIDTH
    wb = w_in.astype(BF16)
    ws = [wb[:, :c1], wb[:, c1:c2], wb[:, c2:c3], wb[:, c3:c4], wb[:, c4:c6], wb[:, c6:]]

    def head_blockdiag(width):
        hid = jnp.arange(width) // HEAD_DIM
        return jnp.where(hid[:, None] == hid[None, :], 1.0 / HEAD_DIM, 0.0).astype(BF16)

    consts = [gmix.reshape(1, D_MODEL)] + ws + [
        jnp.tile(gq, B_Q_WIDTH // HEAD_DIM).reshape(1, B_Q_WIDTH),
        jnp.tile(gk, B_KV_WIDTH // HEAD_DIM).reshape(1, B_KV_WIDTH),
        head_blockdiag(min(B_Q_WIDTH, MXU_WIDTH)), head_blockdiag(B_KV_WIDTH)]

    a_shapes, a_specs = [], []
    for _ in range(3):
        for dil in A_DILATIONS:
            a_shapes.append(jax.ShapeDtypeStruct((b, dil, s // dil, A_GROUP_WIDTH), BF16))
            a_specs.append(pl.BlockSpec((None, dil, tm // dil, A_GROUP_WIDTH),
                                        lambda i: (i // nts, 0, i % nts, 0)))
    kpb = B_TK // tm
    b_shapes = [jax.ShapeDtypeStruct((b, B_KV_HEADS, nts, B_KV_WIDTH, B_GROUP_HEADS * tm), BF16),
                jax.ShapeDtypeStruct((n, B_KV_WIDTH), BF16),
                jax.ShapeDtypeStruct((b, B_KV_HEADS, s // B_TK, HEAD_DIM, B_TK), BF16),
                jax.ShapeDtypeStruct((n, GATE_WIDTH), BF16)]
    b_specs = [pl.BlockSpec((None, B_KV_HEADS, None, B_KV_WIDTH, B_GROUP_HEADS * tm),
                            lambda i: (i // nts, 0, i % nts, 0, 0)),
               pl.BlockSpec((tm, B_KV_WIDTH), lambda i: (i, 0)),
               pl.BlockSpec((None, B_KV_HEADS, None, HEAD_DIM, tm),
                            lambda i: (i // nts, 0, (i % nts) // kpb, 0, i % kpb)),
               pl.BlockSpec((tm, GATE_WIDTH), lambda i: (i, 0))]

    return pl.pallas_call(
        _in_proj_kernel,
        out_shape=a_shapes + b_shapes,
        grid=(n // tm,),
        in_specs=[pl.BlockSpec((tm, D_MODEL), lambda i: (i, 0)),
                  pl.BlockSpec((tm, tab.shape[1]), lambda i: (i % nts, 0))] + [_full_spec(a) for a in consts],
        out_specs=a_specs + b_specs,
        scratch_shapes=[pltpu.VMEM((tm, LANES), F32)],
        compiler_params=_cparams(("parallel",)),
        name="in_proj",
    )(xf, tab, *consts)


def _band_bias():
    qi = jnp.arange(A_SUB)[:, None]
    kj = jnp.arange(A_SUB + 2 * A_RADIUS)[None, :]
    band = jnp.abs(kj - A_RADIUS - qi) <= A_RADIUS
    first = band & (kj >= A_RADIUS)
    last = band & (kj < A_SUB + A_RADIUS)
    return jnp.where(jnp.stack([band, first, last]), 0.0, NEG_INF).astype(F32)


def _attn_a_kernel(bias_ref, q_ref, kp_ref, kc_ref, kn_ref, vp_ref, vc_ref, vn_ref, o_ref, lse_ref):
    classes, tq, _ = q_ref.shape
    nsub = tq // A_SUB
    i = pl.program_id(2)
    last_tile = pl.num_programs(2) - 1
    first_head = lax.broadcasted_iota(jnp.int32, (1, LANES), 1) < HEAD_DIM
    nt_dims = (((1,), (1,)), ((), ()))

    def window(p_ref, c_ref, n_ref, c, u, cols):
        lo, hi = u * A_SUB - A_RADIUS, (u + 1) * A_SUB + A_RADIUS
        parts = [p_ref[c, :, cols]] if lo < 0 else []
        parts.append(c_ref[c, max(lo, 0):min(hi, tq), cols])
        if hi > tq:
            parts.append(n_ref[c, :, cols])
        return parts[0] if len(parts) == 1 else jnp.concatenate(parts, axis=0)

    for c in range(classes):
        for u in range(nsub):
            if u == 0:
                bias = bias_ref[jnp.where(i == 0, 1, 0)]
            elif u == nsub - 1:
                bias = bias_ref[jnp.where(i == last_tile, 2, 0)]
            else:
                bias = bias_ref[0]
            rows = slice(u * A_SUB, (u + 1) * A_SUB)
            for pair in range(A_GROUP_WIDTH // LANES):
                cols = slice(pair * LANES, (pair + 1) * LANES)
                k = window(kp_ref, kc_ref, kn_ref, c, u, cols)
                v = window(vp_ref, vc_ref, vn_ref, c, u, cols)
                q = q_ref[c, rows, cols]
                outs, lses = [], []
                for hh in range(LANES // HEAD_DIM):
                    mine = first_head if hh == 0 else jnp.logical_not(first_head)
                    qm = jnp.where(mine, q, jnp.zeros_like(q))
                    sc = lax.dot_general(qm, k, nt_dims, preferred_element_type=F32) + bias
                    m = jnp.max(sc, axis=1, keepdims=True)
                    p = jnp.exp2(sc - m)
                    den = jnp.sum(p, axis=1, keepdims=True)
                    outs.append(jnp.dot(p.astype(BF16), v, preferred_element_type=F32) / den)
                    lses.append(m * LN2 + jnp.log(den))
                o_ref[c, rows, cols] = jnp.where(first_head, outs[0], outs[1])
                lse_ref[c, rows, cols] = jnp.where(first_head, lses[0], lses[1])


def _attn_a(q, k, v, gi):
    b, dil, cl, _ = q.shape
    tq = min(A_TQ, cl)
    cb = min(dil, A_TQ // tq)
    assert tq >= 2 * A_SUB and A_SUB == 2 * A_RADIUS and dil % cb == 0
    hb = tq // A_RADIUS
    n_halo = cl // A_RADIUS
    bias = _band_bias()
    main = pl.BlockSpec((None, cb, tq, A_GROUP_WIDTH), lambda bi, r, i: (bi, r, i, 0))
    prev = pl.BlockSpec((None, cb, A_RADIUS, A_GROUP_WIDTH),
                        lambda bi, r, i: (bi, r, jnp.maximum(i * hb - 1, 0), 0))
    nxt = pl.BlockSpec((None, cb, A_RADIUS, A_GROUP_WIDTH),
                       lambda bi, r, i: (bi, r, jnp.minimum((i + 1) * hb, n_halo - 1), 0))
    return pl.pallas_call(
        _attn_a_kernel,
        out_shape=[jax.ShapeDtypeStruct((b, dil, cl, A_GROUP_WIDTH), F32)] * 2,
        grid=(b, dil // cb, cl // tq),
        in_specs=[_full_spec(bias), main, prev, main, nxt, prev, main, nxt],
        out_specs=[main, main],
        compiler_params=_cparams(("parallel", "parallel", "parallel")),
        name=f"attn_a{gi}",
    )(bias, q, k, k, k, v, v, v)


def _flash_b_kernel(qt_ref, k_ref, vt_ref, o_ref, acc_sc, l_sc, m_sc, p0_sc, p1_sc, *, online):
    nk, tk, _ = k_ref.shape
    width = qt_ref.shape[1]
    tq = width // B_GROUP_HEADS
    acc_sc[...] = jnp.zeros(acc_sc.shape, F32)
    l_sc[...] = jnp.zeros(l_sc.shape, F32)

    def scores(j):
        return jnp.dot(k_ref[j], qt_ref[...], preferred_element_type=F32)

    def column_sums(p):
        return p.reshape(tk // SUBLANES, SUBLANES, width).sum(axis=0)

    def accumulate(j, pb):
        vt = vt_ref[j]
        for h in range(B_GROUP_HEADS):
            cols = slice(h * tq, (h + 1) * tq)
            acc_sc[:, cols] += jnp.dot(vt, pb[:, cols], preferred_element_type=F32)

    if online:
        m_sc[...] = jnp.full(m_sc.shape, NEG_INF, F32)

        def body(j, carry):
            sc = scores(j)
            m_old = m_sc[...]
            m_new = jnp.maximum(m_old, jnp.max(sc, axis=0, keepdims=True))
            alpha = jnp.exp2(m_old - m_new)
            m_sc[...] = m_new
            p = jnp.exp2(sc - m_new)
            l_sc[...] = alpha * l_sc[...] + column_sums(p)
            acc_sc[...] = alpha * acc_sc[...]
            accumulate(j, p.astype(BF16))
            return carry

        lax.fori_loop(0, nk, body, 0)
    else:
        def exponentials(sc, p_out):
            p = jnp.exp2(sc)
            l_sc[...] += column_sums(p)
            p_out[...] = p.astype(BF16)

        def stage(j, p_in, p_out):
            sc = scores(j)
            accumulate(j - 1, p_in[...])
            exponentials(sc, p_out)

        exponentials(scores(0), p0_sc)

        def body(jj, carry):
            stage(2 * jj + 1, p0_sc, p1_sc)
            stage(2 * jj + 2, p1_sc, p0_sc)
            return carry

        lax.fori_loop(0, nk // 2 - 1, body, 0)
        stage(nk - 1, p0_sc, p1_sc)
        accumulate(nk - 1, p1_sc[...])
    o = acc_sc[...] / jnp.sum(l_sc[...], axis=0, keepdims=True)
    o = jnp.concatenate([o[:, h * tq:(h + 1) * tq] for h in range(B_GROUP_HEADS)], axis=0)
    o_ref[...] = o.T.astype(o_ref.dtype)


def _flash_b(qt, kb, vt, online):
    b, _, nq, _, width = qt.shape
    tq = width // B_GROUP_HEADS
    _, nk, tk, _ = kb.shape
    return pl.pallas_call(
        functools.partial(_flash_b_kernel, online=online),
        out_shape=jax.ShapeDtypeStruct((b, B_KV_HEADS, nq * tq, B_GROUP_WIDTH), BF16),
        grid=(b, B_KV_HEADS, nq),
        in_specs=[pl.BlockSpec((None, None, None, B_KV_WIDTH, width), lambda bi, g, i: (bi, g, i, 0, 0)),
                  pl.BlockSpec((None, nk, tk, B_KV_WIDTH), lambda bi, g, i: (bi, 0, 0, 0)),
                  pl.BlockSpec((None, None, nk, HEAD_DIM, tk), lambda bi, g, i: (bi, g, 0, 0, 0))],
        out_specs=pl.BlockSpec((None, None, tq, B_GROUP_WIDTH), lambda bi, g, i: (bi, g, i, 0)),
        scratch_shapes=[pltpu.VMEM((HEAD_DIM, width), F32), pltpu.VMEM((SUBLANES, width), F32),
                        pltpu.VMEM((1, width), F32),
                        pltpu.VMEM((tk, width), BF16), pltpu.VMEM((tk, width), BF16)],
        compiler_params=_cparams(("parallel", "parallel", "parallel")),
        name="flash_b_online" if online else "flash_b",
    )(qt, kb, vt)


def _stage_classes(blk_ref, stage_ref):
    dil, rows, width = blk_ref.shape
    if dil == 1:
        return
    for c in range(width // LANES):
        for r in range(dil):
            stage_ref.at[c][pl.ds(r, rows, stride=dil), :] = blk_ref[r, :, c * LANES:(c + 1) * LANES]


def _token_rows(blk_ref, stage_ref):
    if blk_ref.shape[0] == 1:
        return blk_ref[0]
    return jnp.concatenate([stage_ref[c] for c in range(stage_ref.shape[0])], axis=1)


def _post_attn_kernel(x_ref, o0_ref, o1_ref, o2_ref, l0_ref, l1_ref, l2_ref, ob0_ref, ob1_ref, gates_ref,
                      wa_ref, wb0_ref, wb1_ref, wout_ref, gffn_ref, wrh_ref, wrl_ref,
                      x1_ref, h2_ref, aff_ref, so1, so2, sl1, sl2):
    for blk, stage in ((l1_ref, sl1), (l2_ref, sl2), (o1_ref, so1), (o2_ref, so2)):
        _stage_classes(blk, stage)
    nt = (((1,), (1,)), ((), ()))
    l0 = _token_rows(l0_ref, None)
    l1 = _token_rows(l1_ref, sl1)
    l2 = _token_rows(l2_ref, sl2)
    m = jnp.maximum(jnp.maximum(l0, l1), l2)
    e0, e1, e2 = jnp.exp(l0 - m), jnp.exp(l1 - m), jnp.exp(l2 - m)
    oa = (e0 * _token_rows(o0_ref, None) + e1 * _token_rows(o1_ref, so1)
          + e2 * _token_rows(o2_ref, so2)) / (e0 + e1 + e2)
    ya = jnp.dot(oa.astype(BF16), wa_ref[...], preferred_element_type=F32)
    yb = (jnp.dot(ob0_ref[...], wb0_ref[...], preferred_element_type=F32)
          + jnp.dot(ob1_ref[...], wb1_ref[...], preferred_element_type=F32))
    ga = gates_ref[:, :D_MODEL].astype(F32)
    gb = gates_ref[:, D_MODEL:].astype(F32)
    z = (ga * ya + gb * yb).astype(BF16)
    x1 = x_ref[...] + jnp.dot(z, wout_ref[...], preferred_element_type=F32)
    x1_ref[...] = x1
    ms = jnp.mean(x1 * x1, axis=-1, keepdims=True)
    h2 = x1 * lax.rsqrt(ms + EPS) * gffn_ref[...]
    hi = h2.astype(BF16)
    lo = (h2 - hi.astype(F32)).astype(BF16)
    h2_ref[...] = hi
    both = lax.dot_general(wrl_ref[...], hi, nt, preferred_element_type=F32)
    logits = (both[:N_EXPERTS] + both[N_EXPERTS:]
              + lax.dot_general(wrh_ref[...], lo, nt, preferred_element_type=F32))
    logits = logits - jnp.max(logits, axis=0, keepdims=True)
    e = jnp.exp(logits)
    aff = e / jnp.sum(e, axis=0, keepdims=True)
    for j in range(aff_ref.shape[0]):
        aff_ref[j] = aff[:, j * LANES:(j + 1) * LANES]


def _post_attn(xf, oas, lses, ob, gates, b, s, w_a, w_b, w_out, g_ffn, w_router):
    n = xf.shape[0]
    tm = ROW_TILE
    nts = s // tm
    wa = w_a.astype(BF16)
    wb = w_b.astype(BF16)
    wb0, wb1 = wb[:B_GROUP_WIDTH], wb[B_GROUP_WIDTH:]
    wout = w_out.astype(BF16)
    wrt = w_router.T
    wrh = wrt.astype(BF16)
    wrl = jnp.concatenate([wrh, (wrt - wrh.astype(F32)).astype(BF16)], axis=0)
    consts = [wa, wb0, wb1, wout, g_ffn.reshape(1, D_MODEL), wrh, wrl]

    def rows(w):
        return pl.BlockSpec((tm, w), lambda i: (i, 0))

    def class_spec(dil):
        return pl.BlockSpec((None, dil, tm // dil, A_GROUP_WIDTH), lambda i: (i // nts, 0, i % nts, 0))

    def ob_spec(g):
        return pl.BlockSpec((None, None, tm, B_GROUP_WIDTH), lambda i: (i // nts, g, i % nts, 0))

    a_specs = [class_spec(d) for d in A_DILATIONS]
    return pl.pallas_call(
        _post_attn_kernel,
        out_shape=[jax.ShapeDtypeStruct((n, D_MODEL), F32),
                   jax.ShapeDtypeStruct((n, D_MODEL), BF16),
                   jax.ShapeDtypeStruct((n // LANES, N_EXPERTS, LANES), F32)],
        grid=(n // tm,),
        in_specs=[rows(D_MODEL)] + a_specs + a_specs + [ob_spec(0), ob_spec(1), rows(GATE_WIDTH)]
                 + [_full_spec(a) for a in consts],
        out_specs=[rows(D_MODEL), rows(D_MODEL),
                   pl.BlockSpec((tm // LANES, N_EXPERTS, LANES), lambda i: (i, 0, 0))],
        scratch_shapes=[pltpu.VMEM((A_GROUP_WIDTH // LANES, tm, LANES), F32)] * 4,
        compiler_params=_cparams(("parallel",)),
        name="post_attn",
    )(xf, *oas, *lses, ob, ob, gates, *consts)


def _route_kernel(aff_ref, tri_ref, posl_ref, starts_ref, sel_sc, *, cap, idx_bits, tiles_per_block):
    nlt = aff_ref.shape[0]
    capf = float(cap)

    def count(mask):
        part = jnp.sum(jnp.where(mask, 1.0, 0.0), axis=0)
        return jnp.sum(part, axis=1, keepdims=True)[None]

    def value_body(i, tau):
        bits = pltpu.bitcast(aff_ref[...], jnp.int32)
        cand = tau | lax.shift_left(jnp.int32(1), 30 - i)
        return jnp.where(count(bits >= cand) >= capf, cand, tau)

    tau = lax.fori_loop(0, 31, value_body, jnp.zeros((1, N_EXPERTS, 1), jnp.int32))
    bits = pltpu.bitcast(aff_ref[...], jnp.int32)
    need = capf - count(bits > tau)
    tok = (lax.broadcasted_iota(jnp.int32, bits.shape, 0) * LANES
           + lax.broadcasted_iota(jnp.int32, bits.shape, 2))

    def index_body(i, last):
        b = pltpu.bitcast(aff_ref[...], jnp.int32)
        cand = last | lax.shift_left(jnp.int32(1), idx_bits - 1 - i)
        return jnp.where(count((b == tau) & (tok < cand)) < need, cand, last)

    last = lax.fori_loop(0, idx_bits, index_body, jnp.zeros((1, N_EXPERTS, 1), jnp.int32))
    sel_sc[...] = jnp.where((bits > tau) | ((bits == tau) & (tok <= last)), 1.0, 0.0)

    sel = sel_sc[...]
    sb = sel.reshape(nlt * N_EXPERTS, LANES).astype(BF16)
    tri = tri_ref[...]
    within = jnp.dot(sb, tri, preferred_element_type=F32).reshape(nlt, N_EXPERTS, LANES)
    total = jnp.dot(sb, jnp.ones((LANES, LANES), BF16),
                    preferred_element_type=F32).reshape(nlt, N_EXPERTS, LANES)
    upto = total
    step = 1
    while step < nlt:
        upto = upto + jnp.concatenate([jnp.zeros((step, N_EXPERTS, LANES), F32), upto[:nlt - step]], axis=0)
        step *= 2
    before = upto - total
    posl_ref[...] = jnp.where(sel > 0, before + within - sel, -1.0).astype(jnp.int32)
    starts_ref[...] = before.reshape(nlt // tiles_per_block, tiles_per_block, N_EXPERTS, LANES)[:, 0].astype(jnp.int32)


def _route(aff3, cap):
    nlt = aff3.shape[0]
    n = nlt * LANES
    tpb = MOE_T // LANES
    tri = (jnp.arange(LANES)[:, None] <= jnp.arange(LANES)[None, :]).astype(BF16)
    vm = pl.BlockSpec(memory_space=pltpu.VMEM)
    return pl.pallas_call(
        functools.partial(_route_kernel, cap=cap, idx_bits=(n - 1).bit_length(), tiles_per_block=tpb),
        out_shape=[jax.ShapeDtypeStruct((nlt, N_EXPERTS, LANES), jnp.int32),
                   jax.ShapeDtypeStruct((n // MOE_T, N_EXPERTS, LANES), jnp.int32)],
        in_specs=[vm, vm],
        out_specs=[vm, vm],
        scratch_shapes=[pltpu.VMEM((nlt, N_EXPERTS, LANES), F32)],
        compiler_params=pltpu.CompilerParams(vmem_limit_bytes=VMEM_LIMIT),
        name="route",
    )(aff3, tri)


def _block_slots(s_ref, i, nt):
    return [(s_ref[e * (nt + 1) + i], s_ref[e * (nt + 1) + i + 1]) for e in range(N_EXPERTS)]


def _align_down(x):
    return (x >> MOE_ALIGN_SHIFT) << MOE_ALIGN_SHIFT


def _moe_gather_kernel(s_ref, aff_ref, posl_ref, h2_ref, xs_hbm, tm_ref,
                       haug_sc, pall_sc, stage_sc, carry_sc, pending_sc, sem, *, nt, cap):
    i = pl.program_id(0)
    nj = posl_ref.shape[0]
    ws = MOE_WS

    @pl.when(i == 0)
    def _():
        carry_sc[...] = jnp.zeros(carry_sc.shape, BF16)
        pending_sc[0] = 0
        pending_sc[1] = 0
        pad = xs_hbm.shape[1] - cap
        stage_sc[0] = jnp.zeros(stage_sc.shape[1:], BF16)
        fills = [pltpu.make_async_copy(stage_sc.at[0, pl.ds(0, pad)], xs_hbm.at[e, pl.ds(cap, pad)], sem)
                 for e in range(N_EXPERTS)]
        for cp in fills:
            cp.start()
        for cp in fills:
            cp.wait()

    for j in range(nj):
        pj = posl_ref[j]
        gj = jnp.where(pj >= 0, aff_ref[j], 0.0)
        blk = jnp.concatenate([gj, pj.astype(F32), gj, jnp.zeros((LANES - 3 * N_EXPERTS, LANES), F32)], axis=0)
        tm_ref[j * LANES:(j + 1) * LANES, :] = blk.T
    tmv = tm_ref[...]
    lane = lax.broadcasted_iota(jnp.int32, tmv.shape, 1)
    ghi = tmv.astype(BF16)
    glo = (tmv - ghi.astype(F32)).astype(BF16)
    zero = jnp.zeros_like(ghi)
    haug_sc[:, :D_MODEL] = h2_ref[...]
    haug_sc[:, D_MODEL:] = jnp.where(lane < N_EXPERTS, ghi,
                                     jnp.where((lane >= TM_GATE_LO) & (lane < TM_GATE_LO + N_EXPERTS), glo, zero))

    astarts, kks, offcs = [], [], []
    nw = jnp.int32(0)
    for st, en in _block_slots(s_ref, i, nt):
        ast = _align_down(st)
        span = _align_down(en) - ast
        kk = span >> MOE_WS_SHIFT
        astarts.append(ast)
        kks.append(kk)
        offcs.append(span - kk * ws)
        nw = jnp.maximum(nw, kk + 1)

    def wait_window():
        for _ in range(N_EXPERTS):
            pltpu.make_async_copy(stage_sc.at[0, pl.ds(0, ws)], xs_hbm.at[0, pl.ds(0, ws)], sem).wait()

    row = lax.broadcasted_iota(jnp.int32, (ws, LANES), 0)

    def window(k, carry):
        for e in range(N_EXPERTS):
            base = astarts[e] + k * ws
            for j in range(nj):
                hit = (posl_ref[j, e:e + 1, :] - base) == row
                pall_sc[e * ws:(e + 1) * ws, j * LANES:(j + 1) * LANES] = jnp.where(hit, 1.0, 0.0).astype(BF16)
        stage = stage_sc.at[pending_sc[1]]
        stage[...] = jnp.dot(pall_sc[...], haug_sc[...], preferred_element_type=F32).astype(BF16)

        @pl.when(k == 0)
        def _():
            for e in range(N_EXPERTS):
                stage[e * ws:e * ws + MOE_ALIGN, :] += carry_sc[e]

        for e in range(N_EXPERTS):
            @pl.when(k == kks[e])
            def _():
                carry_sc[e] = stage[pl.ds(pl.multiple_of(e * ws + offcs[e], MOE_ALIGN), MOE_ALIGN), :]

        @pl.when(pending_sc[0] == 1)
        def _():
            wait_window()

        for e in range(N_EXPERTS):
            base = pl.multiple_of(astarts[e] + k * ws, MOE_ALIGN)
            pltpu.make_async_copy(stage.at[pl.ds(e * ws, ws)], xs_hbm.at[e, pl.ds(base, ws)], sem).start()
        pending_sc[0] = 1
        pending_sc[1] = 1 - pending_sc[1]
        return carry

    lax.fori_loop(0, nw, window, 0)

    @pl.when((i == nt - 1) & (pending_sc[0] == 1))
    def _():
        wait_window()
        pending_sc[0] = 0


def _moe_gather(s_flat, aff3, posl, h2, cap):
    n = h2.shape[0]
    t = MOE_T
    nt = n // t
    cap_pad = cap + (t // MOE_WS + 1) * MOE_WS
    blk3 = pl.BlockSpec((t // LANES, N_EXPERTS, LANES), lambda i, s: (i, 0, 0))
    grid_spec = pltpu.PrefetchScalarGridSpec(
        num_scalar_prefetch=1,
        grid=(nt,),
        in_specs=[blk3, blk3, pl.BlockSpec((t, D_MODEL), lambda i, s: (i, 0))],
        out_specs=[pl.BlockSpec(memory_space=pl.ANY), pl.BlockSpec((t, LANES), lambda i, s: (i, 0))],
        scratch_shapes=[pltpu.VMEM((t, XS_WIDTH), BF16),
                        pltpu.VMEM((N_EXPERTS * MOE_WS, t), BF16),
                        pltpu.VMEM((2, N_EXPERTS * MOE_WS, XS_WIDTH), BF16),
                        pltpu.VMEM((N_EXPERTS, MOE_ALIGN, XS_WIDTH), BF16),
                        pltpu.SMEM((2,), jnp.int32),
                        pltpu.SemaphoreType.DMA(())])
    return pl.pallas_call(
        functools.partial(_moe_gather_kernel, nt=nt, cap=cap),
        out_shape=[jax.ShapeDtypeStruct((N_EXPERTS, cap_pad, XS_WIDTH), BF16),
                   jax.ShapeDtypeStruct((n, LANES), F32)],
        grid_spec=grid_spec,
        compiler_params=_cparams(("arbitrary",)),
        name="moe_gather",
    )(s_flat, aff3, posl, h2)


def _moe_ffn_kernel(xs_ref, w1_ref, w3_ref, w2_ref, ye_ref):
    xe = xs_ref[:, :D_MODEL]
    route = xs_ref[:, D_MODEL:].astype(F32)
    lane = lax.broadcasted_iota(jnp.int32, route.shape, 1)
    e = pl.program_id(0)
    gate = jnp.sum(jnp.where((lane == e) | (lane == e + TM_GATE_LO), route, 0.0), axis=1, keepdims=True)
    a = jnp.dot(xe, w1_ref[...], preferred_element_type=F32)
    u = jnp.dot(xe, w3_ref[...], preferred_element_type=F32)
    hid = (a * jax.nn.sigmoid(a) * u).astype(BF16)
    ye = jnp.dot(hid, w2_ref[...], preferred_element_type=F32) * gate
    ye_ref[...] = ye.astype(ye_ref.dtype)


def _moe_ffn(xs, cap, w1, w3, w2):
    e = xs.shape[0]
    tc = min(FFN_TC, cap)
    wspec = pl.BlockSpec((None, D_MODEL, D_MODEL), lambda ei, ci: (ei, 0, 0))
    return pl.pallas_call(
        _moe_ffn_kernel,
        out_shape=jax.ShapeDtypeStruct((e, cap, D_MODEL), BF16),
        grid=(e, cap // tc),
        in_specs=[pl.BlockSpec((None, tc, XS_WIDTH), lambda ei, ci: (ei, ci, 0)), wspec, wspec, wspec],
        out_specs=pl.BlockSpec((None, tc, D_MODEL), lambda ei, ci: (ei, ci, 0)),
        compiler_params=_cparams(("parallel", "arbitrary")),
        name="moe_ffn",
    )(xs, w1, w3, w2)


def _final_kernel(s_ref, x1_ref, tm_ref, p_ref, ye_hbm, wple_ref, wpg_ref, gple_ref, gfin_ref, expand_ref, out_ref,
                  buf_sc, pall_sc, y_sc, sem, *, nt, cap):
    i = pl.program_id(0)
    ws = MOE_WS
    t = x1_ref.shape[0]
    slot = i % 2

    def window_rows(blk, k):
        out = []
        for st, _ in _block_slots(s_ref, blk, nt):
            lo = _align_down(st) + k * ws
            out.append((lo, pl.multiple_of(jnp.minimum(lo, cap - ws), MOE_ALIGN)))
        return out

    def copies(rows, buf, which):
        return [pltpu.make_async_copy(ye_hbm.at[e, pl.ds(base, ws)], buf_sc.at[buf, pl.ds(e * ws, ws)], sem.at[which])
                for e, (_, base) in enumerate(rows)]

    @pl.when(i == 0)
    def _():
        for cp in copies(window_rows(0, 0), 0, 0):
            cp.start()

    @pl.when(i + 1 < nt)
    def _():
        for cp in copies(window_rows(i + 1, 0), 1 - slot, 1 - slot):
            cp.start()

    nw = jnp.int32(0)
    for st, en in _block_slots(s_ref, i, nt):
        nw = jnp.maximum(nw, (en - _align_down(st) + ws - 1) >> MOE_WS_SHIFT)
    lane = lax.broadcasted_iota(jnp.int32, (1, LANES), 1)
    col = (lax.broadcasted_iota(jnp.int32, (1, N_EXPERTS * ws), 1) & (ws - 1)).astype(F32)

    def onehot(rows):
        basev = jnp.zeros((1, LANES), F32)
        lov = jnp.full((1, LANES), float(2 ** 30), F32)
        for e, (lo, base) in enumerate(rows):
            basev = jnp.where(lane == TM_POS + e, base.astype(F32), basev)
            lov = jnp.where(lane == TM_POS + e, lo.astype(F32), lov)
        tmv = tm_ref[...]
        off = tmv - basev
        off = jnp.where((tmv >= lov) & (off >= 0.0) & (off < float(ws)), off, -1.0)
        spread = jnp.dot(off.astype(BF16), expand_ref[...], preferred_element_type=F32)
        pall_sc[...] = jnp.where(spread == col, 1.0, 0.0).astype(BF16)

    rows0 = window_rows(i, 0)
    onehot(rows0)
    for cp in copies(rows0, slot, slot):
        cp.wait()
    y_sc[...] = jnp.dot(pall_sc[...], buf_sc[slot], preferred_element_type=F32)

    def window(k, carry):
        rows = window_rows(i, k)
        cps = copies(rows, slot, slot)
        for cp in cps:
            cp.start()
        onehot(rows)
        for cp in cps:
            cp.wait()
        y_sc[...] += jnp.dot(pall_sc[...], buf_sc[slot], preferred_element_type=F32)
        return carry

    lax.fori_loop(1, nw, window, 0)

    x2 = x1_ref[...] + y_sc[...]
    ms = jnp.mean(x2 * x2, axis=-1, keepdims=True)
    hp = (x2 * lax.rsqrt(ms + EPS) * gple_ref[...]).astype(BF16)
    gt = jax.nn.sigmoid(jnp.dot(hp, wpg_ref[...], preferred_element_type=F32))
    emb = jnp.dot(p_ref[...].astype(BF16), wple_ref[...], preferred_element_type=F32)
    x3 = x2 + emb * gt
    ms3 = jnp.mean(x3 * x3, axis=-1, keepdims=True)
    out_ref[...] = x3 * lax.rsqrt(ms3 + EPS) * gfin_ref[...]


def _final(s_flat, x1, tm, pf, ye, w_ple, w_ple_gate, g_ple, g_final):
    n = x1.shape[0]
    t = MOE_T
    nt = n // t
    cap = ye.shape[1]
    src = jnp.arange(LANES)[:, None] - TM_POS
    dst = jnp.arange(N_EXPERTS * MOE_WS)[None, :] >> MOE_WS_SHIFT
    expand = (src == dst).astype(BF16)
    consts = [w_ple.astype(BF16), w_ple_gate.astype(BF16), g_ple.reshape(1, D_MODEL), g_final.reshape(1, D_MODEL),
              expand]

    def rows(w):
        return pl.BlockSpec((t, w), lambda i, s: (i, 0))

    grid_spec = pltpu.PrefetchScalarGridSpec(
        num_scalar_prefetch=1,
        grid=(nt,),
        in_specs=[rows(D_MODEL), rows(LANES), rows(PLE_DIM), pl.BlockSpec(memory_space=pl.ANY)]
                 + [_full_spec(a) for a in consts],
        out_specs=rows(D_MODEL),
        scratch_shapes=[pltpu.VMEM((2, N_EXPERTS * MOE_WS, D_MODEL), BF16),
                        pltpu.VMEM((t, N_EXPERTS * MOE_WS), BF16),
                        pltpu.VMEM((t, D_MODEL), F32),
                        pltpu.SemaphoreType.DMA((2,))])
    return pl.pallas_call(
        functools.partial(_final_kernel, nt=nt, cap=cap),
        out_shape=jax.ShapeDtypeStruct((n, D_MODEL), F32),
        grid_spec=grid_spec,
        compiler_params=_cparams(("arbitrary",)),
        name="final",
    )(s_flat, x1, tm, pf, ye, *consts)


def _run_group(x, p, g_mix, w_in, g_qn, g_kn, w_a_proj, w_b_proj, w_out, g_ffn, w_router,
               w1, w3, w2, g_ple, w_ple, w_ple_gate, g_final):
    b, s, _ = x.shape
    n = b * s
    assert s % B_TK == 0 and s % (A_DILATIONS[-1] * A_RADIUS) == 0
    xf = x.reshape(n, D_MODEL)
    tab = _rope_tables(s)
    outs = _in_proj(xf, tab, b, s, g_mix, w_in, g_qn, g_kn)
    qas, kas, vas = outs[0:3], outs[3:6], outs[6:9]
    qt, kb, vt, gates = outs[9:]

    oas, lses = [], []
    for gi, (window, dil) in enumerate(A_PATTERNS):
        assert window // (2 * dil) == A_RADIUS
        o, lse = _attn_a(qas[gi], kas[gi], vas[gi], gi)
        oas.append(o)
        lses.append(lse)

    score_bound = (HEAD_DIM * Q_SCALE * 1.02) * jnp.max(jnp.abs(g_qn)) * jnp.max(jnp.abs(g_kn))
    kb4 = kb.reshape(b, s // B_TK, B_TK, B_KV_WIDTH)
    ob = lax.cond(score_bound <= B_UNSHIFTED_SCORE_MAX,
                  lambda: _flash_b(qt, kb4, vt, online=False),
                  lambda: _flash_b(qt, kb4, vt, online=True))

    x1, h2, aff3 = _post_attn(xf, oas, lses, ob, gates, b, s, w_a_proj, w_b_proj, w_out, g_ffn, w_router)

    cap = CAPACITY_FACTOR * n // N_EXPERTS
    assert cap % MOE_WS == 0 and cap >= MOE_WS
    posl, starts = _route(aff3, cap)
    s_flat = jnp.concatenate([starts[:, :, 0].T, jnp.full((N_EXPERTS, 1), cap, jnp.int32)], axis=1).reshape(-1)
    xs, tm = _moe_gather(s_flat, aff3, posl, h2, cap)
    ye = _moe_ffn(xs, cap, w1, w3, w2)
    out = _final(s_flat, x1, tm, p.reshape(n, PLE_DIM), ye, w_ple, w_ple_gate, g_ple, g_final)
    return out.reshape(b, s, D_MODEL)


def kernel(x_prompt, x_sample, p_prompt, p_sample, g_mix, w_in, g_qn, g_kn, w_a_proj, w_b_proj, w_out, g_ffn,
           w_router, w_exp_gate, w_exp_up, w_exp_down, g_ple, w_ple, w_ple_gate, g_final):
    assert g_mix.shape[0] == 1, "single layer"
    w1 = w_exp_gate[0].astype(BF16)
    w3 = w_exp_up[0].astype(BF16)
    w2 = w_exp_down[0].astype(BF16)
    args = (g_mix[0], w_in[0], g_qn[0], g_kn[0], w_a_proj[0], w_b_proj[0], w_out[0], g_ffn[0], w_router[0],
            w1, w3, w2, g_ple[0], w_ple[0], w_ple_gate[0], g_final)
    y_prompt = _run_group(x_prompt, p_prompt[0], *args)
    y_sample = _run_group(x_sample, p_sample[0], *args)
    return (y_prompt, y_sample)
```
